```python
import math
import jax, jax.numpy as jnp
from jax import lax
import numpy as np

D_MODEL = 2048
BATCH = 4
SEQ = 2048
DEPTH = 1

PLE_DIM = 256
POOL_WIDTH = 1024
POOL_GROUPS = 4
POOL_GROUP_DIM = POOL_WIDTH // POOL_GROUPS
POOL_WINDOWS = (2, 4, 8, 16)
N_HEADS = 16
N_KV_HEADS = 2
HEAD_DIM = 64
ATTN_WIDTH = N_HEADS * HEAD_DIM
KV_WIDTH = N_KV_HEADS * HEAD_DIM
MIX_WIDTH = POOL_WIDTH + ATTN_WIDTH
IN_WIDTH = POOL_WIDTH + ATTN_WIDTH + 2 * KV_WIDTH
WINDOW = 128
ATTN_BLOCK = 128
RPE_BUCKETS = 32
RPE_MAX_EXACT = RPE_BUCKETS // 2
RPE_MAX_DISTANCE = 128
N_EXPERTS = 64
TOP_K = 8
N_EXPERT_GROUPS = 8
EXPERTS_PER_GROUP = N_EXPERTS // N_EXPERT_GROUPS
TOPK_GROUPS = 4
EXPERT_FF = 512
SHARED_FF = 512
ROUTED_SCALE = 2.5
MOE_ROW_BLOCK = 128
EPS = 1e-6

kernel_name = "hybrid_pool_swa_moe_block"


def rms_norm(x, g):
    xf = x.astype(jnp.float32)
    y = xf * lax.rsqrt(jnp.mean(xf * xf, axis=-1, keepdims=True) + EPS)
    return (y * g.astype(jnp.float32)).astype(x.dtype)


def pool_mixer(u, w_pool, scale):
    B, S, _ = u.shape
    uf = u.astype(jnp.float32).reshape(B, S, POOL_GROUPS, POOL_GROUP_DIM)
    t = jnp.arange(S)
    outs = []
    for gi, w in enumerate(POOL_WINDOWS):
        ug = uf[:, :, gi]
        c = lax.cumsum(ug, axis=1)
        c_shift = jnp.pad(c, ((0, 0), (w, 0), (0, 0)))[:, :S]
        count = jnp.minimum(t + 1, w).astype(jnp.float32)[None, :, None]
        pooled = (c - c_shift) / count - ug
        outs.append(jnp.einsum('bsc,cd->bsd', pooled.astype(u.dtype), w_pool[gi]))
    return jnp.concatenate(outs, axis=-1) * scale


def t5_causal_bucket(dist):
    n = jnp.maximum(dist, 0)
    nf = jnp.maximum(n, 1).astype(jnp.float32)
    large = RPE_MAX_EXACT + (jnp.log(nf / RPE_MAX_EXACT) / math.log(RPE_MAX_DISTANCE / RPE_MAX_EXACT)
                             * (RPE_BUCKETS - RPE_MAX_EXACT)).astype(jnp.int32)
    large = jnp.minimum(large, RPE_BUCKETS - 1)
    return jnp.where(n < RPE_MAX_EXACT, n, large)


def swa_attention(q, k, v, sinks, rpe_table):
    B, S, _ = q.shape
    NB = S // ATTN_BLOCK
    G = N_HEADS // N_KV_HEADS
    L = ATTN_BLOCK
    qb = q.astype(jnp.float32).reshape(B, NB, L, N_KV_HEADS, G, HEAD_DIM) * (HEAD_DIM ** -0.5)
    kb = k.astype(jnp.float32).reshape(B, NB, L, N_KV_HEADS, HEAD_DIM)
    vb = v.astype(jnp.float32).reshape(B, NB, L, N_KV_HEADS, HEAD_DIM)

    def band(a):
        prev = jnp.concatenate([jnp.zeros_like(a[:, :1]), a[:, :-1]], axis=1)
        return jnp.concatenate([prev, a], axis=2)

    kband, vband = band(kb), band(vb)
    logits = jnp.einsum('bnqkgd,bnskd->bnkgqs', qb, kband)

    dist = jnp.arange(L)[:, None] + L - jnp.arange(2 * L)[None, :]
    key_pos = jnp.arange(NB)[:, None] * L - L + jnp.arange(2 * L)[None, :]
    valid = ((dist >= 0) & (dist < WINDOW))[None] & (key_pos >= 0)[:, None, :]
    bias = rpe_table.astype(jnp.float32)[t5_causal_bucket(dist)]
    bias = bias.transpose(2, 0, 1).reshape(N_KV_HEADS, G, L, 2 * L)
    logits = jnp.where(valid[None, :, None, None], logits + bias[None, None], -jnp.inf)

    sink = sinks.astype(jnp.float32).reshape(N_KV_HEADS, G)[None, None, :, :, None, None]
    m = jnp.maximum(jnp.max(logits, axis=-1, keepdims=True), sink)
    pr = jnp.exp(logits - m)
    denom = jnp.sum(pr, axis=-1, keepdims=True) + jnp.exp(sink - m)
    out = jnp.einsum('bnkgqs,bnskd->bnqkgd', pr / denom, vband)
    return out.reshape(B, S, ATTN_WIDTH).astype(q.dtype)


def moe(h, w_router, router_bias, w_gate, w_up, w_down, ws_gate, ws_up, ws_down):
    B, S, D = h.shape
    T = B * S
    xf = h.reshape(T, D)
    scores = jax.nn.sigmoid(jnp.dot(xf.astype(jnp.float32), w_router.astype(jnp.float32)))
    biased = scores + router_bias.astype(jnp.float32)
    grouped = biased.reshape(T, N_EXPERT_GROUPS, EXPERTS_PER_GROUP)
    group_score = jnp.sum(lax.top_k(grouped, 2)[0], axis=-1)
    _, top_groups = lax.top_k(group_score, TOPK_GROUPS)
    group_mask = jnp.any(top_groups[:, :, None] == jnp.arange(N_EXPERT_GROUPS)[None, None, :], axis=1)
    masked = jnp.where(jnp.repeat(group_mask, EXPERTS_PER_GROUP, axis=1), biased, -jnp.inf)
    _, top_idx = lax.top_k(masked, TOP_K)
    top_w = jnp.take_along_axis(scores, top_idx, axis=-1)
    top_w = top_w / jnp.sum(top_w, axis=-1, keepdims=True) * ROUTED_SCALE

    A = T * TOP_K
    RB = MOE_ROW_BLOCK
    flat_e = top_idx.reshape(A)
    flat_tok = jnp.arange(A, dtype=jnp.int32) // TOP_K
    flat_w = top_w.reshape(A)
    order = jnp.argsort(flat_e)
    se = flat_e[order]
    counts = jnp.bincount(flat_e, length=N_EXPERTS)
    padded = (counts + RB - 1) // RB * RB
    offsets = jnp.cumsum(counts) - counts
    pad_ends = jnp.cumsum(padded)
    pad_offsets = pad_ends - padded
    dest = pad_offsets[se] + jnp.arange(A) - offsets[se]
    n_rows = A + N_EXPERTS * RB
    n_blocks = n_rows // RB
    row_tok = jnp.full((n_rows,), T, jnp.int32).at[dest].set(flat_tok[order])
    row_w = jnp.zeros((n_rows,), jnp.float32).at[dest].set(flat_w[order])
    block_e = jnp.minimum(jnp.searchsorted(pad_ends, jnp.arange(n_blocks) * RB, side='right'), N_EXPERTS - 1)
    x_pad = jnp.concatenate([xf, jnp.zeros((1, D), xf.dtype)], axis=0)

    def expert_block(args):
        tok, w, e = args
        xb = x_pad[tok]
        hb = jax.nn.silu(xb @ w_gate[e]) * (xb @ w_up[e])
        return (hb @ w_down[e]) * w[:, None].astype(xb.dtype)

    y_rows = lax.map(expert_block, (row_tok.reshape(n_blocks, RB), row_w.reshape(n_blocks, RB), block_e))
    routed = jax.ops.segment_sum(y_rows.reshape(n_rows, D), row_tok, num_segments=T + 1)[:T]
    shared = (jax.nn.silu(xf @ ws_gate) * (xf @ ws_up)) @ ws_down
    return (routed + shared).reshape(B, S, D)


def setup_inputs(seed: int = 0) -> dict:
    key = jax.random.key(seed)
    ks = jax.random.split(key, 24)
    f32 = jnp.float32
    nrm = lambda k, shape, s: jax.random.normal(k, shape, f32) * s
    L = DEPTH
    return {
        "x": nrm(ks[0], (BATCH, SEQ, D_MODEL), 1.0),
        "p": nrm(ks[1], (DEPTH, BATCH, SEQ, PLE_DIM), 1.0),
        "g_mix": 1.0 + nrm(ks[2], (L, D_MODEL), 0.02),
        "w_in": nrm(ks[3], (L, D_MODEL, IN_WIDTH), D_MODEL ** -0.5),
        "w_pool": nrm(ks[4], (L, POOL_GROUPS, POOL_GROUP_DIM, POOL_GROUP_DIM), POOL_GROUP_DIM ** -0.5),
        "pool_scale": 1.0 + nrm(ks[5], (L, POOL_WIDTH), 0.1),
        "attn_sinks": nrm(ks[6], (L, N_HEADS), 1.0),
        "rpe_table": nrm(ks[7], (RPE_BUCKETS, N_HEADS), 0.5),
        "w_out": nrm(ks[8], (L, MIX_WIDTH, D_MODEL), MIX_WIDTH ** -0.5),
        "g_ffn": 1.0 + nrm(ks[9], (L, D_MODEL), 0.02),
        "w_router": nrm(ks[10], (L, D_MODEL, N_EXPERTS), D_MODEL ** -0.5),
        "router_bias": nrm(ks[11], (L, N_EXPERTS), 0.01),
        "w_gate": nrm(ks[12], (L, N_EXPERTS, D_MODEL, EXPERT_FF), D_MODEL ** -0.5),
        "w_up": nrm(ks[13], (L, N_EXPERTS, D_MODEL, EXPERT_FF), D_MODEL ** -0.5),
        "w_down": nrm(ks[14], (L, N_EXPERTS, EXPERT_FF, D_MODEL), EXPERT_FF ** -0.5),
        "ws_gate": nrm(ks[15], (L, D_MODEL, SHARED_FF), D_MODEL ** -0.5),
        "ws_up": nrm(ks[16], (L, D_MODEL, SHARED_FF), D_MODEL ** -0.5),
        "ws_down": nrm(ks[17], (L, SHARED_FF, D_MODEL), SHARED_FF ** -0.5),
        "w_ple_proj": nrm(ks[18], (L, PLE_DIM, D_MODEL), PLE_DIM ** -0.5),
        "g_ple": 1.0 + nrm(ks[19], (L, D_MODEL), 0.02),
        "w_ple_gate": nrm(ks[20], (L, D_MODEL, D_MODEL), D_MODEL ** -0.5),
        "g_final": 1.0 + nrm(ks[21], (D_MODEL,), 0.02),
    }


def reference(x, p, g_mix, w_in, w_pool, pool_scale, attn_sinks, rpe_table, w_out, g_ffn,
              w_router, router_bias, w_gate, w_up, w_down, ws_gate, ws_up, ws_down,
              w_ple_proj, g_ple, w_ple_gate, g_final):
    h = x
    split_at = [POOL_WIDTH, POOL_WIDTH + ATTN_WIDTH, POOL_WIDTH + ATTN_WIDTH + KV_WIDTH]
    for i in range(DEPTH):
        a = rms_norm(h, g_mix[i])
        z = a @ w_in[i]
        u, q, k, v = jnp.split(z, split_at, axis=-1)
        y_pool = pool_mixer(u, w_pool[i], pool_scale[i])
        y_attn = swa_attention(q, k, v, attn_sinks[i], rpe_table)
        h = h + jnp.concatenate([y_pool, y_attn], axis=-1) @ w_out[i]
        h = h + moe(rms_norm(h, g_ffn[i]), w_router[i], router_bias[i], w_gate[i], w_up[i], w_down[i],
                    ws_gate[i], ws_up[i], ws_down[i])
        gate = jax.nn.sigmoid(h @ w_ple_gate[i])
        h = h + gate * rms_norm(p[i] @ w_ple_proj[i], g_ple[i])
    return rms_norm(h, g_final)
```

```python
import math

import numpy as np
import jax
import jax.numpy as jnp
from jax import lax
from jax.experimental import pallas as pl
from jax.experimental.pallas import tpu as pltpu

F32 = jnp.float32
BF16 = jnp.bfloat16
NEG_INF = float("-inf")

D_MODEL = 2048
BATCH = 4
SEQ = 2048
TOKENS = BATCH * SEQ
PLE_DIM = 256
POOL_WIDTH = 1024
POOL_GROUPS = 4
POOL_GROUP_DIM = POOL_WIDTH // POOL_GROUPS
POOL_WINDOWS = (2, 4, 8, 16)
N_HEADS = 16
N_KV_HEADS = 2
HEAD_DIM = 64
HEADS_PER_KV = N_HEADS // N_KV_HEADS
ATTN_WIDTH = N_HEADS * HEAD_DIM
KV_WIDTH = N_KV_HEADS * HEAD_DIM
IN_WIDTH = POOL_WIDTH + ATTN_WIDTH + 2 * KV_WIDTH
Q_OFF = POOL_WIDTH
K_OFF = POOL_WIDTH + ATTN_WIDTH
V_OFF = K_OFF + KV_WIDTH
ATTN_BLOCK = 128
WINDOW = 128
RPE_BUCKETS = 32
RPE_MAX_EXACT = RPE_BUCKETS // 2
RPE_MAX_DISTANCE = 128
N_EXPERTS = 64
TOP_K = 8
N_EXPERT_GROUPS = 8
EXPERTS_PER_GROUP = N_EXPERTS // N_EXPERT_GROUPS
TOPK_GROUPS = 4
EXPERT_FF = 512
SHARED_FF = 512
ROUTED_SCALE = 2.5
EPS = 1e-6

LANES = 128
SUBLANES = 8
VMEM_LIMIT_BYTES = 58 * 1024 * 1024

MIX_TOKENS = 256
POOL_HISTORY = 16
ROW_TILE = 256
N_ROWS = TOKENS * TOP_K + N_EXPERTS * ROW_TILE
N_ROW_TILES = N_ROWS // ROW_TILE
DISPATCH_TOKENS = 512
FINAL_TOKENS = 128
MIX_WIDTH = POOL_WIDTH + ATTN_WIDTH
ROW_CHUNKS = D_MODEL // LANES
GATHER_PITCH = 24


def _rms(x, g):
    return x * lax.rsqrt(jnp.mean(x * x, axis=-1, keepdims=True) + EPS) * g


def _sigmoid(x):
    return 1.0 / (1.0 + jnp.exp(-x))


def _dot(a, b):
    return jnp.dot(a, b, preferred_element_type=F32)


def _dot_nt(a, b, precision=None):
    return lax.dot_general(a, b, (((1,), (1,)), ((), ())), preferred_element_type=F32, precision=precision)


def _rpe_bucket_map():
    i = np.arange(ATTN_BLOCK)[:, None]
    j = np.arange(2 * ATTN_BLOCK)[None, :]
    dist = i + ATTN_BLOCK - j
    n = np.maximum(dist, 0)
    nf = np.maximum(n, 1).astype(np.float32)
    large = RPE_MAX_EXACT + (np.log(nf / np.float32(RPE_MAX_EXACT)) / np.float32(math.log(RPE_MAX_DISTANCE / RPE_MAX_EXACT))
                             * np.float32(RPE_BUCKETS - RPE_MAX_EXACT)).astype(np.int32)
    large = np.minimum(large, RPE_BUCKETS - 1)
    bucket = np.where(n < RPE_MAX_EXACT, n, large)
    valid = (dist >= 0) & (dist < WINDOW)
    return np.where(valid, bucket, -1).astype(np.int32)


def _mix_kernel(x_ref, gmix_ref, win_ref, wpool_ref, pscale_ref, sinks_ref, rpe_ref, bucket_ref,
                wout_ref, gffn_ref, wrt_ref, rbias_ref,
                h_ref, h2_ref, idx_ref, topw_ref, rank_ref, cnt_ref,
                ubuf, kbuf, vbuf, bias_buf, ybuf, cnt_acc):
    TS = MIX_TOKENS
    b = pl.program_id(0)
    s = pl.program_id(1)

    @pl.when((b == 0) & (s == 0))
    def _init():
        bucket = bucket_ref[...]
        for h in range(N_HEADS):
            acc = jnp.full((ATTN_BLOCK, 2 * ATTN_BLOCK), NEG_INF, F32)
            for bk in range(RPE_BUCKETS):
                acc = jnp.where(bucket == bk, rpe_ref[bk, h], acc)
            bias_buf[h] = acc
        cnt_acc[...] = jnp.zeros_like(cnt_acc)

    @pl.when(s == 0)
    def _reset_history():
        ubuf[0:POOL_HISTORY, :] = jnp.zeros((POOL_HISTORY, POOL_WIDTH), F32)
        kbuf[0:ATTN_BLOCK, :] = jnp.zeros((ATTN_BLOCK, KV_WIDTH), BF16)
        vbuf[0:ATTN_BLOCK, :] = jnp.zeros((ATTN_BLOCK, KV_WIDTH), BF16)

    x = x_ref[...]
    a = _rms(x, gmix_ref[...]).astype(BF16)
    z = _dot(a, win_ref[...])

    ubuf[POOL_HISTORY:POOL_HISTORY + TS, :] = z[:, 0:POOL_WIDTH]
    pos = s * TS + lax.broadcasted_iota(jnp.int32, (TS, 1), 0)
    for gi, w in enumerate(POOL_WINDOWS):
        c0, c1 = gi * POOL_GROUP_DIM, (gi + 1) * POOL_GROUP_DIM
        e = ubuf[:, c0:c1]
        shift = 1
        while shift < w:
            e = e + pltpu.roll(e, shift, axis=0)
            shift *= 2
        wsum = e[POOL_HISTORY:, :]
        count = jnp.minimum(pos + 1, w).astype(F32)
        pooled = wsum / count - z[:, c0:c1]
        yp = _dot(pooled.astype(BF16), wpool_ref[gi]) * pscale_ref[:, c0:c1]
        ybuf[:, c0:c1] = yp.astype(BF16)
    ubuf[0:POOL_HISTORY, :] = ubuf[TS:TS + POOL_HISTORY, :]

    kbuf[ATTN_BLOCK:ATTN_BLOCK + TS, :] = z[:, K_OFF:K_OFF + KV_WIDTH].astype(BF16)
    vbuf[ATTN_BLOCK:ATTN_BLOCK + TS, :] = z[:, V_OFF:V_OFF + KV_WIDTH].astype(BF16)
    lane = lax.broadcasted_iota(jnp.int32, (ATTN_BLOCK, LANES), 1)
    low_half = lane < HEAD_DIM
    col = lax.broadcasted_iota(jnp.int32, (1, 2 * ATTN_BLOCK), 1)
    first_mask = jnp.where((col < ATTN_BLOCK) & (s == 0), NEG_INF, 0.0).astype(F32)
    for sb in range(TS // ATTN_BLOCK):
        r0 = sb * ATTN_BLOCK
        kband = kbuf[r0:r0 + 2 * ATTN_BLOCK, :]
        vband = vbuf[r0:r0 + 2 * ATTN_BLOCK, :]
        for p in range(N_HEADS // 2):
            qp = z[r0:r0 + ATTN_BLOCK, Q_OFF + p * LANES:Q_OFF + (p + 1) * LANES] * (HEAD_DIM ** -0.5)
            qr = pltpu.roll(qp, HEAD_DIM, axis=1)
            kvh = (2 * p) // HEADS_PER_KV
            kv_lanes = low_half if kvh == 0 else jnp.logical_not(low_half)
            outs = []
            for par in range(2):
                h = 2 * p + par
                src = qp if par == kvh else qr
                qh = jnp.where(kv_lanes, src, 0.0).astype(BF16)
                lg = _dot_nt(qh, kband) + bias_buf[h]
                if sb == 0:
                    lg = lg + first_mask
                sink = sinks_ref[h]
                m = jnp.maximum(jnp.max(lg, axis=-1, keepdims=True), sink)
                pe = jnp.exp(lg - m)
                den = jnp.sum(pe, axis=-1, keepdims=True) + jnp.exp(sink - m)
                o = _dot(pe.astype(BF16), vband)
                outs.append(o / den)
            if kvh == 0:
                pair = jnp.where(low_half, outs[0], pltpu.roll(outs[1], HEAD_DIM, axis=1))
            else:
                pair = jnp.where(low_half, pltpu.roll(outs[0], HEAD_DIM, axis=1), outs[1])
            ybuf[r0:r0 + ATTN_BLOCK, POOL_WIDTH + p * LANES:POOL_WIDTH + (p + 1) * LANES] = pair.astype(BF16)
    kbuf[0:ATTN_BLOCK, :] = kbuf[TS:TS + ATTN_BLOCK, :]
    vbuf[0:ATTN_BLOCK, :] = vbuf[TS:TS + ATTN_BLOCK, :]

    h = x + _dot(ybuf[...], wout_ref[...])
    h_ref[...] = h
    h2 = _rms(h, gffn_ref[...])
    for c in range(ROW_CHUNKS):
        h2_ref[pl.ds(c, TS, stride=ROW_CHUNKS), :] = h2[:, c * LANES:(c + 1) * LANES]

    logits = _dot_nt(wrt_ref[...], h2, precision=lax.Precision.HIGHEST)
    scores = _sigmoid(logits)
    biased = scores + rbias_ref[...]
    erow = lax.broadcasted_iota(jnp.int32, (N_EXPERTS, TS), 0)
    grow = lax.broadcasted_iota(jnp.int32, (EXPERTS_PER_GROUP, TS), 0)
    group_scores = []
    for g in range(N_EXPERT_GROUPS):
        blk = biased[g * EXPERTS_PER_GROUP:(g + 1) * EXPERTS_PER_GROUP, :]
        m1 = jnp.max(blk, axis=0, keepdims=True)
        i1 = jnp.min(jnp.where(blk == m1, grow, EXPERTS_PER_GROUP), axis=0, keepdims=True)
        m2 = jnp.max(jnp.where(grow == i1, NEG_INF, blk), axis=0, keepdims=True)
        group_scores.append(m1 + m2)
    cur = jnp.concatenate(group_scores, axis=0)
    gsel = jnp.zeros((N_EXPERT_GROUPS, TS), jnp.bool_)
    for _ in range(TOPK_GROUPS):
        m = jnp.max(cur, axis=0, keepdims=True)
        i = jnp.min(jnp.where(cur == m, grow, N_EXPERT_GROUPS), axis=0, keepdims=True)
        hit = grow == i
        gsel = jnp.logical_or(gsel, hit)
        cur = jnp.where(hit, NEG_INF, cur)
    gmask = jnp.concatenate(
        [jnp.broadcast_to(gsel[g:g + 1, :], (EXPERTS_PER_GROUP, TS)) for g in range(N_EXPERT_GROUPS)], axis=0)
    masked = jnp.where(gmask, biased, NEG_INF)
    sel = jnp.zeros((N_EXPERTS, TS), jnp.bool_)
    idxs, ws = [], []
    for _ in range(TOP_K):
        m = jnp.max(masked, axis=0, keepdims=True)
        i = jnp.min(jnp.where(masked == m, erow, N_EXPERTS), axis=0, keepdims=True)
        hit = erow == i
        idxs.append(i)
        ws.append(jnp.sum(jnp.where(hit, scores, 0.0), axis=0, keepdims=True))
        sel = jnp.logical_or(sel, hit)
        masked = jnp.where(hit, NEG_INF, masked)
    wtot = ws[0]
    for wk in ws[1:]:
        wtot = wtot + wk
    idx_ref[...] = jnp.concatenate(idxs, axis=0)
    topw_ref[...] = jnp.concatenate([wk / wtot * ROUTED_SCALE for wk in ws], axis=0)

    self32 = sel.astype(F32)
    ri = lax.broadcasted_iota(jnp.int32, (TS, TS), 0)
    ci = lax.broadcasted_iota(jnp.int32, (TS, TS), 1)
    before = (ri < ci).astype(BF16)
    running = _dot(self32.astype(BF16), before) + cnt_acc[:, 0:1]
    rank_ref[...] = jnp.concatenate(
        [jnp.sum(jnp.where(erow == i, running, 0.0), axis=0, keepdims=True) for i in idxs], axis=0).astype(jnp.int32)
    cnt_acc[...] = cnt_acc[...] + jnp.sum(self32, axis=1, keepdims=True)
    cnt_ref[...] = cnt_acc[...]


def _mix_call(x, g_mix, w_in, w_pool, pool_scale, sinks, rpe_table, bucket, w_out, g_ffn, w_rt, r_bias):
    TS = MIX_TOKENS
    ns = SEQ // TS
    const = lambda shape: pl.BlockSpec(shape, lambda b, s: (0,) * len(shape), pipeline_mode=pl.Buffered(1))
    smem = pl.BlockSpec(memory_space=pltpu.SMEM)
    tok3 = pl.BlockSpec((None, TS, D_MODEL), lambda b, s: (b, s, 0))
    lane_blk = pl.BlockSpec((TOP_K, TS), lambda b, s: (0, b * ns + s))
    return pl.pallas_call(
        _mix_kernel,
        grid=(BATCH, ns),
        in_specs=[
            tok3,
            const((1, D_MODEL)),
            const((D_MODEL, IN_WIDTH)),
            const((POOL_GROUPS, POOL_GROUP_DIM, POOL_GROUP_DIM)),
            const((1, POOL_WIDTH)),
            smem,
            smem,
            const((ATTN_BLOCK, 2 * ATTN_BLOCK)),
            const((MIX_WIDTH, D_MODEL)),
            const((1, D_MODEL)),
            const((N_EXPERTS, D_MODEL)),
            const((N_EXPERTS, 1)),
        ],
        out_specs=[
            tok3,
            pl.BlockSpec((TS * ROW_CHUNKS, LANES), lambda b, s: (b * ns + s, 0)),
            lane_blk,
            lane_blk,
            lane_blk,
            pl.BlockSpec((N_EXPERTS, LANES), lambda b, s: (0, 0)),
        ],
        out_shape=[
            jax.ShapeDtypeStruct((BATCH, SEQ, D_MODEL), F32),
            jax.ShapeDtypeStruct((TOKENS * ROW_CHUNKS, LANES), F32),
            jax.ShapeDtypeStruct((TOP_K, TOKENS), jnp.int32),
            jax.ShapeDtypeStruct((TOP_K, TOKENS), F32),
            jax.ShapeDtypeStruct((TOP_K, TOKENS), jnp.int32),
            jax.ShapeDtypeStruct((N_EXPERTS, LANES), F32),
        ],
        scratch_shapes=[
            pltpu.VMEM((POOL_HISTORY + TS, POOL_WIDTH), F32),
            pltpu.VMEM((ATTN_BLOCK + TS, KV_WIDTH), BF16),
            pltpu.VMEM((ATTN_BLOCK + TS, KV_WIDTH), BF16),
            pltpu.VMEM((N_HEADS, ATTN_BLOCK, 2 * ATTN_BLOCK), F32),
            pltpu.VMEM((TS, MIX_WIDTH), BF16),
            pltpu.VMEM((N_EXPERTS, LANES), F32),
        ],
        compiler_params=pltpu.CompilerParams(
            dimension_semantics=("arbitrary", "arbitrary"), vmem_limit_bytes=VMEM_LIMIT_BYTES),
        name="mix_router",
    )(x, g_mix, w_in, w_pool, pool_scale, sinks, rpe_table, bucket, w_out, g_ffn, w_rt, r_bias)


def _dispatch_kernel(idx_ref, rank_ref, poff_ref, pend_ref, h2_hbm, xs_hbm, zbuf, sem, zsem):
    step = pl.program_id(0)

    def zero_copy(e):
        start = pl.multiple_of((pend_ref[e] - ROW_TILE) * ROW_CHUNKS, ROW_CHUNKS)
        return pltpu.make_async_copy(zbuf, xs_hbm.at[pl.ds(start, ROW_TILE * ROW_CHUNKS)], zsem)

    @pl.when(step == 0)
    def _zero_tails():
        zbuf[...] = jnp.zeros_like(zbuf)

        def start(e, c):
            @pl.when(pend_ref[e] > poff_ref[e])
            def _():
                zero_copy(e).start()
            return c

        def wait(e, c):
            @pl.when(pend_ref[e] > poff_ref[e])
            def _():
                zero_copy(e).wait()
            return c

        lax.fori_loop(0, N_EXPERTS, start, 0)
        lax.fori_loop(0, N_EXPERTS, wait, 0)

    base = step * DISPATCH_TOKENS

    def row_copy(t, k):
        src = pl.multiple_of((base + t) * ROW_CHUNKS, ROW_CHUNKS)
        dest = pl.multiple_of((poff_ref[idx_ref[k, t]] + rank_ref[k, t]) * ROW_CHUNKS, ROW_CHUNKS)
        return pltpu.make_async_copy(h2_hbm.at[pl.ds(src, ROW_CHUNKS)], xs_hbm.at[pl.ds(dest, ROW_CHUNKS)], sem)

    def start_tok(t, c):
        for k in range(TOP_K):
            row_copy(t, k).start()
        return c

    def wait_tok(t, c):
        for k in range(TOP_K):
            row_copy(t, k).wait()
        return c

    lax.fori_loop(0, DISPATCH_TOKENS, start_tok, 0)
    lax.fori_loop(0, DISPATCH_TOKENS, wait_tok, 0)


def _dispatch_call(idx3, rank3, poff, pend, h2):
    n_steps = TOKENS // DISPATCH_TOKENS
    smem_blk = pl.BlockSpec((None, TOP_K, DISPATCH_TOKENS), lambda i: (i, 0, 0), memory_space=pltpu.SMEM)
    smem = pl.BlockSpec(memory_space=pltpu.SMEM)
    any_spec = pl.BlockSpec(memory_space=pl.ANY)
    return pl.pallas_call(
        _dispatch_kernel,
        grid=(n_steps,),
        in_specs=[smem_blk, smem_blk, smem, smem, any_spec],
        out_specs=any_spec,
        out_shape=jax.ShapeDtypeStruct((N_ROWS * ROW_CHUNKS, LANES), F32),
        scratch_shapes=[
            pltpu.VMEM((ROW_TILE * ROW_CHUNKS, LANES), F32),
            pltpu.SemaphoreType.DMA,
            pltpu.SemaphoreType.DMA,
        ],
        compiler_params=pltpu.CompilerParams(dimension_semantics=("arbitrary",)),
        name="dispatch_rows",
    )(idx3, rank3, poff, pend, h2)


def _expert_kernel(te_ref, tb_ref, tv_ref, xs_ref, wg_ref, wu_ref, wd_ref, ys_ref, xb_ref):
    i = pl.program_id(0)

    @pl.when(tv_ref[i] == 1)
    def _():
        for c in range(ROW_CHUNKS):
            xb_ref[:, c * LANES:(c + 1) * LANES] = xs_ref[pl.ds(c, ROW_TILE, stride=ROW_CHUNKS), :].astype(BF16)
        xb = xb_ref[...]
        g = _dot(xb, wg_ref[...].astype(BF16))
        u = _dot(xb, wu_ref[...].astype(BF16))
        hb = (g * _sigmoid(g) * u).astype(BF16)
        y = _dot(hb, wd_ref[...].astype(BF16))
        for c in range(ROW_CHUNKS):
            ys_ref[pl.ds(c, ROW_TILE, stride=ROW_CHUNKS), :] = y[:, c * LANES:(c + 1) * LANES]


def _expert_call(tile_e, tile_b, tile_v, xs, w_gate, w_up, w_down):
    row_blk = pl.BlockSpec((ROW_TILE * ROW_CHUNKS, LANES), lambda i, te, tb, tv: (tb[i], 0))
    return pl.pallas_call(
        _expert_kernel,
        grid_spec=pltpu.PrefetchScalarGridSpec(
            num_scalar_prefetch=3,
            grid=(N_ROW_TILES,),
            in_specs=[
                row_blk,
                pl.BlockSpec((None, D_MODEL, EXPERT_FF), lambda i, te, tb, tv: (te[i], 0, 0)),
                pl.BlockSpec((None, D_MODEL, EXPERT_FF), lambda i, te, tb, tv: (te[i], 0, 0)),
                pl.BlockSpec((None, EXPERT_FF, D_MODEL), lambda i, te, tb, tv: (te[i], 0, 0)),
            ],
            out_specs=row_blk,
            scratch_shapes=[pltpu.VMEM((ROW_TILE, D_MODEL), BF16)],
        ),
        out_shape=jax.ShapeDtypeStruct((N_ROWS * ROW_CHUNKS, LANES), F32),
        compiler_params=pltpu.CompilerParams(
            dimension_semantics=("arbitrary",), vmem_limit_bytes=VMEM_LIMIT_BYTES),
        name="routed_experts",
    )(tile_e, tile_b, tile_v, xs, w_gate, w_up, w_down)


def _final_kernel(idx_ref, rank_ref, poff_ref, h_ref, gffn_ref, p_ref, wt_ref, ys_hbm,
                  wsg_ref, wsu_ref, wsd_ref, wpp_ref, gple_ref, wpg_ref, gfin_ref,
                  out_ref, rows, h3_ref, sem):
    TB = FINAL_TOKENS

    def row_copy(t, k):
        src = pl.multiple_of((poff_ref[idx_ref[k, t]] + rank_ref[k, t]) * ROW_CHUNKS, ROW_CHUNKS)
        dst = pl.multiple_of((k * TB + t) * GATHER_PITCH, SUBLANES)
        return pltpu.make_async_copy(ys_hbm.at[pl.ds(src, ROW_CHUNKS)], rows.at[pl.ds(dst, ROW_CHUNKS)], sem)

    def start_tok(t, c):
        for k in range(TOP_K):
            row_copy(t, k).start()
        return c

    def wait_tok(t, c):
        for k in range(TOP_K):
            row_copy(t, k).wait()
        return c

    lax.fori_loop(0, TB, start_tok, 0)

    h = h_ref[...]
    h2b = _rms(h, gffn_ref[...]).astype(BF16)
    g = _dot(h2b, wsg_ref[...])
    u = _dot(h2b, wsu_ref[...])
    shared = _dot((g * _sigmoid(g) * u).astype(BF16), wsd_ref[...])
    ple = _rms(_dot(p_ref[...].astype(BF16), wpp_ref[...]), gple_ref[...])

    lax.fori_loop(0, TB, wait_tok, 0)
    wt = wt_ref[...]
    for c in range(ROW_CHUNKS):
        cs = slice(c * LANES, (c + 1) * LANES)
        routed = rows[pl.ds(c, TB, stride=GATHER_PITCH), :] * wt[:, 0:1]
        for k in range(1, TOP_K):
            routed = routed + rows[pl.ds(k * TB * GATHER_PITCH + c, TB, stride=GATHER_PITCH), :] * wt[:, k:k + 1]
        h3_ref[:, cs] = h[:, cs] + (routed + shared[:, cs])

    h3 = h3_ref[...]
    gate = _sigmoid(_dot(h3.astype(BF16), wpg_ref[...]))
    h4 = h3 + gate * ple
    out_ref[...] = _rms(h4, gfin_ref[...])


def _final_call(idx3, rank3, poff, h, g_ffn, p, wt, ys, ws_gate, ws_up, ws_down, w_pp, g_ple, w_pg, g_final):
    TB = FINAL_TOKENS
    n_steps = TOKENS // TB
    const = lambda shape: pl.BlockSpec(shape, lambda i: (0,) * len(shape), pipeline_mode=pl.Buffered(1))
    smem_blk = pl.BlockSpec((None, TOP_K, TB), lambda i: (i, 0, 0), memory_space=pltpu.SMEM)
    smem = pl.BlockSpec(memory_space=pltpu.SMEM)
    tok = pl.BlockSpec((TB, D_MODEL), lambda i: (i, 0))
    return pl.pallas_call(
        _final_kernel,
        grid=(n_steps,),
        in_specs=[
            smem_blk, smem_blk, smem,
            tok,
            const((1, D_MODEL)),
            pl.BlockSpec((TB, PLE_DIM), lambda i: (i, 0)),
            pl.BlockSpec((TB, TOP_K), lambda i: (i, 0)),
            pl.BlockSpec(memory_space=pl.ANY),
            const((D_MODEL, SHARED_FF)),
            const((D_MODEL, SHARED_FF)),
            const((SHARED_FF, D_MODEL)),
            const((PLE_DIM, D_MODEL)),
            const((1, D_MODEL)),
            const((D_MODEL, D_MODEL)),
            const((1, D_MODEL)),
        ],
        out_specs=tok,
        out_shape=jax.ShapeDtypeStruct((TOKENS, D_MODEL), F32),
        scratch_shapes=[
            pltpu.VMEM((TOP_K * TB * GATHER_PITCH, LANES), F32),
            pltpu.VMEM((TB, D_MODEL), F32),
            pltpu.SemaphoreType.DMA,
        ],
        compiler_params=pltpu.CompilerParams(
            dimension_semantics=("arbitrary",), vmem_limit_bytes=VMEM_LIMIT_BYTES),
        name="combine_final",
    )(idx3, rank3, poff, h, g_ffn, p, wt, ys, ws_gate, ws_up, ws_down, w_pp, g_ple, w_pg, g_final)


def _per_step(a, tokens_per_step):
    return a.reshape(TOP_K, TOKENS // tokens_per_step, tokens_per_step).transpose(1, 0, 2)


def kernel(x, p, g_mix, w_in, w_pool, pool_scale, attn_sinks, rpe_table, w_out, g_ffn, w_router, router_bias,
           w_gate, w_up, w_down, ws_gate, ws_up, ws_down, w_ple_proj, g_ple, w_ple_gate, g_final):
    bucket = jnp.asarray(_rpe_bucket_map())
    h, h2, idx, topw, rank, cnt = _mix_call(
        x, g_mix[0][None, :], w_in[0].astype(BF16), w_pool[0].astype(BF16), pool_scale[0][None, :],
        attn_sinks[0], rpe_table, bucket, w_out[0].astype(BF16), g_ffn[0][None, :],
        w_router[0].T, router_bias[0][:, None])

    counts = cnt[:, 0].astype(jnp.int32)
    padded = (counts + ROW_TILE - 1) // ROW_TILE * ROW_TILE
    pend = jnp.cumsum(padded).astype(jnp.int32)
    poff = pend - padded
    n_tiles = pend[-1] // ROW_TILE
    tile = jnp.arange(N_ROW_TILES, dtype=jnp.int32)
    tile_v = (tile < n_tiles).astype(jnp.int32)
    tile_b = jnp.minimum(tile, jnp.maximum(n_tiles - 1, 0))
    tile_e = jnp.minimum(jnp.searchsorted(pend, tile_b * ROW_TILE, side="right"), N_EXPERTS - 1).astype(jnp.int32)

    xs = _dispatch_call(_per_step(idx, DISPATCH_TOKENS), _per_step(rank, DISPATCH_TOKENS), poff, pend, h2)
    ys = _expert_call(tile_e, tile_b, tile_v, xs, w_gate[0], w_up[0], w_down[0])
    out = _final_call(
        _per_step(idx, FINAL_TOKENS), _per_step(rank, FINAL_TOKENS), poff,
        h.reshape(TOKENS, D_MODEL), g_ffn[0][None, :], p[0].reshape(TOKENS, PLE_DIM), topw.T, ys,
        ws_gate[0].astype(BF16), ws_up[0].astype(BF16), ws_down[0].astype(BF16),
        w_ple_proj[0].astype(BF16), g_ple[0][None, :], w_ple_gate[0].astype(BF16), g_final[None, :])
    return out.reshape(BATCH, SEQ, D_MODEL)
```

```python
import math

import numpy as np
import jax
import jax.numpy as jnp
from jax import lax
from jax.experimental import pallas as pl
from jax.experimental.pallas import tpu as pltpu

F32 = jnp.float32
BF16 = jnp.bfloat16
NEG_INF = float("-inf")

D_MODEL = 2048
BATCH = 4
SEQ = 2048
TOKENS = BATCH * SEQ
PLE_DIM = 256
POOL_WIDTH = 1024
POOL_GROUPS = 4
POOL_GROUP_DIM = POOL_WIDTH // POOL_GROUPS
POOL_WINDOWS = (2, 4, 8, 16)
N_HEADS = 16
N_KV_HEADS = 2
HEAD_DIM = 64
HEADS_PER_KV = N_HEADS // N_KV_HEADS
ATTN_WIDTH = N_HEADS * HEAD_DIM
KV_WIDTH = N_KV_HEADS * HEAD_DIM
IN_WIDTH = POOL_WIDTH + ATTN_WIDTH + 2 * KV_WIDTH
Q_OFF = POOL_WIDTH
K_OFF = POOL_WIDTH + ATTN_WIDTH
V_OFF = K_OFF + KV_WIDTH
ATTN_BLOCK = 128
WINDOW = 128
RPE_BUCKETS = 32
RPE_MAX_EXACT = RPE_BUCKETS // 2
RPE_MAX_DISTANCE = 128
N_EXPERTS = 64
TOP_K = 8
N_EXPERT_GROUPS = 8
EXPERTS_PER_GROUP = N_EXPERTS // N_EXPERT_GROUPS
TOPK_GROUPS = 4
EXPERT_FF = 512
SHARED_FF = 512
ROUTED_SCALE = 2.5
EPS = 1e-6

LANES = 128
SUBLANES = 8
VMEM_LIMIT_BYTES = 58 * 1024 * 1024

MIX_TOKENS = 256
POOL_HISTORY = 16
ROW_TILE = 256
N_ROWS = TOKENS * TOP_K + N_EXPERTS * ROW_TILE
N_ROW_TILES = N_ROWS // ROW_TILE
DISPATCH_TOKENS = 512
FINAL_TOKENS = 128
MIX_WIDTH = POOL_WIDTH + ATTN_WIDTH
ROW_CHUNKS = D_MODEL // LANES
GATHER_PITCH = 24


def _rms(x, g):
    return x * lax.rsqrt(jnp.mean(x * x, axis=-1, keepdims=True) + EPS) * g


def _sigmoid(x):
    return 1.0 / (1.0 + jnp.exp(-x))


def _dot(a, b):
    return jnp.dot(a, b, preferred_element_type=F32)


def _dot_nt(a, b, precision=None):
    return lax.dot_general(a, b, (((1,), (1,)), ((), ())), preferred_element_type=F32, precision=precision)


def _rpe_bucket_map():
    i = np.arange(ATTN_BLOCK)[:, None]
    j = np.arange(2 * ATTN_BLOCK)[None, :]
    dist = i + ATTN_BLOCK - j
    n = np.maximum(dist, 0)
    nf = np.maximum(n, 1).astype(np.float32)
    large = RPE_MAX_EXACT + (np.log(nf / np.float32(RPE_MAX_EXACT)) / np.float32(math.log(RPE_MAX_DISTANCE / RPE_MAX_EXACT))
                             * np.float32(RPE_BUCKETS - RPE_MAX_EXACT)).astype(np.int32)
    large = np.minimum(large, RPE_BUCKETS - 1)
    bucket = np.where(n < RPE_MAX_EXACT, n, large)
    valid = (dist >= 0) & (dist < WINDOW)
    return np.where(valid, bucket, -1).astype(np.int32)


def _mix_kernel(x_ref, gmix_ref, win_ref, wpool_ref, pscale_ref, sinks_ref, rpe_ref, bucket_ref,
                wout_ref, gffn_ref, wrt_ref, rbias_ref,
                h_ref, h2_ref, idx_ref, topw_ref, rank_ref, cnt_ref,
                ubuf, kbuf, vbuf, bias_buf, ybuf, cnt_acc):
    TS = MIX_TOKENS
    b = pl.program_id(0)
    s = pl.program_id(1)

    @pl.when((b == 0) & (s == 0))
    def _init():
        bucket = bucket_ref[...]
        for h in range(N_HEADS):
            acc = jnp.full((ATTN_BLOCK, 2 * ATTN_BLOCK), NEG_INF, F32)
            for bk in range(RPE_BUCKETS):
                acc = jnp.where(bucket == bk, rpe_ref[bk, h], acc)
            bias_buf[h] = acc
        cnt_acc[...] = jnp.zeros_like(cnt_acc)

    @pl.when(s == 0)
    def _reset_history():
        ubuf[0:POOL_HISTORY, :] = jnp.zeros((POOL_HISTORY, POOL_WIDTH), F32)
        kbuf[0:ATTN_BLOCK, :] = jnp.zeros((ATTN_BLOCK, KV_WIDTH), BF16)
        vbuf[0:ATTN_BLOCK, :] = jnp.zeros((ATTN_BLOCK, KV_WIDTH), BF16)

    x = x_ref[...]
    a = _rms(x, gmix_ref[...]).astype(BF16)
    z = _dot(a, win_ref[...])

    ubuf[POOL_HISTORY:POOL_HISTORY + TS, :] = z[:, 0:POOL_WIDTH]
    pos = s * TS + lax.broadcasted_iota(jnp.int32, (TS, 1), 0)
    for gi, w in enumerate(POOL_WINDOWS):
        c0, c1 = gi * POOL_GROUP_DIM, (gi + 1) * POOL_GROUP_DIM
        e = ubuf[:, c0:c1]
        shift = 1
        while shift < w:
            e = e + pltpu.roll(e, shift, axis=0)
            shift *= 2
        wsum = e[POOL_HISTORY:, :]
        count = jnp.minimum(pos + 1, w).astype(F32)
        pooled = wsum / count - z[:, c0:c1]
        yp = _dot(pooled.astype(BF16), wpool_ref[gi]) * pscale_ref[:, c0:c1]
        ybuf[:, c0:c1] = yp.astype(BF16)
    ubuf[0:POOL_HISTORY, :] = ubuf[TS:TS + POOL_HISTORY, :]

    kbuf[ATTN_BLOCK:ATTN_BLOCK + TS, :] = z[:, K_OFF:K_OFF + KV_WIDTH].astype(BF16)
    vbuf[ATTN_BLOCK:ATTN_BLOCK + TS, :] = z[:, V_OFF:V_OFF + KV_WIDTH].astype(BF16)
    lane = lax.broadcasted_iota(jnp.int32, (ATTN_BLOCK, LANES), 1)
    low_half = lane < HEAD_DIM
    col = lax.broadcasted_iota(jnp.int32, (1, 2 * ATTN_BLOCK), 1)
    first_mask = jnp.where((col < ATTN_BLOCK) & (s == 0), NEG_INF, 0.0).astype(F32)
    for sb in range(TS // ATTN_BLOCK):
        r0 = sb * ATTN_BLOCK
        kband = kbuf[r0:r0 + 2 * ATTN_BLOCK, :]
        vband = vbuf[r0:r0 + 2 * ATTN_BLOCK, :]
        for p in range(N_HEADS // 2):
            qp = z[r0:r0 + ATTN_BLOCK, Q_OFF + p * LANES:Q_OFF + (p + 1) * LANES] * (HEAD_DIM ** -0.5)
            qr = pltpu.roll(qp, HEAD_DIM, axis=1)
            kvh = (2 * p) // HEADS_PER_KV
            kv_lanes = low_half if kvh == 0 else jnp.logical_not(low_half)
            outs = []
            for par in range(2):
                h = 2 * p + par
                src = qp if par == kvh else qr
                qh = jnp.where(kv_lanes, src, 0.0).astype(BF16)
                lg = _dot_nt(qh, kband) + bias_buf[h]
                if sb == 0:
                    lg = lg + first_mask
                sink = sinks_ref[h]
                m = jnp.maximum(jnp.max(lg, axis=-1, keepdims=True), sink)
                pe = jnp.exp(lg - m)
                den = jnp.sum(pe, axis=-1, keepdims=True) + jnp.exp(sink - m)
                o = _dot(pe.astype(BF16), vband)
                outs.append(o / den)
            if kvh == 0:
                pair = jnp.where(low_half, outs[0], pltpu.roll(outs[1], HEAD_DIM, axis=1))
            else:
                pair = jnp.where(low_half, pltpu.roll(outs[0], HEAD_DIM, axis=1), outs[1])
            ybuf[r0:r0 + ATTN_BLOCK, POOL_WIDTH + p * LANES:POOL_WIDTH + (p + 1) * LANES] = pair.astype(BF16)
    kbuf[0:ATTN_BLOCK, :] = kbuf[TS:TS + ATTN_BLOCK, :]
    vbuf[0:ATTN_BLOCK, :] = vbuf[TS:TS + ATTN_BLOCK, :]

    h = x + _dot(ybuf[...], wout_ref[...])
    h_ref[...] = h
    h2 = _rms(h, gffn_ref[...])
    for c in range(ROW_CHUNKS):
        h2_ref[pl.ds(c, TS, stride=ROW_CHUNKS), :] = h2[:, c * LANES:(c + 1) * LANES]

    logits = _dot_nt(wrt_ref[...], h2, precision=lax.Precision.HIGHEST)
    scores = _sigmoid(logits)
    biased = scores + rbias_ref[...]
    erow = lax.broadcasted_iota(jnp.int32, (N_EXPERTS, TS), 0)
    grow = lax.broadcasted_iota(jnp.int32, (EXPERTS_PER_GROUP, TS), 0)
    group_scores = []
    for g in range(N_EXPERT_GROUPS):
        blk = biased[g * EXPERTS_PER_GROUP:(g + 1) * EXPERTS_PER_GROUP, :]
        m1 = jnp.max(blk, axis=0, keepdims=True)
        i1 = jnp.min(jnp.where(blk == m1, grow, EXPERTS_PER_GROUP), axis=0, keepdims=True)
        m2 = jnp.max(jnp.where(grow == i1, NEG_INF, blk), axis=0, keepdims=True)
        group_scores.append(m1 + m2)
    cur = jnp.concatenate(group_scores, axis=0)
    gsel = jnp.zeros((N_EXPERT_GROUPS, TS), jnp.bool_)
    for _ in range(TOPK_GROUPS):
        m = jnp.max(cur, axis=0, keepdims=True)
        i = jnp.min(jnp.where(cur == m, grow, N_EXPERT_GROUPS), axis=0, keepdims=True)
        hit = grow == i
        gsel = jnp.logical_or(gsel, hit)
        cur = jnp.where(hit, NEG_INF, cur)
    gmask = jnp.concatenate(
        [jnp.broadcast_to(gsel[g:g + 1, :], (EXPERTS_PER_GROUP, TS)) for g in range(N_EXPERT_GROUPS)], axis=0)
    masked = jnp.where(gmask, biased, NEG_INF)
    sel = jnp.zeros((N_EXPERTS, TS), jnp.bool_)
    idxs, ws = [], []
    for _ in range(TOP_K):
        m = jnp.max(masked, axis=0, keepdims=True)
        i = jnp.min(jnp.where(masked == m, erow, N_EXPERTS), axis=0, keepdims=True)
        hit = erow == i
        idxs.append(i)
        ws.append(jnp.sum(jnp.where(hit, scores, 0.0), axis=0, keepdims=True))
        sel = jnp.logical_or(sel, hit)
        masked = jnp.where(hit, NEG_INF, masked)
    wtot = ws[0]
    for wk in ws[1:]:
        wtot = wtot + wk
    idx_ref[...] = jnp.concatenate(idxs, axis=0)
    topw_ref[...] = jnp.concatenate([wk / wtot * ROUTED_SCALE for wk in ws], axis=0)

    self32 = sel.astype(F32)
    ri = lax.broadcasted_iota(jnp.int32, (TS, TS), 0)
    ci = lax.broadcasted_iota(jnp.int32, (TS, TS), 1)
    before = (ri < ci).astype(BF16)
    running = _dot(self32.astype(BF16), before) + cnt_acc[:, 0:1]
    rank_ref[...] = jnp.concatenate(
        [jnp.sum(jnp.where(erow == i, running, 0.0), axis=0, keepdims=True) for i in idxs], axis=0).astype(jnp.int32)
    cnt_acc[...] = cnt_acc[...] + jnp.sum(self32, axis=1, keepdims=True)
    cnt_ref[...] = cnt_acc[...]


def _mix_call(x, g_mix, w_in, w_pool, pool_scale, sinks, rpe_table, bucket, w_out, g_ffn, w_rt, r_bias):
    TS = MIX_TOKENS
    ns = SEQ // TS
    const = lambda shape: pl.BlockSpec(shape, lambda b, s: (0,) * len(shape), pipeline_mode=pl.Buffered(1))
    smem = pl.BlockSpec(memory_space=pltpu.SMEM)
    tok3 = pl.BlockSpec((None, TS, D_MODEL), lambda b, s: (b, s, 0))
    lane_blk = pl.BlockSpec((TOP_K, TS), lambda b, s: (0, b * ns + s))
    return pl.pallas_call(
        _mix_kernel,
        grid=(BATCH, ns),
        in_specs=[
            tok3,
            const((1, D_MODEL)),
            const((D_MODEL, IN_WIDTH)),
            const((POOL_GROUPS, POOL_GROUP_DIM, POOL_GROUP_DIM)),
            const((1, POOL_WIDTH)),
            smem,
            smem,
            const((ATTN_BLOCK, 2 * ATTN_BLOCK)),
            const((MIX_WIDTH, D_MODEL)),
            const((1, D_MODEL)),
            const((N_EXPERTS, D_MODEL)),
            const((N_EXPERTS, 1)),
        ],
        out_specs=[
            tok3,
            pl.BlockSpec((TS * ROW_CHUNKS, LANES), lambda b, s: (b * ns + s, 0)),
            lane_blk,
            lane_blk,
            lane_blk,
            pl.BlockSpec((N_EXPERTS, LANES), lambda b, s: (0, 0)),
        ],
        out_shape=[
            jax.ShapeDtypeStruct((BATCH, SEQ, D_MODEL), F32),
            jax.ShapeDtypeStruct((TOKENS * ROW_CHUNKS, LANES), F32),
            jax.ShapeDtypeStruct((TOP_K, TOKENS), jnp.int32),
            jax.ShapeDtypeStruct((TOP_K, TOKENS), F32),
            jax.ShapeDtypeStruct((TOP_K, TOKENS), jnp.int32),
            jax.ShapeDtypeStruct((N_EXPERTS, LANES), F32),
        ],
        scratch_shapes=[
            pltpu.VMEM((POOL_HISTORY + TS, POOL_WIDTH), F32),
            pltpu.VMEM((ATTN_BLOCK + TS, KV_WIDTH), BF16),
            pltpu.VMEM((ATTN_BLOCK + TS, KV_WIDTH), BF16),
            pltpu.VMEM((N_HEADS, ATTN_BLOCK, 2 * ATTN_BLOCK), F32),
            pltpu.VMEM((TS, MIX_WIDTH), BF16),
            pltpu.VMEM((N_EXPERTS, LANES), F32),
        ],
        compiler_params=pltpu.CompilerParams(
            dimension_semantics=("arbitrary", "arbitrary"), vmem_limit_bytes=VMEM_LIMIT_BYTES),
        name="mix_router",
    )(x, g_mix, w_in, w_pool, pool_scale, sinks, rpe_table, bucket, w_out, g_ffn, w_rt, r_bias)


def _dispatch_kernel(idx_ref, rank_ref, poff_ref, pend_ref, h2_ref, xs_hbm, zbuf, sem, zsem):
    step = pl.program_id(0)

    def zero_copy(e):
        start = pl.multiple_of((pend_ref[e] - ROW_TILE) * ROW_CHUNKS, ROW_CHUNKS)
        return pltpu.make_async_copy(zbuf, xs_hbm.at[pl.ds(start, ROW_TILE * ROW_CHUNKS)], zsem)

    @pl.when(step == 0)
    def _zero_tails():
        zbuf[...] = jnp.zeros_like(zbuf)

        def start(e, c):
            @pl.when(pend_ref[e] > poff_ref[e])
            def _():
                zero_copy(e).start()
            return c

        def wait(e, c):
            @pl.when(pend_ref[e] > poff_ref[e])
            def _():
                zero_copy(e).wait()
            return c

        lax.fori_loop(0, N_EXPERTS, start, 0)
        lax.fori_loop(0, N_EXPERTS, wait, 0)

    def row_copy(t, k):
        src = pl.multiple_of(t * ROW_CHUNKS, ROW_CHUNKS)
        dest = pl.multiple_of((poff_ref[idx_ref[k, t]] + rank_ref[k, t]) * ROW_CHUNKS, ROW_CHUNKS)
        return pltpu.make_async_copy(h2_ref.at[pl.ds(src, ROW_CHUNKS)], xs_hbm.at[pl.ds(dest, ROW_CHUNKS)], sem)

    def start_tok(t, c):
        for k in range(TOP_K):
            row_copy(t, k).start()
        return c

    def wait_tok(t, c):
        for k in range(TOP_K):
            row_copy(t, k).wait()
        return c

    lax.fori_loop(0, DISPATCH_TOKENS, start_tok, 0)
    lax.fori_loop(0, DISPATCH_TOKENS, wait_tok, 0)


def _dispatch_call(idx3, rank3, poff, pend, h2):
    n_steps = TOKENS // DISPATCH_TOKENS
    smem_blk = pl.BlockSpec((None, TOP_K, DISPATCH_TOKENS), lambda i: (i, 0, 0), memory_space=pltpu.SMEM)
    smem = pl.BlockSpec(memory_space=pltpu.SMEM)
    any_spec = pl.BlockSpec(memory_space=pl.ANY)
    return pl.pallas_call(
        _dispatch_kernel,
        grid=(n_steps,),
        in_specs=[smem_blk, smem_blk, smem, smem,
                  pl.BlockSpec((DISPATCH_TOKENS * ROW_CHUNKS, LANES), lambda i: (i, 0))],
        out_specs=any_spec,
        out_shape=jax.ShapeDtypeStruct((N_ROWS * ROW_CHUNKS, LANES), F32),
        scratch_shapes=[
            pltpu.VMEM((ROW_TILE * ROW_CHUNKS, LANES), F32),
            pltpu.SemaphoreType.DMA,
            pltpu.SemaphoreType.DMA,
        ],
        compiler_params=pltpu.CompilerParams(dimension_semantics=("arbitrary",)),
        name="dispatch_rows",
    )(idx3, rank3, poff, pend, h2)


def _expert_kernel(te_ref, tb_ref, tv_ref, xs_ref, wg_ref, wu_ref, wd_ref, ys_ref, xb_ref):
    i = pl.program_id(0)

    @pl.when(tv_ref[i] == 1)
    def _():
        for c in range(ROW_CHUNKS):
            xb_ref[:, c * LANES:(c + 1) * LANES] = xs_ref[pl.ds(c, ROW_TILE, stride=ROW_CHUNKS), :].astype(BF16)
        xb = xb_ref[...]
        g = _dot(xb, wg_ref[...].astype(BF16))
        u = _dot(xb, wu_ref[...].astype(BF16))
        hb = (g * _sigmoid(g) * u).astype(BF16)
        y = _dot(hb, wd_ref[...].astype(BF16))
        for c in range(ROW_CHUNKS):
            ys_ref[pl.ds(c, ROW_TILE, stride=ROW_CHUNKS), :] = y[:, c * LANES:(c + 1) * LANES]


def _expert_call(tile_e, tile_b, tile_v, xs, w_gate, w_up, w_down):
    row_blk = pl.BlockSpec((ROW_TILE * ROW_CHUNKS, LANES), lambda i, te, tb, tv: (tb[i], 0))
    return pl.pallas_call(
        _expert_kernel,
        grid_spec=pltpu.PrefetchScalarGridSpec(
            num_scalar_prefetch=3,
            grid=(N_ROW_TILES,),
            in_specs=[
                row_blk,
                pl.BlockSpec((None, D_MODEL, EXPERT_FF), lambda i, te, tb, tv: (te[i], 0, 0)),
                pl.BlockSpec((None, D_MODEL, EXPERT_FF), lambda i, te, tb, tv: (te[i], 0, 0)),
                pl.BlockSpec((None, EXPERT_FF, D_MODEL), lambda i, te, tb, tv: (te[i], 0, 0)),
            ],
            out_specs=row_blk,
            scratch_shapes=[pltpu.VMEM((ROW_TILE, D_MODEL), BF16)],
        ),
        out_shape=jax.ShapeDtypeStruct((N_ROWS * ROW_CHUNKS, LANES), F32),
        compiler_params=pltpu.CompilerParams(
            dimension_semantics=("arbitrary",), vmem_limit_bytes=VMEM_LIMIT_BYTES),
        name="routed_experts",
    )(tile_e, tile_b, tile_v, xs, w_gate, w_up, w_down)


def _final_kernel(idx_ref, rank_ref, poff_ref, h_ref, gffn_ref, p_ref, wt_ref, ys_hbm,
                  wsg_ref, wsu_ref, wsd_ref, wpp_ref, gple_ref, wpg_ref, gfin_ref,
                  out_ref, rows, h3_ref, sem):
    TB = FINAL_TOKENS

    def row_copy(t, k):
        src = pl.multiple_of((poff_ref[idx_ref[k, t]] + rank_ref[k, t]) * ROW_CHUNKS, ROW_CHUNKS)
        dst = pl.multiple_of((k * TB + t) * GATHER_PITCH, SUBLANES)
        return pltpu.make_async_copy(ys_hbm.at[pl.ds(src, ROW_CHUNKS)], rows.at[pl.ds(dst, ROW_CHUNKS)], sem)

    def start_tok(t, c):
        for k in range(TOP_K):
            row_copy(t, k).start()
        return c

    def wait_tok(t, c):
        for k in range(TOP_K):
            row_copy(t, k).wait()
        return c

    lax.fori_loop(0, TB, start_tok, 0)

    h = h_ref[...]
    h2b = _rms(h, gffn_ref[...]).astype(BF16)
    g = _dot(h2b, wsg_ref[...])
    u = _dot(h2b, wsu_ref[...])
    shared = _dot((g * _sigmoid(g) * u).astype(BF16), wsd_ref[...])
    ple = _rms(_dot(p_ref[...].astype(BF16), wpp_ref[...]), gple_ref[...])

    lax.fori_loop(0, TB, wait_tok, 0)
    wt = wt_ref[...]
    for c in range(ROW_CHUNKS):
        cs = slice(c * LANES, (c + 1) * LANES)
        routed = rows[pl.ds(c, TB, stride=GATHER_PITCH), :] * wt[:, 0:1]
        for k in range(1, TOP_K):
            routed = routed + rows[pl.ds(k * TB * GATHER_PITCH + c, TB, stride=GATHER_PITCH), :] * wt[:, k:k + 1]
        h3_ref[:, cs] = h[:, cs] + (routed + shared[:, cs])

    h3 = h3_ref[...]
    gate = _sigmoid(_dot(h3.astype(BF16), wpg_ref[...]))
    h4 = h3 + gate * ple
    out_ref[...] = _rms(h4, gfin_ref[...])


def _final_call(idx3, rank3, poff, h, g_ffn, p, wt, ys, ws_gate, ws_up, ws_down, w_pp, g_ple, w_pg, g_final):
    TB = FINAL_TOKENS
    n_steps = TOKENS // TB
    const = lambda shape: pl.BlockSpec(shape, lambda i: (0,) * len(shape), pipeline_mode=pl.Buffered(1))
    smem_blk = pl.BlockSpec((None, TOP_K, TB), lambda i: (i, 0, 0), memory_space=pltpu.SMEM)
    smem = pl.BlockSpec(memory_space=pltpu.SMEM)
    tok = pl.BlockSpec((TB, D_MODEL), lambda i: (i, 0))
    return pl.pallas_call(
        _final_kernel,
        grid=(n_steps,),
        in_specs=[
            smem_blk, smem_blk, smem,
            tok,
            const((1, D_MODEL)),
            pl.BlockSpec((TB, PLE_DIM), lambda i: (i, 0)),
            pl.BlockSpec((TB, TOP_K), lambda i: (i, 0)),
            pl.BlockSpec(memory_space=pl.ANY),
            const((D_MODEL, SHARED_FF)),
            const((D_MODEL, SHARED_FF)),
            const((SHARED_FF, D_MODEL)),
            const((PLE_DIM, D_MODEL)),
            const((1, D_MODEL)),
            const((D_MODEL, D_MODEL)),
            const((1, D_MODEL)),
        ],
        out_specs=tok,
        out_shape=jax.ShapeDtypeStruct((TOKENS, D_MODEL), F32),
        scratch_shapes=[
            pltpu.VMEM((TOP_K * TB * GATHER_PITCH, LANES), F32),
            pltpu.VMEM((TB, D_MODEL), F32),
            pltpu.SemaphoreType.DMA,
        ],
        compiler_params=pltpu.CompilerParams(
            dimension_semantics=("arbitrary",), vmem_limit_bytes=VMEM_LIMIT_BYTES),
        name="combine_final",
    )(idx3, rank3, poff, h, g_ffn, p, wt, ys, ws_gate, ws_up, ws_down, w_pp, g_ple, w_pg, g_final)


def _per_step(a, tokens_per_step):
    return a.reshape(TOP_K, TOKENS // tokens_per_step, tokens_per_step).transpose(1, 0, 2)


def kernel(x, p, g_mix, w_in, w_pool, pool_scale, attn_sinks, rpe_table, w_out, g_ffn, w_router, router_bias,
           w_gate, w_up, w_down, ws_gate, ws_up, ws_down, w_ple_proj, g_ple, w_ple_gate, g_final):
    bucket = jnp.asarray(_rpe_bucket_map())
    h, h2, idx, topw, rank, cnt = _mix_call(
        x, g_mix[0][None, :], w_in[0].astype(BF16), w_pool[0].astype(BF16), pool_scale[0][None, :],
        attn_sinks[0], rpe_table, bucket, w_out[0].astype(BF16), g_ffn[0][None, :],
        w_router[0].T, router_bias[0][:, None])

    counts = cnt[:, 0].astype(jnp.int32)
    padded = (counts + ROW_TILE - 1) // ROW_TILE * ROW_TILE
    pend = jnp.cumsum(padded).astype(jnp.int32)
    poff = pend - padded
    n_tiles = pend[-1] // ROW_TILE
    tile = jnp.arange(N_ROW_TILES, dtype=jnp.int32)
    tile_v = (tile < n_tiles).astype(jnp.int32)
    tile_b = jnp.minimum(tile, jnp.maximum(n_tiles - 1, 0))
    tile_e = jnp.minimum(jnp.sum(pend[None, :] <= (tile_b * ROW_TILE)[:, None], axis=1), N_EXPERTS - 1).astype(jnp.int32)

    xs = _dispatch_call(_per_step(idx, DISPATCH_TOKENS), _per_step(rank, DISPATCH_TOKENS), poff, pend, h2)
    ys = _expert_call(tile_e, tile_b, tile_v, xs, w_gate[0], w_up[0], w_down[0])
    out = _final_call(
        _per_step(idx, FINAL_TOKENS), _per_step(rank, FINAL_TOKENS), poff,
        h.reshape(TOKENS, D_MODEL), g_ffn[0][None, :], p[0].reshape(TOKENS, PLE_DIM), topw.T, ys,
        ws_gate[0].astype(BF16), ws_up[0].astype(BF16), ws_down[0].astype(BF16),
        w_ple_proj[0].astype(BF16), g_ple[0][None, :], w_ple_gate[0].astype(BF16), g_final[None, :])
    return out.reshape(BATCH, SEQ, D_MODEL)
```

```python
import math

import numpy as np
import jax
import jax.numpy as jnp
from jax import lax
from jax.experimental import pallas as pl
from jax.experimental.pallas import tpu as pltpu

F32 = jnp.float32
BF16 = jnp.bfloat16
NEG_INF = float("-inf")

D_MODEL = 2048
PLE_DIM = 256
POOL_WIDTH = 1024
POOL_GROUPS = 4
POOL_GROUP_DIM = POOL_WIDTH // POOL_GROUPS
POOL_WINDOWS = (2, 4, 8, 16)
N_HEADS = 16
N_KV_HEADS = 2
HEAD_DIM = 64
HEADS_PER_KV = N_HEADS // N_KV_HEADS
ATTN_WIDTH = N_HEADS * HEAD_DIM
KV_WIDTH = N_KV_HEADS * HEAD_DIM
MIX_WIDTH = POOL_WIDTH + ATTN_WIDTH
IN_WIDTH = POOL_WIDTH + ATTN_WIDTH + 2 * KV_WIDTH
Q_OFF = POOL_WIDTH
K_OFF = POOL_WIDTH + ATTN_WIDTH
V_OFF = K_OFF + KV_WIDTH
ATTN_BLOCK = 128
WINDOW = 128
RPE_BUCKETS = 32
RPE_MAX_EXACT = RPE_BUCKETS // 2
RPE_MAX_DISTANCE = 128
N_EXPERTS = 64
TOP_K = 8
N_EXPERT_GROUPS = 8
EXPERTS_PER_GROUP = N_EXPERTS // N_EXPERT_GROUPS
TOPK_GROUPS = 4
EXPERT_FF = 512
SHARED_FF = 512
ROUTED_SCALE = 2.5
EPS = 1e-6

LANES = 128
SUBLANES = 8
VMEM_LIMIT_BYTES = 58 * 1024 * 1024

MIX_TOKENS = 256
POOL_HISTORY = 16
ROW_TILE = 256
DISPATCH_TOKENS = 512
FINAL_TOKENS = 128
WEIGHT_CAST_ROWS = 256
ROW_CHUNKS = D_MODEL // LANES
GATHER_PITCH = 24
GATHER_SLOT_ROWS = TOP_K * FINAL_TOKENS * GATHER_PITCH


def _rms(x, g):
    return x * lax.rsqrt(jnp.mean(x * x, axis=-1, keepdims=True) + EPS) * g


def _sigmoid(x):
    return 1.0 / (1.0 + jnp.exp(-x))


def _dot(a, b):
    return jnp.dot(a, b, preferred_element_type=F32)


def _dot_nt(a, b, precision=None):
    return lax.dot_general(a, b, (((1,), (1,)), ((), ())), preferred_element_type=F32, precision=precision)


def _rpe_bucket_map():
    i = np.arange(ATTN_BLOCK)[:, None]
    j = np.arange(2 * ATTN_BLOCK)[None, :]
    dist = i + ATTN_BLOCK - j
    n = np.maximum(dist, 0)
    nf = np.maximum(n, 1).astype(np.float32)
    large = RPE_MAX_EXACT + (np.log(nf / np.float32(RPE_MAX_EXACT)) / np.float32(math.log(RPE_MAX_DISTANCE / RPE_MAX_EXACT))
                             * np.float32(RPE_BUCKETS - RPE_MAX_EXACT)).astype(np.int32)
    large = np.minimum(large, RPE_BUCKETS - 1)
    bucket = np.where(n < RPE_MAX_EXACT, n, large)
    valid = (dist >= 0) & (dist < WINDOW)
    return np.where(valid, bucket, -1).astype(np.int32)


def _mix_kernel(x_ref, gmix_ref, win_ref, wpool_ref, pscale_ref, sinks_ref, rpe_ref, bucket_ref,
                wout_ref, gffn_ref, wrt_ref, rbias_ref,
                h_ref, h2_ref, idx_ref, topw_ref, rank_ref, cnt_ref,
                ubuf, kbuf, vbuf, bias_buf, ybuf, cnt_acc):
    TS = MIX_TOKENS
    b = pl.program_id(0)
    s = pl.program_id(1)

    @pl.when((b == 0) & (s == 0))
    def _init():
        bucket = bucket_ref[...]
        for h in range(N_HEADS):
            acc = jnp.full((ATTN_BLOCK, 2 * ATTN_BLOCK), NEG_INF, F32)
            for bk in range(RPE_BUCKETS):
                acc = jnp.where(bucket == bk, rpe_ref[bk, h], acc)
            bias_buf[h * ATTN_BLOCK:(h + 1) * ATTN_BLOCK, :] = acc
        cnt_acc[...] = jnp.zeros_like(cnt_acc)

    @pl.when(s == 0)
    def _reset_history():
        ubuf[0:POOL_HISTORY, :] = jnp.zeros((POOL_HISTORY, POOL_WIDTH), F32)
        kbuf[0:ATTN_BLOCK, :] = jnp.zeros((ATTN_BLOCK, KV_WIDTH), BF16)
        vbuf[0:ATTN_BLOCK, :] = jnp.zeros((ATTN_BLOCK, KV_WIDTH), BF16)

    x = x_ref[...]
    a = _rms(x, gmix_ref[...]).astype(BF16)
    z = _dot(a, win_ref[...])

    ubuf[POOL_HISTORY:POOL_HISTORY + TS, :] = z[:, 0:POOL_WIDTH]
    pos = s * TS + lax.broadcasted_iota(jnp.int32, (TS, 1), 0)
    for gi, w in enumerate(POOL_WINDOWS):
        c0, c1 = gi * POOL_GROUP_DIM, (gi + 1) * POOL_GROUP_DIM
        e = ubuf[:, c0:c1]
        shift = 1
        while shift < w:
            e = e + pltpu.roll(e, shift, axis=0)
            shift *= 2
        wsum = e[POOL_HISTORY:, :]
        count = jnp.minimum(pos + 1, w).astype(F32)
        pooled = wsum / count - z[:, c0:c1]
        yp = _dot(pooled.astype(BF16), wpool_ref[gi]) * pscale_ref[:, c0:c1]
        ybuf[:, c0:c1] = yp.astype(BF16)
    ubuf[0:POOL_HISTORY, :] = ubuf[TS:TS + POOL_HISTORY, :]

    kbuf[ATTN_BLOCK:ATTN_BLOCK + TS, :] = z[:, K_OFF:K_OFF + KV_WIDTH].astype(BF16)
    vbuf[ATTN_BLOCK:ATTN_BLOCK + TS, :] = z[:, V_OFF:V_OFF + KV_WIDTH].astype(BF16)
    lane = lax.broadcasted_iota(jnp.int32, (ATTN_BLOCK, LANES), 1)
    low_half = lane < HEAD_DIM
    high_half = jnp.logical_not(low_half)
    col = lax.broadcasted_iota(jnp.int32, (1, 2 * ATTN_BLOCK), 1)
    first_mask = jnp.where((col < ATTN_BLOCK) & (s == 0), NEG_INF, 0.0).astype(F32)
    for sb in range(TS // ATTN_BLOCK):
        r0 = sb * ATTN_BLOCK
        kband = kbuf[r0:r0 + 2 * ATTN_BLOCK, :]
        vband = vbuf[r0:r0 + 2 * ATTN_BLOCK, :]
        for p in range(N_HEADS // 2):
            qp = z[r0:r0 + ATTN_BLOCK, Q_OFF + p * LANES:Q_OFF + (p + 1) * LANES] * (HEAD_DIM ** -0.5)
            qr = pltpu.roll(qp, HEAD_DIM, axis=1)
            kvh = (2 * p) // HEADS_PER_KV
            kv_lanes = low_half if kvh == 0 else high_half
            outs = []
            for par in range(2):
                h = 2 * p + par
                qh = jnp.where(kv_lanes, qp if par == kvh else qr, 0.0).astype(BF16)
                lg = _dot_nt(qh, kband) + bias_buf[h * ATTN_BLOCK:(h + 1) * ATTN_BLOCK, :]
                if sb == 0:
                    lg = lg + first_mask
                sink = sinks_ref[h]
                m = jnp.maximum(jnp.max(lg, axis=-1, keepdims=True), sink)
                pe = jnp.exp(lg - m)
                den = jnp.sum(pe, axis=-1, keepdims=True) + jnp.exp(sink - m)
                outs.append(_dot(pe.astype(BF16), vband) / den)
            if kvh == 0:
                pair = jnp.where(low_half, outs[0], pltpu.roll(outs[1], HEAD_DIM, axis=1))
            else:
                pair = jnp.where(low_half, pltpu.roll(outs[0], HEAD_DIM, axis=1), outs[1])
            ybuf[r0:r0 + ATTN_BLOCK, POOL_WIDTH + p * LANES:POOL_WIDTH + (p + 1) * LANES] = pair.astype(BF16)
    kbuf[0:ATTN_BLOCK, :] = kbuf[TS:TS + ATTN_BLOCK, :]
    vbuf[0:ATTN_BLOCK, :] = vbuf[TS:TS + ATTN_BLOCK, :]

    h = x + _dot(ybuf[...], wout_ref[...])
    h_ref[...] = h
    h2 = _rms(h, gffn_ref[...])
    for c in range(ROW_CHUNKS):
        h2_ref[pl.ds(c, TS, stride=ROW_CHUNKS), :] = h2[:, c * LANES:(c + 1) * LANES]

    logits = _dot_nt(wrt_ref[...], h2, precision=lax.Precision.HIGHEST)
    scores = _sigmoid(logits)
    biased = scores + rbias_ref[...]
    erow = lax.broadcasted_iota(jnp.int32, (N_EXPERTS, TS), 0)
    grow = lax.broadcasted_iota(jnp.int32, (EXPERTS_PER_GROUP, TS), 0)
    group_scores = []
    for g in range(N_EXPERT_GROUPS):
        blk = biased[g * EXPERTS_PER_GROUP:(g + 1) * EXPERTS_PER_GROUP, :]
        m1 = jnp.max(blk, axis=0, keepdims=True)
        i1 = jnp.min(jnp.where(blk == m1, grow, EXPERTS_PER_GROUP), axis=0, keepdims=True)
        m2 = jnp.max(jnp.where(grow == i1, NEG_INF, blk), axis=0, keepdims=True)
        group_scores.append(m1 + m2)
    cur = jnp.concatenate(group_scores, axis=0)
    gsel = jnp.zeros((N_EXPERT_GROUPS, TS), jnp.bool_)
    for _ in range(TOPK_GROUPS):
        m = jnp.max(cur, axis=0, keepdims=True)
        i = jnp.min(jnp.where(cur == m, grow, N_EXPERT_GROUPS), axis=0, keepdims=True)
        hit = grow == i
        gsel = jnp.logical_or(gsel, hit)
        cur = jnp.where(hit, NEG_INF, cur)
    gmask = jnp.concatenate(
        [jnp.broadcast_to(gsel[g:g + 1, :], (EXPERTS_PER_GROUP, TS)) for g in range(N_EXPERT_GROUPS)], axis=0)
    masked = jnp.where(gmask, biased, NEG_INF)
    sel = jnp.zeros((N_EXPERTS, TS), jnp.bool_)
    idxs, ws = [], []
    for _ in range(TOP_K):
        m = jnp.max(masked, axis=0, keepdims=True)
        i = jnp.min(jnp.where(masked == m, erow, N_EXPERTS), axis=0, keepdims=True)
        hit = erow == i
        idxs.append(i)
        ws.append(jnp.sum(jnp.where(hit, scores, 0.0), axis=0, keepdims=True))
        sel = jnp.logical_or(sel, hit)
        masked = jnp.where(hit, NEG_INF, masked)
    wtot = ws[0]
    for wk in ws[1:]:
        wtot = wtot + wk
    idx_ref[...] = jnp.concatenate(idxs, axis=0)
    topw_ref[...] = jnp.concatenate([wk / wtot * ROUTED_SCALE for wk in ws], axis=0)

    self32 = sel.astype(F32)
    ri = lax.broadcasted_iota(jnp.int32, (TS, TS), 0)
    ci = lax.broadcasted_iota(jnp.int32, (TS, TS), 1)
    before = (ri < ci).astype(BF16)
    running = _dot(self32.astype(BF16), before) + cnt_acc[:, 0:1]
    rank_ref[...] = jnp.concatenate(
        [jnp.sum(jnp.where(erow == i, running, 0.0), axis=0, keepdims=True) for i in idxs], axis=0).astype(jnp.int32)
    cnt_acc[...] = cnt_acc[...] + jnp.sum(self32, axis=1, keepdims=True)
    cnt_ref[...] = cnt_acc[...]


def _mix_call(x, g_mix, w_in, w_pool, pool_scale, sinks, rpe_table, bucket, w_out, g_ffn, w_rt, r_bias):
    batch, seq, _ = x.shape
    tokens = batch * seq
    TS = MIX_TOKENS
    ns = seq // TS
    const = lambda shape: pl.BlockSpec(shape, lambda b, s: (0,) * len(shape), pipeline_mode=pl.Buffered(1))
    smem = pl.BlockSpec(memory_space=pltpu.SMEM)
    tok3 = pl.BlockSpec((None, TS, D_MODEL), lambda b, s: (b, s, 0))
    lane_blk = pl.BlockSpec((TOP_K, TS), lambda b, s: (0, b * ns + s))
    return pl.pallas_call(
        _mix_kernel,
        grid=(batch, ns),
        in_specs=[
            tok3,
            const((1, D_MODEL)),
            const((D_MODEL, IN_WIDTH)),
            const((POOL_GROUPS, POOL_GROUP_DIM, POOL_GROUP_DIM)),
            const((1, POOL_WIDTH)),
            smem,
            smem,
            const((ATTN_BLOCK, 2 * ATTN_BLOCK)),
            const((MIX_WIDTH, D_MODEL)),
            const((1, D_MODEL)),
            const((N_EXPERTS, D_MODEL)),
            const((N_EXPERTS, 1)),
        ],
        out_specs=[
            tok3,
            pl.BlockSpec((TS * ROW_CHUNKS, LANES), lambda b, s: (b * ns + s, 0)),
            lane_blk,
            lane_blk,
            lane_blk,
            pl.BlockSpec((N_EXPERTS, LANES), lambda b, s: (0, 0)),
        ],
        out_shape=[
            jax.ShapeDtypeStruct((batch, seq, D_MODEL), F32),
            jax.ShapeDtypeStruct((tokens * ROW_CHUNKS, LANES), F32),
            jax.ShapeDtypeStruct((TOP_K, tokens), jnp.int32),
            jax.ShapeDtypeStruct((TOP_K, tokens), F32),
            jax.ShapeDtypeStruct((TOP_K, tokens), jnp.int32),
            jax.ShapeDtypeStruct((N_EXPERTS, LANES), F32),
        ],
        scratch_shapes=[
            pltpu.VMEM((POOL_HISTORY + TS, POOL_WIDTH), F32),
            pltpu.VMEM((ATTN_BLOCK + TS, KV_WIDTH), BF16),
            pltpu.VMEM((ATTN_BLOCK + TS, KV_WIDTH), BF16),
            pltpu.VMEM((N_HEADS * ATTN_BLOCK, 2 * ATTN_BLOCK), F32),
            pltpu.VMEM((TS, MIX_WIDTH), BF16),
            pltpu.VMEM((N_EXPERTS, LANES), F32),
        ],
        compiler_params=pltpu.CompilerParams(
            dimension_semantics=("arbitrary", "arbitrary"), vmem_limit_bytes=VMEM_LIMIT_BYTES),
        name="mix_router",
    )(x, g_mix, w_in, w_pool, pool_scale, sinks, rpe_table, bucket, w_out, g_ffn, w_rt, r_bias)


def _dispatch_kernel(dest_ref, poff_ref, pend_ref, h2_ref, xs_hbm, zbuf, sem, zsem):
    step = pl.program_id(0)

    def zero_copy(e):
        start = pl.multiple_of((pend_ref[e] - ROW_TILE) * ROW_CHUNKS, ROW_CHUNKS)
        return pltpu.make_async_copy(zbuf, xs_hbm.at[pl.ds(start, ROW_TILE * ROW_CHUNKS)], zsem)

    @pl.when(step == 0)
    def _zero_tails():
        zbuf[...] = jnp.zeros_like(zbuf)

        def start(e, c):
            @pl.when(pend_ref[e] > poff_ref[e])
            def _():
                zero_copy(e).start()
            return c

        def wait(e, c):
            @pl.when(pend_ref[e] > poff_ref[e])
            def _():
                zero_copy(e).wait()
            return c

        lax.fori_loop(0, N_EXPERTS, start, 0)
        lax.fori_loop(0, N_EXPERTS, wait, 0)

    def row_copy(t, k):
        src = pl.multiple_of(t * ROW_CHUNKS, ROW_CHUNKS)
        dest = pl.multiple_of(dest_ref[k, t], ROW_CHUNKS)
        return pltpu.make_async_copy(h2_ref.at[pl.ds(src, ROW_CHUNKS)], xs_hbm.at[pl.ds(dest, ROW_CHUNKS)], sem)

    def start_tok(t, c):
        for k in range(TOP_K):
            row_copy(t, k).start(priority=k % 2)
        return c

    def wait_tok(t, c):
        for k in range(TOP_K):
            row_copy(t, k).wait()
        return c

    lax.fori_loop(0, DISPATCH_TOKENS, start_tok, 0, unroll=4)
    lax.fori_loop(0, DISPATCH_TOKENS, wait_tok, 0)


def _dispatch_call(dest3, poff, pend, h2, n_rows):
    n_steps = dest3.shape[0]
    smem_blk = pl.BlockSpec((None, TOP_K, DISPATCH_TOKENS), lambda i: (i, 0, 0), memory_space=pltpu.SMEM)
    smem = pl.BlockSpec(memory_space=pltpu.SMEM)
    return pl.pallas_call(
        _dispatch_kernel,
        grid=(n_steps,),
        in_specs=[smem_blk, smem, smem,
                  pl.BlockSpec((DISPATCH_TOKENS * ROW_CHUNKS, LANES), lambda i: (i, 0))],
        out_specs=pl.BlockSpec(memory_space=pl.ANY),
        out_shape=jax.ShapeDtypeStruct((n_rows * ROW_CHUNKS, LANES), F32),
        scratch_shapes=[
            pltpu.VMEM((ROW_TILE * ROW_CHUNKS, LANES), F32),
            pltpu.SemaphoreType.DMA,
            pltpu.SemaphoreType.DMA,
        ],
        compiler_params=pltpu.CompilerParams(dimension_semantics=("arbitrary",)),
        name="dispatch_rows",
    )(dest3, poff, pend, h2)


def _expert_kernel(te_ref, tb_ref, tv_ref, tfirst_ref, tnext_ref, tslot_ref,
                   xs_ref, wg_hbm, wu_hbm, wd_hbm, ys_ref,
                   xb_ref, wg_stage, wu_stage, wd_stage, wg_bf, wu_bf, wd_bf, sems):
    i = pl.program_id(0)

    def weight_copies(e, slot):
        return (pltpu.make_async_copy(wg_hbm.at[e], wg_stage.at[slot], sems.at[slot, 0]),
                pltpu.make_async_copy(wu_hbm.at[e], wu_stage.at[slot], sems.at[slot, 1]),
                pltpu.make_async_copy(wd_hbm.at[e], wd_stage.at[slot], sems.at[slot, 2]))

    @pl.when((i == 0) & (tv_ref[0] == 1))
    def _prologue():
        for cp in weight_copies(te_ref[0], tslot_ref[0]):
            cp.start()

    @pl.when((tv_ref[i] == 1) & (tfirst_ref[i] == 1))
    def _new_expert():
        slot = tslot_ref[i]
        for cp in weight_copies(te_ref[i], slot):
            cp.wait()

        @pl.when(tnext_ref[i] >= 0)
        def _():
            for cp in weight_copies(tnext_ref[i], 1 - slot):
                cp.start()

        def cast_in(j, c):
            r = pl.multiple_of(j * WEIGHT_CAST_ROWS, WEIGHT_CAST_ROWS)
            wg_bf[pl.ds(r, WEIGHT_CAST_ROWS), :] = wg_stage[slot, pl.ds(r, WEIGHT_CAST_ROWS), :].astype(BF16)
            wu_bf[pl.ds(r, WEIGHT_CAST_ROWS), :] = wu_stage[slot, pl.ds(r, WEIGHT_CAST_ROWS), :].astype(BF16)
            return c

        def cast_out(j, c):
            r = pl.multiple_of(j * (WEIGHT_CAST_ROWS // 4), WEIGHT_CAST_ROWS // 4)
            wd_bf[pl.ds(r, WEIGHT_CAST_ROWS // 4), :] = wd_stage[slot, pl.ds(r, WEIGHT_CAST_ROWS // 4), :].astype(BF16)
            return c

        lax.fori_loop(0, D_MODEL // WEIGHT_CAST_ROWS, cast_in, 0)
        lax.fori_loop(0, EXPERT_FF // (WEIGHT_CAST_ROWS // 4), cast_out, 0)

    @pl.when(tv_ref[i] == 1)
    def _tile():
        for c in range(ROW_CHUNKS):
            xb_ref[:, c * LANES:(c + 1) * LANES] = xs_ref[pl.ds(c, ROW_TILE, stride=ROW_CHUNKS), :].astype(BF16)
        xb = xb_ref[...]
        g = _dot(xb, wg_bf[...])
        u = _dot(xb, wu_bf[...])
        hb = (g * _sigmoid(g) * u).astype(BF16)
        y = _dot(hb, wd_bf[...])
        for c in range(ROW_CHUNKS):
            ys_ref[pl.ds(c, ROW_TILE, stride=ROW_CHUNKS), :] = y[:, c * LANES:(c + 1) * LANES]


def _expert_call(tile_meta, xs, w_gate, w_up, w_down):
    n_tiles = tile_meta[0].shape[0]
    n_rows = xs.shape[0] // ROW_CHUNKS
    row_blk = pl.BlockSpec((ROW_TILE * ROW_CHUNKS, LANES), lambda i, te, tb, *_: (tb[i], 0))
    any_spec = pl.BlockSpec(memory_space=pl.ANY)
    return pl.pallas_call(
        _expert_kernel,
        grid_spec=pltpu.PrefetchScalarGridSpec(
            num_scalar_prefetch=len(tile_meta),
            grid=(n_tiles,),
            in_specs=[row_blk, any_spec, any_spec, any_spec],
            out_specs=row_blk,
            scratch_shapes=[
                pltpu.VMEM((ROW_TILE, D_MODEL), BF16),
                pltpu.VMEM((2, D_MODEL, EXPERT_FF), F32),
                pltpu.VMEM((2, D_MODEL, EXPERT_FF), F32),
                pltpu.VMEM((2, EXPERT_FF, D_MODEL), F32),
                pltpu.VMEM((D_MODEL, EXPERT_FF), BF16),
                pltpu.VMEM((D_MODEL, EXPERT_FF), BF16),
                pltpu.VMEM((EXPERT_FF, D_MODEL), BF16),
                pltpu.SemaphoreType.DMA((2, 3)),
            ],
        ),
        out_shape=jax.ShapeDtypeStruct((n_rows * ROW_CHUNKS, LANES), F32),
        compiler_params=pltpu.CompilerParams(
            dimension_semantics=("arbitrary",), vmem_limit_bytes=VMEM_LIMIT_BYTES),
        name="routed_experts",
    )(*tile_meta, xs, w_gate, w_up, w_down)


def _final_kernel(src_ref, src_next_ref,
                  h_ref, gffn_ref, p_ref, wt_ref, ys_hbm,
                  wsg_ref, wsu_ref, wsd_ref, wpp_ref, gple_ref, wpg_ref, gfin_ref,
                  out_ref, rows, h3_ref, sems):
    TB = FINAL_TOKENS
    i = pl.program_id(0)
    n = pl.num_programs(0)

    def row_copy(src_r, slot, t, k):
        src = pl.multiple_of(src_r[k, t], ROW_CHUNKS)
        dst = pl.multiple_of(slot * GATHER_SLOT_ROWS + (k * TB + t) * GATHER_PITCH, SUBLANES)
        return pltpu.make_async_copy(ys_hbm.at[pl.ds(src, ROW_CHUNKS)], rows.at[pl.ds(dst, ROW_CHUNKS)], sems.at[slot])

    def start_block(src_r, slot):
        def body(t, c):
            for k in range(TOP_K):
                row_copy(src_r, slot, t, k).start(priority=k % 2)
            return c
        lax.fori_loop(0, TB, body, 0, unroll=4)

    def wait_block(src_r, slot):
        def body(t, c):
            for k in range(TOP_K):
                row_copy(src_r, slot, t, k).wait()
            return c
        lax.fori_loop(0, TB, body, 0)

    slot = lax.rem(i, 2)

    @pl.when(i == 0)
    def _first():
        start_block(src_ref, slot)

    @pl.when(i + 1 < n)
    def _prefetch():
        start_block(src_next_ref, 1 - slot)

    h = h_ref[...]
    h2b = _rms(h, gffn_ref[...]).astype(BF16)
    g = _dot(h2b, wsg_ref[...])
    u = _dot(h2b, wsu_ref[...])
    shared = _dot((g * _sigmoid(g) * u).astype(BF16), wsd_ref[...])
    ple = _rms(_dot(p_ref[...].astype(BF16), wpp_ref[...]), gple_ref[...])

    wait_block(src_ref, slot)
    base = slot * GATHER_SLOT_ROWS
    wt = wt_ref[...]
    for c in range(ROW_CHUNKS):
        cs = slice(c * LANES, (c + 1) * LANES)
        routed = rows[pl.ds(base + c, TB, stride=GATHER_PITCH), :] * wt[:, 0:1]
        for k in range(1, TOP_K):
            routed = routed + rows[pl.ds(base + k * TB * GATHER_PITCH + c, TB, stride=GATHER_PITCH), :] * wt[:, k:k + 1]
        h3_ref[:, cs] = h[:, cs] + (routed + shared[:, cs])

    h3 = h3_ref[...]
    gate = _sigmoid(_dot(h3.astype(BF16), wpg_ref[...]))
    h4 = h3 + gate * ple
    out_ref[...] = _rms(h4, gfin_ref[...])


def _final_call(src3, h, g_ffn, p, wt, ys, ws_gate, ws_up, ws_down, w_pp, g_ple, w_pg, g_final):
    TB = FINAL_TOKENS
    n_steps = src3.shape[0]
    tokens = h.shape[0]
    const = lambda shape: pl.BlockSpec(shape, lambda i: (0,) * len(shape), pipeline_mode=pl.Buffered(1))
    smem_blk = pl.BlockSpec((None, TOP_K, TB), lambda i: (i, 0, 0), memory_space=pltpu.SMEM)
    smem_next = pl.BlockSpec((None, TOP_K, TB), lambda i: (jnp.minimum(i + 1, n_steps - 1), 0, 0),
                             memory_space=pltpu.SMEM)
    tok = pl.BlockSpec((TB, D_MODEL), lambda i: (i, 0))
    return pl.pallas_call(
        _final_kernel,
        grid=(n_steps,),
        in_specs=[
            smem_blk, smem_next,
            tok,
            const((1, D_MODEL)),
            pl.BlockSpec((TB, PLE_DIM), lambda i: (i, 0)),
            pl.BlockSpec((TB, TOP_K), lambda i: (i, 0)),
            pl.BlockSpec(memory_space=pl.ANY),
            const((D_MODEL, SHARED_FF)),
            const((D_MODEL, SHARED_FF)),
            const((SHARED_FF, D_MODEL)),
            const((PLE_DIM, D_MODEL)),
            const((1, D_MODEL)),
            const((D_MODEL, D_MODEL)),
            const((1, D_MODEL)),
        ],
        out_specs=tok,
        out_shape=jax.ShapeDtypeStruct((tokens, D_MODEL), F32),
        scratch_shapes=[
            pltpu.VMEM((2 * GATHER_SLOT_ROWS, LANES), F32),
            pltpu.VMEM((TB, D_MODEL), F32),
            pltpu.SemaphoreType.DMA((2,)),
        ],
        compiler_params=pltpu.CompilerParams(
            dimension_semantics=("arbitrary",), vmem_limit_bytes=VMEM_LIMIT_BYTES),
        name="combine_final",
    )(src3, src3, h, g_ffn, p, wt, ys, ws_gate, ws_up, ws_down, w_pp, g_ple, w_pg, g_final)


def _per_step(a, tokens_per_step):
    tokens = a.shape[1]
    return a.reshape(TOP_K, tokens // tokens_per_step, tokens_per_step).transpose(1, 0, 2)


def _tile_metadata(counts, n_tiles_max):
    i32 = jnp.int32
    padded = (counts + ROW_TILE - 1) // ROW_TILE * ROW_TILE
    pend = jnp.cumsum(padded).astype(i32)
    poff = pend - padded
    n_tiles = pend[-1] // ROW_TILE
    tile = jnp.arange(n_tiles_max, dtype=i32)
    tile_v = (tile < n_tiles).astype(i32)
    tile_b = jnp.minimum(tile, jnp.maximum(n_tiles - 1, 0))
    tile_e = jnp.minimum(jnp.sum(pend[None, :] <= (tile_b * ROW_TILE)[:, None], axis=1), N_EXPERTS - 1).astype(i32)
    tile_first = (tile_v * (tile_b * ROW_TILE == poff[tile_e])).astype(i32)
    nonempty = counts > 0
    order = jnp.cumsum(nonempty.astype(i32)) - 1
    ids = jnp.where(nonempty, jnp.arange(N_EXPERTS, dtype=i32), N_EXPERTS)
    later = jnp.flip(lax.cummin(jnp.flip(ids)))
    next_e = jnp.concatenate([later[1:], jnp.full((1,), N_EXPERTS, i32)])
    next_e = jnp.where(next_e == N_EXPERTS, -1, next_e)
    tile_next = next_e[tile_e].astype(i32)
    tile_slot = (order[tile_e] % 2).astype(i32)
    return poff, pend, (tile_e, tile_b, tile_v, tile_first, tile_next, tile_slot)


def kernel(x, p, g_mix, w_in, w_pool, pool_scale, attn_sinks, rpe_table, w_out, g_ffn, w_router, router_bias,
           w_gate, w_up, w_down, ws_gate, ws_up, ws_down, w_ple_proj, g_ple, w_ple_gate, g_final):
    batch, seq, _ = x.shape
    tokens = batch * seq
    assert seq % MIX_TOKENS == 0 and tokens % DISPATCH_TOKENS == 0 and tokens % FINAL_TOKENS == 0
    n_rows = tokens * TOP_K + N_EXPERTS * ROW_TILE
    bucket = jnp.asarray(_rpe_bucket_map())
    h, h2, idx, topw, rank, cnt = _mix_call(
        x, g_mix[0][None, :], w_in[0].astype(BF16), w_pool[0].astype(BF16), pool_scale[0][None, :],
        attn_sinks[0], rpe_table, bucket, w_out[0].astype(BF16), g_ffn[0][None, :],
        w_router[0].T, router_bias[0][:, None])

    poff, pend, tile_meta = _tile_metadata(cnt[:, 0].astype(jnp.int32), n_rows // ROW_TILE)
    experts = jnp.arange(N_EXPERTS, dtype=jnp.int32)[:, None, None]
    dest = (jnp.sum(jnp.where(idx[None] == experts, poff[:, None, None], 0), axis=0) + rank) * ROW_CHUNKS
    xs = _dispatch_call(_per_step(dest, DISPATCH_TOKENS), poff, pend, h2, n_rows)
    ys = _expert_call(tile_meta, xs, w_gate[0], w_up[0], w_down[0])
    out = _final_call(
        _per_step(dest, FINAL_TOKENS),
        h.reshape(tokens, D_MODEL), g_ffn[0][None, :], p[0].reshape(tokens, PLE_DIM), topw.T, ys,
        ws_gate[0].astype(BF16), ws_up[0].astype(BF16), ws_down[0].astype(BF16),
        w_ple_proj[0].astype(BF16), g_ple[0][None, :], w_ple_gate[0].astype(BF16), g_final[None, :])
    return out.reshape(batch, seq, D_MODEL)
```

```python
import math

import numpy as np
import jax
import jax.numpy as jnp
from jax import lax
from jax.experimental import pallas as pl
from jax.experimental.pallas import tpu as pltpu

F32 = jnp.float32
BF16 = jnp.bfloat16
NEG_INF = float("-inf")

D_MODEL = 2048
PLE_DIM = 256
POOL_WIDTH = 1024
POOL_GROUPS = 4
POOL_GROUP_DIM = POOL_WIDTH // POOL_GROUPS
POOL_WINDOWS = (2, 4, 8, 16)
N_HEADS = 16
N_KV_HEADS = 2
HEAD_DIM = 64
HEADS_PER_KV = N_HEADS // N_KV_HEADS
ATTN_WIDTH = N_HEADS * HEAD_DIM
KV_WIDTH = N_KV_HEADS * HEAD_DIM
MIX_WIDTH = POOL_WIDTH + ATTN_WIDTH
IN_WIDTH = POOL_WIDTH + ATTN_WIDTH + 2 * KV_WIDTH
Q_OFF = POOL_WIDTH
K_OFF = POOL_WIDTH + ATTN_WIDTH
V_OFF = K_OFF + KV_WIDTH
ATTN_BLOCK = 128
WINDOW = 128
RPE_BUCKETS = 32
RPE_MAX_EXACT = RPE_BUCKETS // 2
RPE_MAX_DISTANCE = 128
N_EXPERTS = 64
TOP_K = 8
N_EXPERT_GROUPS = 8
EXPERTS_PER_GROUP = N_EXPERTS // N_EXPERT_GROUPS
TOPK_GROUPS = 4
EXPERT_FF = 512
SHARED_FF = 512
ROUTED_SCALE = 2.5
EPS = 1e-6

LANES = 128
SUBLANES = 8
VMEM_LIMIT_BYTES = 58 * 1024 * 1024

MIX_TOKENS = 256
POOL_HISTORY = 16
ROW_TILE = 256
DISPATCH_TOKENS = 512
FINAL_TOKENS = 256
WEIGHT_CAST_ROWS = 256
HALF_MODEL = D_MODEL // 2
ROW_CHUNKS = HALF_MODEL // LANES
GATHER_SLOT_ROWS = TOP_K * FINAL_TOKENS * ROW_CHUNKS
U32 = jnp.uint32


def _rms(x, g):
    return x * lax.rsqrt(jnp.mean(x * x, axis=-1, keepdims=True) + EPS) * g


def _sigmoid(x):
    return 1.0 / (1.0 + jnp.exp(-x))


def _pack_bf16_pair(lo, hi):
    ulo = lax.bitcast_convert_type(lo.astype(BF16).astype(F32), U32)
    uhi = lax.bitcast_convert_type(hi.astype(BF16).astype(F32), U32)
    return (ulo >> 16) | uhi


def _unpack_bf16_pair(w):
    lo = lax.bitcast_convert_type(w << 16, F32)
    hi = lax.bitcast_convert_type(w & jnp.uint32(0xFFFF0000), F32)
    return lo, hi


def _dot(a, b):
    return jnp.dot(a, b, preferred_element_type=F32)


def _dot_nt(a, b, precision=None):
    return lax.dot_general(a, b, (((1,), (1,)), ((), ())), preferred_element_type=F32, precision=precision)


def _rpe_bucket_map():
    i = np.arange(ATTN_BLOCK)[:, None]
    j = np.arange(2 * ATTN_BLOCK)[None, :]
    dist = i + ATTN_BLOCK - j
    n = np.maximum(dist, 0)
    nf = np.maximum(n, 1).astype(np.float32)
    large = RPE_MAX_EXACT + (np.log(nf / np.float32(RPE_MAX_EXACT)) / np.float32(math.log(RPE_MAX_DISTANCE / RPE_MAX_EXACT))
                             * np.float32(RPE_BUCKETS - RPE_MAX_EXACT)).astype(np.int32)
    large = np.minimum(large, RPE_BUCKETS - 1)
    bucket = np.where(n < RPE_MAX_EXACT, n, large)
    valid = (dist >= 0) & (dist < WINDOW)
    return np.where(valid, bucket, -1).astype(np.int32)


def _mix_kernel(x_ref, gmix_ref, win_ref, wpool_ref, pscale_ref, sinks_ref, rpe_ref, bucket_ref,
                wout_ref, gffn_ref, wrt_ref, rbias_ref,
                h_ref, h2_ref, idx_ref, topw_ref, rank_ref, cnt_ref,
                ubuf, kbuf, vbuf, bias_buf, ybuf, cnt_acc):
    TS = MIX_TOKENS
    b = pl.program_id(0)
    s = pl.program_id(1)

    @pl.when((b == 0) & (s == 0))
    def _init():
        bucket = bucket_ref[...]
        for h in range(N_HEADS):
            acc = jnp.full((ATTN_BLOCK, 2 * ATTN_BLOCK), NEG_INF, F32)
            for bk in range(RPE_BUCKETS):
                acc = jnp.where(bucket == bk, rpe_ref[bk, h], acc)
            bias_buf[h * ATTN_BLOCK:(h + 1) * ATTN_BLOCK, :] = acc
        cnt_acc[...] = jnp.zeros_like(cnt_acc)

    @pl.when(s == 0)
    def _reset_history():
        ubuf[0:POOL_HISTORY, :] = jnp.zeros((POOL_HISTORY, POOL_WIDTH), F32)
        kbuf[0:ATTN_BLOCK, :] = jnp.zeros((ATTN_BLOCK, KV_WIDTH), BF16)
        vbuf[0:ATTN_BLOCK, :] = jnp.zeros((ATTN_BLOCK, KV_WIDTH), BF16)

    x = x_ref[...]
    a = _rms(x, gmix_ref[...]).astype(BF16)
    z = _dot(a, win_ref[...])

    ubuf[POOL_HISTORY:POOL_HISTORY + TS, :] = z[:, 0:POOL_WIDTH]
    pos = s * TS + lax.broadcasted_iota(jnp.int32, (TS, 1), 0)
    for gi, w in enumerate(POOL_WINDOWS):
        c0, c1 = gi * POOL_GROUP_DIM, (gi + 1) * POOL_GROUP_DIM
        e = ubuf[:, c0:c1]
        shift = 1
        while shift < w:
            e = e + pltpu.roll(e, shift, axis=0)
            shift *= 2
        wsum = e[POOL_HISTORY:, :]
        count = jnp.minimum(pos + 1, w).astype(F32)
        pooled = wsum / count - z[:, c0:c1]
        yp = _dot(pooled.astype(BF16), wpool_ref[gi]) * pscale_ref[:, c0:c1]
        ybuf[:, c0:c1] = yp.astype(BF16)
    ubuf[0:POOL_HISTORY, :] = ubuf[TS:TS + POOL_HISTORY, :]

    kbuf[ATTN_BLOCK:ATTN_BLOCK + TS, :] = z[:, K_OFF:K_OFF + KV_WIDTH].astype(BF16)
    vbuf[ATTN_BLOCK:ATTN_BLOCK + TS, :] = z[:, V_OFF:V_OFF + KV_WIDTH].astype(BF16)
    lane = lax.broadcasted_iota(jnp.int32, (ATTN_BLOCK, LANES), 1)
    low_half = lane < HEAD_DIM
    high_half = jnp.logical_not(low_half)
    col = lax.broadcasted_iota(jnp.int32, (1, 2 * ATTN_BLOCK), 1)
    first_mask = jnp.where((col < ATTN_BLOCK) & (s == 0), NEG_INF, 0.0).astype(F32)
    for sb in range(TS // ATTN_BLOCK):
        r0 = sb * ATTN_BLOCK
        kband = kbuf[r0:r0 + 2 * ATTN_BLOCK, :]
        vband = vbuf[r0:r0 + 2 * ATTN_BLOCK, :]
        for p in range(N_HEADS // 2):
            qp = z[r0:r0 + ATTN_BLOCK, Q_OFF + p * LANES:Q_OFF + (p + 1) * LANES] * (HEAD_DIM ** -0.5)
            qr = pltpu.roll(qp, HEAD_DIM, axis=1)
            kvh = (2 * p) // HEADS_PER_KV
            kv_lanes = low_half if kvh == 0 else high_half
            outs = []
            for par in range(2):
                h = 2 * p + par
                qh = jnp.where(kv_lanes, qp if par == kvh else qr, 0.0).astype(BF16)
                lg = _dot_nt(qh, kband) + bias_buf[h * ATTN_BLOCK:(h + 1) * ATTN_BLOCK, :]
                if sb == 0:
                    lg = lg + first_mask
                sink = sinks_ref[h]
                m = jnp.maximum(jnp.max(lg, axis=-1, keepdims=True), sink)
                pe = jnp.exp(lg - m)
                den = jnp.sum(pe, axis=-1, keepdims=True) + jnp.exp(sink - m)
                outs.append(_dot(pe.astype(BF16), vband) / den)
            if kvh == 0:
                pair = jnp.where(low_half, outs[0], pltpu.roll(outs[1], HEAD_DIM, axis=1))
            else:
                pair = jnp.where(low_half, pltpu.roll(outs[0], HEAD_DIM, axis=1), outs[1])
            ybuf[r0:r0 + ATTN_BLOCK, POOL_WIDTH + p * LANES:POOL_WIDTH + (p + 1) * LANES] = pair.astype(BF16)
    kbuf[0:ATTN_BLOCK, :] = kbuf[TS:TS + ATTN_BLOCK, :]
    vbuf[0:ATTN_BLOCK, :] = vbuf[TS:TS + ATTN_BLOCK, :]

    h = x + _dot(ybuf[...], wout_ref[...])
    h_ref[...] = h
    h2 = _rms(h, gffn_ref[...])
    for c in range(ROW_CHUNKS):
        h2_ref[pl.ds(c, TS, stride=ROW_CHUNKS), :] = _pack_bf16_pair(
            h2[:, c * LANES:(c + 1) * LANES], h2[:, HALF_MODEL + c * LANES:HALF_MODEL + (c + 1) * LANES])

    logits = _dot_nt(wrt_ref[...], h2, precision=lax.Precision.HIGHEST)
    scores = _sigmoid(logits)
    biased = scores + rbias_ref[...]
    erow = lax.broadcasted_iota(jnp.int32, (N_EXPERTS, TS), 0)
    grow = lax.broadcasted_iota(jnp.int32, (EXPERTS_PER_GROUP, TS), 0)
    group_scores = []
    for g in range(N_EXPERT_GROUPS):
        blk = biased[g * EXPERTS_PER_GROUP:(g + 1) * EXPERTS_PER_GROUP, :]
        m1 = jnp.max(blk, axis=0, keepdims=True)
        i1 = jnp.min(jnp.where(blk == m1, grow, EXPERTS_PER_GROUP), axis=0, keepdims=True)
        m2 = jnp.max(jnp.where(grow == i1, NEG_INF, blk), axis=0, keepdims=True)
        group_scores.append(m1 + m2)
    cur = jnp.concatenate(group_scores, axis=0)
    gsel = jnp.zeros((N_EXPERT_GROUPS, TS), jnp.bool_)
    for _ in range(TOPK_GROUPS):
        m = jnp.max(cur, axis=0, keepdims=True)
        i = jnp.min(jnp.where(cur == m, grow, N_EXPERT_GROUPS), axis=0, keepdims=True)
        hit = grow == i
        gsel = jnp.logical_or(gsel, hit)
        cur = jnp.where(hit, NEG_INF, cur)
    gmask = jnp.concatenate(
        [jnp.broadcast_to(gsel[g:g + 1, :], (EXPERTS_PER_GROUP, TS)) for g in range(N_EXPERT_GROUPS)], axis=0)
    masked = jnp.where(gmask, biased, NEG_INF)
    sel = jnp.zeros((N_EXPERTS, TS), jnp.bool_)
    idxs, ws = [], []
    for _ in range(TOP_K):
        m = jnp.max(masked, axis=0, keepdims=True)
        i = jnp.min(jnp.where(masked == m, erow, N_EXPERTS), axis=0, keepdims=True)
        hit = erow == i
        idxs.append(i)
        ws.append(jnp.sum(jnp.where(hit, scores, 0.0), axis=0, keepdims=True))
        sel = jnp.logical_or(sel, hit)
        masked = jnp.where(hit, NEG_INF, masked)
    wtot = ws[0]
    for wk in ws[1:]:
        wtot = wtot + wk
    idx_ref[...] = jnp.concatenate(idxs, axis=0)
    topw_ref[...] = jnp.concatenate([wk / wtot * ROUTED_SCALE for wk in ws], axis=0)

    self32 = sel.astype(F32)
    ri = lax.broadcasted_iota(jnp.int32, (TS, TS), 0)
    ci = lax.broadcasted_iota(jnp.int32, (TS, TS), 1)
    before = (ri < ci).astype(BF16)
    running = _dot(self32.astype(BF16), before) + cnt_acc[:, 0:1]
    rank_ref[...] = jnp.concatenate(
        [jnp.sum(jnp.where(erow == i, running, 0.0), axis=0, keepdims=True) for i in idxs], axis=0).astype(jnp.int32)
    cnt_acc[...] = cnt_acc[...] + jnp.sum(self32, axis=1, keepdims=True)
    cnt_ref[...] = cnt_acc[...]


def _mix_call(x, g_mix, w_in, w_pool, pool_scale, sinks, rpe_table, bucket, w_out, g_ffn, w_rt, r_bias):
    batch, seq, _ = x.shape
    tokens = batch * seq
    TS = MIX_TOKENS
    ns = seq // TS
    const = lambda shape: pl.BlockSpec(shape, lambda b, s: (0,) * len(shape), pipeline_mode=pl.Buffered(1))
    smem = pl.BlockSpec(memory_space=pltpu.SMEM)
    tok3 = pl.BlockSpec((None, TS, D_MODEL), lambda b, s: (b, s, 0))
    lane_blk = pl.BlockSpec((TOP_K, TS), lambda b, s: (0, b * ns + s))
    return pl.pallas_call(
        _mix_kernel,
        grid=(batch, ns),
        in_specs=[
            tok3,
            const((1, D_MODEL)),
            const((D_MODEL, IN_WIDTH)),
            const((POOL_GROUPS, POOL_GROUP_DIM, POOL_GROUP_DIM)),
            const((1, POOL_WIDTH)),
            smem,
            smem,
            const((ATTN_BLOCK, 2 * ATTN_BLOCK)),
            const((MIX_WIDTH, D_MODEL)),
            const((1, D_MODEL)),
            const((N_EXPERTS, D_MODEL)),
            const((N_EXPERTS, 1)),
        ],
        out_specs=[
            tok3,
            pl.BlockSpec((TS * ROW_CHUNKS, LANES), lambda b, s: (b * ns + s, 0)),
            lane_blk,
            lane_blk,
            lane_blk,
            pl.BlockSpec((N_EXPERTS, LANES), lambda b, s: (0, 0)),
        ],
        out_shape=[
            jax.ShapeDtypeStruct((batch, seq, D_MODEL), F32),
            jax.ShapeDtypeStruct((tokens * ROW_CHUNKS, LANES), U32),
            jax.ShapeDtypeStruct((TOP_K, tokens), jnp.int32),
            jax.ShapeDtypeStruct((TOP_K, tokens), F32),
            jax.ShapeDtypeStruct((TOP_K, tokens), jnp.int32),
            jax.ShapeDtypeStruct((N_EXPERTS, LANES), F32),
        ],
        scratch_shapes=[
            pltpu.VMEM((POOL_HISTORY + TS, POOL_WIDTH), F32),
            pltpu.VMEM((ATTN_BLOCK + TS, KV_WIDTH), BF16),
            pltpu.VMEM((ATTN_BLOCK + TS, KV_WIDTH), BF16),
            pltpu.VMEM((N_HEADS * ATTN_BLOCK, 2 * ATTN_BLOCK), F32),
            pltpu.VMEM((TS, MIX_WIDTH), BF16),
            pltpu.VMEM((N_EXPERTS, LANES), F32),
        ],
        compiler_params=pltpu.CompilerParams(
            dimension_semantics=("arbitrary", "arbitrary"), vmem_limit_bytes=VMEM_LIMIT_BYTES),
        name="mix_router",
    )(x, g_mix, w_in, w_pool, pool_scale, sinks, rpe_table, bucket, w_out, g_ffn, w_rt, r_bias)


def _dispatch_kernel(dest_ref, poff_ref, pend_ref, h2_ref, xs_hbm, zbuf, sem, zsem):
    step = pl.program_id(0)

    def zero_copy(e):
        start = pl.multiple_of((pend_ref[e] - ROW_TILE) * ROW_CHUNKS, ROW_CHUNKS)
        return pltpu.make_async_copy(zbuf, xs_hbm.at[pl.ds(start, ROW_TILE * ROW_CHUNKS)], zsem)

    @pl.when(step == 0)
    def _zero_tails():
        zbuf[...] = jnp.zeros_like(zbuf)

        def start(e, c):
            @pl.when(pend_ref[e] > poff_ref[e])
            def _():
                zero_copy(e).start()
            return c

        def wait(e, c):
            @pl.when(pend_ref[e] > poff_ref[e])
            def _():
                zero_copy(e).wait()
            return c

        lax.fori_loop(0, N_EXPERTS, start, 0)
        lax.fori_loop(0, N_EXPERTS, wait, 0)

    def row_copy(t, k):
        src = pl.multiple_of(t * ROW_CHUNKS, ROW_CHUNKS)
        dest = pl.multiple_of(dest_ref[k, t], ROW_CHUNKS)
        return pltpu.make_async_copy(h2_ref.at[pl.ds(src, ROW_CHUNKS)], xs_hbm.at[pl.ds(dest, ROW_CHUNKS)], sem)

    def start_tok(t, c):
        for k in range(TOP_K):
            row_copy(t, k).start(priority=k % 2)
        return c

    def wait_tok(t, c):
        for k in range(TOP_K):
            row_copy(t, k).wait()
        return c

    lax.fori_loop(0, DISPATCH_TOKENS, start_tok, 0, unroll=4)
    lax.fori_loop(0, DISPATCH_TOKENS, wait_tok, 0)


def _dispatch_call(dest3, poff, pend, h2, n_rows):
    n_steps = dest3.shape[0]
    smem_blk = pl.BlockSpec((None, TOP_K, DISPATCH_TOKENS), lambda i: (i, 0, 0), memory_space=pltpu.SMEM)
    smem = pl.BlockSpec(memory_space=pltpu.SMEM)
    return pl.pallas_call(
        _dispatch_kernel,
        grid=(n_steps,),
        in_specs=[smem_blk, smem, smem,
                  pl.BlockSpec((DISPATCH_TOKENS * ROW_CHUNKS, LANES), lambda i: (i, 0))],
        out_specs=pl.BlockSpec(memory_space=pl.ANY),
        out_shape=jax.ShapeDtypeStruct((n_rows * ROW_CHUNKS, LANES), U32),
        scratch_shapes=[
            pltpu.VMEM((ROW_TILE * ROW_CHUNKS, LANES), U32),
            pltpu.SemaphoreType.DMA,
            pltpu.SemaphoreType.DMA,
        ],
        compiler_params=pltpu.CompilerParams(dimension_semantics=("arbitrary",)),
        name="dispatch_rows",
    )(dest3, poff, pend, h2)


def _expert_kernel(te_ref, tb_ref, tv_ref, tfirst_ref, tnext_ref, tslot_ref,
                   xs_ref, wg_hbm, wu_hbm, wd_hbm, ys_ref,
                   xb_ref, wg_stage, wu_stage, wd_stage, wg_bf, wu_bf, wd_bf, sems):
    i = pl.program_id(0)

    def weight_copies(e, slot):
        return (pltpu.make_async_copy(wg_hbm.at[e], wg_stage.at[slot], sems.at[slot, 0]),
                pltpu.make_async_copy(wu_hbm.at[e], wu_stage.at[slot], sems.at[slot, 1]),
                pltpu.make_async_copy(wd_hbm.at[e], wd_stage.at[slot], sems.at[slot, 2]))

    @pl.when((i == 0) & (tv_ref[0] == 1))
    def _prologue():
        for cp in weight_copies(te_ref[0], tslot_ref[0]):
            cp.start()

    @pl.when((tv_ref[i] == 1) & (tfirst_ref[i] == 1))
    def _new_expert():
        slot = tslot_ref[i]
        for cp in weight_copies(te_ref[i], slot):
            cp.wait()

        @pl.when(tnext_ref[i] >= 0)
        def _():
            for cp in weight_copies(tnext_ref[i], 1 - slot):
                cp.start()

        def cast_in(j, c):
            r = pl.multiple_of(j * WEIGHT_CAST_ROWS, WEIGHT_CAST_ROWS)
            wg_bf[pl.ds(r, WEIGHT_CAST_ROWS), :] = wg_stage[slot, pl.ds(r, WEIGHT_CAST_ROWS), :].astype(BF16)
            wu_bf[pl.ds(r, WEIGHT_CAST_ROWS), :] = wu_stage[slot, pl.ds(r, WEIGHT_CAST_ROWS), :].astype(BF16)
            return c

        def cast_out(j, c):
            r = pl.multiple_of(j * (WEIGHT_CAST_ROWS // 4), WEIGHT_CAST_ROWS // 4)
            wd_bf[pl.ds(r, WEIGHT_CAST_ROWS // 4), :] = wd_stage[slot, pl.ds(r, WEIGHT_CAST_ROWS // 4), :].astype(BF16)
            return c

        lax.fori_loop(0, D_MODEL // WEIGHT_CAST_ROWS, cast_in, 0)
        lax.fori_loop(0, EXPERT_FF // (WEIGHT_CAST_ROWS // 4), cast_out, 0)

    @pl.when(tv_ref[i] == 1)
    def _tile():
        for c in range(ROW_CHUNKS):
            lo, hi = _unpack_bf16_pair(xs_ref[pl.ds(c, ROW_TILE, stride=ROW_CHUNKS), :])
            xb_ref[:, c * LANES:(c + 1) * LANES] = lo.astype(BF16)
            xb_ref[:, HALF_MODEL + c * LANES:HALF_MODEL + (c + 1) * LANES] = hi.astype(BF16)
        xb = xb_ref[...]
        g = _dot(xb, wg_bf[...])
        u = _dot(xb, wu_bf[...])
        hb = (g * _sigmoid(g) * u).astype(BF16)
        y = _dot(hb, wd_bf[...])
        for c in range(ROW_CHUNKS):
            ys_ref[pl.ds(c, ROW_TILE, stride=ROW_CHUNKS), :] = _pack_bf16_pair(
                y[:, c * LANES:(c + 1) * LANES], y[:, HALF_MODEL + c * LANES:HALF_MODEL + (c + 1) * LANES])


def _expert_call(tile_meta, xs, w_gate, w_up, w_down):
    n_tiles = tile_meta[0].shape[0]
    n_rows = xs.shape[0] // ROW_CHUNKS
    row_blk = pl.BlockSpec((ROW_TILE * ROW_CHUNKS, LANES), lambda i, te, tb, *_: (tb[i], 0))
    any_spec = pl.BlockSpec(memory_space=pl.ANY)
    return pl.pallas_call(
        _expert_kernel,
        grid_spec=pltpu.PrefetchScalarGridSpec(
            num_scalar_prefetch=len(tile_meta),
            grid=(n_tiles,),
            in_specs=[row_blk, any_spec, any_spec, any_spec],
            out_specs=row_blk,
            scratch_shapes=[
                pltpu.VMEM((ROW_TILE, D_MODEL), BF16),
                pltpu.VMEM((2, D_MODEL, EXPERT_FF), F32),
                pltpu.VMEM((2, D_MODEL, EXPERT_FF), F32),
                pltpu.VMEM((2, EXPERT_FF, D_MODEL), F32),
                pltpu.VMEM((D_MODEL, EXPERT_FF), BF16),
                pltpu.VMEM((D_MODEL, EXPERT_FF), BF16),
                pltpu.VMEM((EXPERT_FF, D_MODEL), BF16),
                pltpu.SemaphoreType.DMA((2, 3)),
            ],
        ),
        out_shape=jax.ShapeDtypeStruct((n_rows * ROW_CHUNKS, LANES), U32),
        compiler_params=pltpu.CompilerParams(
            dimension_semantics=("arbitrary",), vmem_limit_bytes=VMEM_LIMIT_BYTES),
        name="routed_experts",
    )(*tile_meta, xs, w_gate, w_up, w_down)


def _final_kernel(src_ref, src_next_ref,
                  h_ref, gffn_ref, p_ref, wt_ref, ys_hbm,
                  wsg_ref, wsu_ref, wsd_ref, wpp_ref, gple_ref, wpg_ref, gfin_ref,
                  out_ref, rows, h3_ref, sems):
    TB = FINAL_TOKENS
    i = pl.program_id(0)
    n = pl.num_programs(0)

    def row_copy(src_r, slot, t, k):
        src = pl.multiple_of(src_r[k, t], ROW_CHUNKS)
        dst = pl.multiple_of(slot * GATHER_SLOT_ROWS + (k * TB + t) * ROW_CHUNKS, ROW_CHUNKS)
        return pltpu.make_async_copy(ys_hbm.at[pl.ds(src, ROW_CHUNKS)], rows.at[pl.ds(dst, ROW_CHUNKS)], sems.at[slot])

    def start_block(src_r, slot):
        def body(t, c):
            for k in range(TOP_K):
                row_copy(src_r, slot, t, k).start(priority=k % 2)
            return c
        lax.fori_loop(0, TB, body, 0, unroll=4)

    def wait_block(src_r, slot):
        def body(t, c):
            for k in range(TOP_K):
                row_copy(src_r, slot, t, k).wait()
            return c
        lax.fori_loop(0, TB, body, 0)

    slot = lax.rem(i, 2)

    @pl.when(i == 0)
    def _first():
        start_block(src_ref, slot)

    @pl.when(i + 1 < n)
    def _prefetch():
        start_block(src_next_ref, 1 - slot)

    h = h_ref[...]
    h2b = _rms(h, gffn_ref[...]).astype(BF16)
    g = _dot(h2b, wsg_ref[...])
    u = _dot(h2b, wsu_ref[...])
    shared = _dot((g * _sigmoid(g) * u).astype(BF16), wsd_ref[...])
    ple = _rms(_dot(p_ref[...].astype(BF16), wpp_ref[...]), gple_ref[...])

    wait_block(src_ref, slot)
    base = slot * GATHER_SLOT_ROWS
    wt = wt_ref[...]
    for c in range(ROW_CHUNKS):
        routed_lo = routed_hi = None
        for k in range(TOP_K):
            lo, hi = _unpack_bf16_pair(rows[pl.ds(base + k * TB * ROW_CHUNKS + c, TB, stride=ROW_CHUNKS), :])
            wk = wt[:, k:k + 1]
            routed_lo = lo * wk if k == 0 else routed_lo + lo * wk
            routed_hi = hi * wk if k == 0 else routed_hi + hi * wk
        for routed, c0 in ((routed_lo, c * LANES), (routed_hi, HALF_MODEL + c * LANES)):
            cs = slice(c0, c0 + LANES)
            h3_ref[:, cs] = h[:, cs] + (routed + shared[:, cs])

    h3 = h3_ref[...]
    gate = _sigmoid(_dot(h3.astype(BF16), wpg_ref[...]))
    h4 = h3 + gate * ple
    out_ref[...] = _rms(h4, gfin_ref[...])


def _final_call(src3, h, g_ffn, p, wt, ys, ws_gate, ws_up, ws_down, w_pp, g_ple, w_pg, g_final):
    TB = FINAL_TOKENS
    n_steps = src3.shape[0]
    tokens = h.shape[0]
    const = lambda shape: pl.BlockSpec(shape, lambda i: (0,) * len(shape), pipeline_mode=pl.Buffered(1))
    smem_blk = pl.BlockSpec((None, TOP_K, TB), lambda i: (i, 0, 0), memory_space=pltpu.SMEM)
    smem_next = pl.BlockSpec((None, TOP_K, TB), lambda i: (jnp.minimum(i + 1, n_steps - 1), 0, 0),
                             memory_space=pltpu.SMEM)
    tok = pl.BlockSpec((TB, D_MODEL), lambda i: (i, 0))
    return pl.pallas_call(
        _final_kernel,
        grid=(n_steps,),
        in_specs=[
            smem_blk, smem_next,
            tok,
            const((1, D_MODEL)),
            pl.BlockSpec((TB, PLE_DIM), lambda i: (i, 0)),
            pl.BlockSpec((TB, TOP_K), lambda i: (i, 0)),
            pl.BlockSpec(memory_space=pl.ANY),
            const((D_MODEL, SHARED_FF)),
            const((D_MODEL, SHARED_FF)),
            const((SHARED_FF, D_MODEL)),
            const((PLE_DIM, D_MODEL)),
            const((1, D_MODEL)),
            const((D_MODEL, D_MODEL)),
            const((1, D_MODEL)),
        ],
        out_specs=tok,
        out_shape=jax.ShapeDtypeStruct((tokens, D_MODEL), F32),
        scratch_shapes=[
            pltpu.VMEM((2 * GATHER_SLOT_ROWS, LANES), U32),
            pltpu.VMEM((TB, D_MODEL), F32),
            pltpu.SemaphoreType.DMA((2,)),
        ],
        compiler_params=pltpu.CompilerParams(
            dimension_semantics=("arbitrary",), vmem_limit_bytes=VMEM_LIMIT_BYTES),
        name="combine_final",
    )(src3, src3, h, g_ffn, p, wt, ys, ws_gate, ws_up, ws_down, w_pp, g_ple, w_pg, g_final)


def _per_step(a, tokens_per_step):
    tokens = a.shape[1]
    return a.reshape(TOP_K, tokens // tokens_per_step, tokens_per_step).transpose(1, 0, 2)


def _tile_metadata(counts, n_tiles_max):
    i32 = jnp.int32
    padded = (counts + ROW_TILE - 1) // ROW_TILE * ROW_TILE
    pend = jnp.cumsum(padded).astype(i32)
    poff = pend - padded
    n_tiles = pend[-1] // ROW_TILE
    tile = jnp.arange(n_tiles_max, dtype=i32)
    tile_v = (tile < n_tiles).astype(i32)
    tile_b = jnp.minimum(tile, jnp.maximum(n_tiles - 1, 0))
    tile_e = jnp.minimum(jnp.sum(pend[None, :] <= (tile_b * ROW_TILE)[:, None], axis=1), N_EXPERTS - 1).astype(i32)
    onehot = tile_e[:, None] == jnp.arange(N_EXPERTS, dtype=i32)[None, :]

    def lookup(table):
        return jnp.sum(jnp.where(onehot, table[None, :], 0), axis=1).astype(i32)

    tile_first = (tile_v * (tile_b * ROW_TILE == lookup(poff))).astype(i32)
    nonempty = counts > 0
    order = jnp.cumsum(nonempty.astype(i32)) - 1
    ids = jnp.where(nonempty, jnp.arange(N_EXPERTS, dtype=i32), N_EXPERTS)
    later = jnp.flip(lax.cummin(jnp.flip(ids)))
    next_e = jnp.concatenate([later[1:], jnp.full((1,), N_EXPERTS, i32)])
    next_e = jnp.where(next_e == N_EXPERTS, -1, next_e)
    tile_next = lookup(next_e)
    tile_slot = lookup(order % 2)
    return poff, pend, (tile_e, tile_b, tile_v, tile_first, tile_next, tile_slot)


def kernel(x, p, g_mix, w_in, w_pool, pool_scale, attn_sinks, rpe_table, w_out, g_ffn, w_router, router_bias,
           w_gate, w_up, w_down, ws_gate, ws_up, ws_down, w_ple_proj, g_ple, w_ple_gate, g_final):
    batch, seq, _ = x.shape
    tokens = batch * seq
    assert seq % MIX_TOKENS == 0 and tokens % DISPATCH_TOKENS == 0 and tokens % FINAL_TOKENS == 0
    n_rows = tokens * TOP_K + N_EXPERTS * ROW_TILE
    bucket = jnp.asarray(_rpe_bucket_map())
    h, h2, idx, topw, rank, cnt = _mix_call(
        x, g_mix[0][None, :], w_in[0].astype(BF16), w_pool[0].astype(BF16), pool_scale[0][None, :],
        attn_sinks[0], rpe_table, bucket, w_out[0].astype(BF16), g_ffn[0][None, :],
        w_router[0].T, router_bias[0][:, None])

    poff, pend, tile_meta = _tile_metadata(cnt[:, 0].astype(jnp.int32), n_rows // ROW_TILE)
    experts = jnp.arange(N_EXPERTS, dtype=jnp.int32)[:, None, None]
    dest = (jnp.sum(jnp.where(idx[None] == experts, poff[:, None, None], 0), axis=0) + rank) * ROW_CHUNKS
    xs = _dispatch_call(_per_step(dest, DISPATCH_TOKENS), poff, pend, h2, n_rows)
    ys = _expert_call(tile_meta, xs, w_gate[0], w_up[0], w_down[0])
    out = _final_call(
        _per_step(dest, FINAL_TOKENS),
        h.reshape(tokens, D_MODEL), g_ffn[0][None, :], p[0].reshape(tokens, PLE_DIM), topw.T, ys,
        ws_gate[0].astype(BF16), ws_up[0].astype(BF16), ws_down[0].astype(BF16),
        w_ple_proj[0].astype(BF16), g_ple[0][None, :], w_ple_gate[0].astype(BF16), g_final[None, :])
    return out.reshape(batch, seq, D_MODEL)
```

```python
import math

import numpy as np
import jax
import jax.numpy as jnp
from jax import lax
from jax.experimental import pallas as pl
from jax.experimental.pallas import tpu as pltpu

F32 = jnp.float32
BF16 = jnp.bfloat16
NEG_INF = float("-inf")

D_MODEL = 2048
PLE_DIM = 256
POOL_WIDTH = 1024
POOL_GROUPS = 4
POOL_GROUP_DIM = POOL_WIDTH // POOL_GROUPS
POOL_WINDOWS = (2, 4, 8, 16)
N_HEADS = 16
N_KV_HEADS = 2
HEAD_DIM = 64
HEADS_PER_KV = N_HEADS // N_KV_HEADS
ATTN_WIDTH = N_HEADS * HEAD_DIM
KV_WIDTH = N_KV_HEADS * HEAD_DIM
MIX_WIDTH = POOL_WIDTH + ATTN_WIDTH
IN_WIDTH = POOL_WIDTH + ATTN_WIDTH + 2 * KV_WIDTH
Q_OFF = POOL_WIDTH
K_OFF = POOL_WIDTH + ATTN_WIDTH
V_OFF = K_OFF + KV_WIDTH
ATTN_BLOCK = 128
WINDOW = 128
RPE_BUCKETS = 32
RPE_MAX_EXACT = RPE_BUCKETS // 2
RPE_MAX_DISTANCE = 128
N_EXPERTS = 64
TOP_K = 8
N_EXPERT_GROUPS = 8
EXPERTS_PER_GROUP = N_EXPERTS // N_EXPERT_GROUPS
TOPK_GROUPS = 4
EXPERT_FF = 512
SHARED_FF = 512
ROUTED_SCALE = 2.5
EPS = 1e-6

LANES = 128
SUBLANES = 8
VMEM_LIMIT_BYTES = 58 * 1024 * 1024

MIX_TOKENS = 256
POOL_HISTORY = 16
ROW_TILE = 256
META_TOKENS = 512
FINAL_TOKENS = 256
WEIGHT_CAST_ROWS = 256
HALF_MODEL = D_MODEL // 2
ROW_CHUNKS = HALF_MODEL // LANES
TOKEN_PITCH = (TOP_K + 1) * ROW_CHUNKS
U32 = jnp.uint32


def _rms(x, g):
    return x * lax.rsqrt(jnp.mean(x * x, axis=-1, keepdims=True) + EPS) * g


def _sigmoid(x):
    return 1.0 / (1.0 + jnp.exp(-x))


def _pack_bf16_pair(lo, hi):
    ulo = lax.bitcast_convert_type(lo.astype(BF16).astype(F32), U32)
    uhi = lax.bitcast_convert_type(hi.astype(BF16).astype(F32), U32)
    return (ulo >> 16) | uhi


def _unpack_bf16_pair(w):
    lo = lax.bitcast_convert_type(w << 16, F32)
    hi = lax.bitcast_convert_type(w & jnp.uint32(0xFFFF0000), F32)
    return lo, hi


def _dot(a, b):
    return jnp.dot(a, b, preferred_element_type=F32)


def _dot_nt(a, b, precision=None):
    return lax.dot_general(a, b, (((1,), (1,)), ((), ())), preferred_element_type=F32, precision=precision)


def _rpe_bucket_map():
    i = np.arange(ATTN_BLOCK)[:, None]
    j = np.arange(2 * ATTN_BLOCK)[None, :]
    dist = i + ATTN_BLOCK - j
    n = np.maximum(dist, 0)
    nf = np.maximum(n, 1).astype(np.float32)
    large = RPE_MAX_EXACT + (np.log(nf / np.float32(RPE_MAX_EXACT)) / np.float32(math.log(RPE_MAX_DISTANCE / RPE_MAX_EXACT))
                             * np.float32(RPE_BUCKETS - RPE_MAX_EXACT)).astype(np.int32)
    large = np.minimum(large, RPE_BUCKETS - 1)
    bucket = np.where(n < RPE_MAX_EXACT, n, large)
    valid = (dist >= 0) & (dist < WINDOW)
    return np.where(valid, bucket, -1).astype(np.int32)


def _mix_kernel(x_ref, gmix_ref, win_ref, wpool_ref, pscale_ref, sinks_ref, rpe_ref, bucket_ref,
                wout_ref, gffn_ref, wrt_ref, rbias_ref,
                h_ref, h2_ref, idx_ref, topw_ref, rank_ref, cnt_ref,
                ubuf, kbuf, vbuf, bias_buf, ybuf, cnt_acc):
    TS = MIX_TOKENS
    b = pl.program_id(0)
    s = pl.program_id(1)

    @pl.when((b == 0) & (s == 0))
    def _init():
        bucket = bucket_ref[...]
        for h in range(N_HEADS):
            acc = jnp.full((ATTN_BLOCK, 2 * ATTN_BLOCK), NEG_INF, F32)
            for bk in range(RPE_BUCKETS):
                acc = jnp.where(bucket == bk, rpe_ref[bk, h], acc)
            bias_buf[h * ATTN_BLOCK:(h + 1) * ATTN_BLOCK, :] = acc
        cnt_acc[...] = jnp.zeros_like(cnt_acc)

    @pl.when(s == 0)
    def _reset_history():
        ubuf[0:POOL_HISTORY, :] = jnp.zeros((POOL_HISTORY, POOL_WIDTH), F32)
        kbuf[0:ATTN_BLOCK, :] = jnp.zeros((ATTN_BLOCK, KV_WIDTH), BF16)
        vbuf[0:ATTN_BLOCK, :] = jnp.zeros((ATTN_BLOCK, KV_WIDTH), BF16)

    x = x_ref[...]
    a = _rms(x, gmix_ref[...]).astype(BF16)
    z = _dot(a, win_ref[...])

    ubuf[POOL_HISTORY:POOL_HISTORY + TS, :] = z[:, 0:POOL_WIDTH]
    pos = s * TS + lax.broadcasted_iota(jnp.int32, (TS, 1), 0)
    for gi, w in enumerate(POOL_WINDOWS):
        c0, c1 = gi * POOL_GROUP_DIM, (gi + 1) * POOL_GROUP_DIM
        e = ubuf[:, c0:c1]
        shift = 1
        while shift < w:
            e = e + pltpu.roll(e, shift, axis=0)
            shift *= 2
        wsum = e[POOL_HISTORY:, :]
        count = jnp.minimum(pos + 1, w).astype(F32)
        pooled = wsum / count - z[:, c0:c1]
        yp = _dot(pooled.astype(BF16), wpool_ref[gi]) * pscale_ref[:, c0:c1]
        ybuf[:, c0:c1] = yp.astype(BF16)
    ubuf[0:POOL_HISTORY, :] = ubuf[TS:TS + POOL_HISTORY, :]

    kbuf[ATTN_BLOCK:ATTN_BLOCK + TS, :] = z[:, K_OFF:K_OFF + KV_WIDTH].astype(BF16)
    vbuf[ATTN_BLOCK:ATTN_BLOCK + TS, :] = z[:, V_OFF:V_OFF + KV_WIDTH].astype(BF16)
    lane = lax.broadcasted_iota(jnp.int32, (ATTN_BLOCK, LANES), 1)
    low_half = lane < HEAD_DIM
    high_half = jnp.logical_not(low_half)
    col = lax.broadcasted_iota(jnp.int32, (1, 2 * ATTN_BLOCK), 1)
    first_mask = jnp.where((col < ATTN_BLOCK) & (s == 0), NEG_INF, 0.0).astype(F32)
    for sb in range(TS // ATTN_BLOCK):
        r0 = sb * ATTN_BLOCK
        kband = kbuf[r0:r0 + 2 * ATTN_BLOCK, :]
        vband = vbuf[r0:r0 + 2 * ATTN_BLOCK, :]
        for p in range(N_HEADS // 2):
            qp = z[r0:r0 + ATTN_BLOCK, Q_OFF + p * LANES:Q_OFF + (p + 1) * LANES] * (HEAD_DIM ** -0.5)
            qr = pltpu.roll(qp, HEAD_DIM, axis=1)
            kvh = (2 * p) // HEADS_PER_KV
            kv_lanes = low_half if kvh == 0 else high_half
            outs = []
            for par in range(2):
                h = 2 * p + par
                qh = jnp.where(kv_lanes, qp if par == kvh else qr, 0.0).astype(BF16)
                lg = _dot_nt(qh, kband) + bias_buf[h * ATTN_BLOCK:(h + 1) * ATTN_BLOCK, :]
                if sb == 0:
                    lg = lg + first_mask
                sink = sinks_ref[h]
                m = jnp.maximum(jnp.max(lg, axis=-1, keepdims=True), sink)
                pe = jnp.exp(lg - m)
                den = jnp.sum(pe, axis=-1, keepdims=True) + jnp.exp(sink - m)
                outs.append(_dot(pe.astype(BF16), vband) / den)
            if kvh == 0:
                pair = jnp.where(low_half, outs[0], pltpu.roll(outs[1], HEAD_DIM, axis=1))
            else:
                pair = jnp.where(low_half, pltpu.roll(outs[0], HEAD_DIM, axis=1), outs[1])
            ybuf[r0:r0 + ATTN_BLOCK, POOL_WIDTH + p * LANES:POOL_WIDTH + (p + 1) * LANES] = pair.astype(BF16)
    kbuf[0:ATTN_BLOCK, :] = kbuf[TS:TS + ATTN_BLOCK, :]
    vbuf[0:ATTN_BLOCK, :] = vbuf[TS:TS + ATTN_BLOCK, :]

    h = x + _dot(ybuf[...], wout_ref[...])
    h_ref[...] = h
    h2 = _rms(h, gffn_ref[...])
    for c in range(ROW_CHUNKS):
        h2_ref[pl.ds(c, TS, stride=ROW_CHUNKS), :] = _pack_bf16_pair(
            h2[:, c * LANES:(c + 1) * LANES], h2[:, HALF_MODEL + c * LANES:HALF_MODEL + (c + 1) * LANES])

    logits = _dot_nt(wrt_ref[...], h2, precision=lax.Precision.HIGHEST)
    scores = _sigmoid(logits)
    biased = scores + rbias_ref[...]
    erow = lax.broadcasted_iota(jnp.int32, (N_EXPERTS, TS), 0)
    grow = lax.broadcasted_iota(jnp.int32, (EXPERTS_PER_GROUP, TS), 0)
    group_scores = []
    for g in range(N_EXPERT_GROUPS):
        blk = biased[g * EXPERTS_PER_GROUP:(g + 1) * EXPERTS_PER_GROUP, :]
        m1 = jnp.max(blk, axis=0, keepdims=True)
        i1 = jnp.min(jnp.where(blk == m1, grow, EXPERTS_PER_GROUP), axis=0, keepdims=True)
        m2 = jnp.max(jnp.where(grow == i1, NEG_INF, blk), axis=0, keepdims=True)
        group_scores.append(m1 + m2)
    cur = jnp.concatenate(group_scores, axis=0)
    gsel = jnp.zeros((N_EXPERT_GROUPS, TS), jnp.bool_)
    for _ in range(TOPK_GROUPS):
        m = jnp.max(cur, axis=0, keepdims=True)
        i = jnp.min(jnp.where(cur == m, grow, N_EXPERT_GROUPS), axis=0, keepdims=True)
        hit = grow == i
        gsel = jnp.logical_or(gsel, hit)
        cur = jnp.where(hit, NEG_INF, cur)
    gmask = jnp.concatenate(
        [jnp.broadcast_to(gsel[g:g + 1, :], (EXPERTS_PER_GROUP, TS)) for g in range(N_EXPERT_GROUPS)], axis=0)
    masked = jnp.where(gmask, biased, NEG_INF)
    sel = jnp.zeros((N_EXPERTS, TS), jnp.bool_)
    idxs, ws = [], []
    for _ in range(TOP_K):
        m = jnp.max(masked, axis=0, keepdims=True)
        i = jnp.min(jnp.where(masked == m, erow, N_EXPERTS), axis=0, keepdims=True)
        hit = erow == i
        idxs.append(i)
        ws.append(jnp.sum(jnp.where(hit, scores, 0.0), axis=0, keepdims=True))
        sel = jnp.logical_or(sel, hit)
        masked = jnp.where(hit, NEG_INF, masked)
    wtot = ws[0]
    for wk in ws[1:]:
        wtot = wtot + wk
    idx_ref[...] = jnp.concatenate(idxs, axis=0)
    topw_ref[...] = jnp.concatenate([wk / wtot * ROUTED_SCALE for wk in ws], axis=0)

    self32 = sel.astype(F32)
    ri = lax.broadcasted_iota(jnp.int32, (TS, TS), 0)
    ci = lax.broadcasted_iota(jnp.int32, (TS, TS), 1)
    before = (ri < ci).astype(BF16)
    running = _dot(self32.astype(BF16), before) + cnt_acc[:, 0:1]
    rank_ref[...] = jnp.concatenate(
        [jnp.sum(jnp.where(erow == i, running, 0.0), axis=0, keepdims=True) for i in idxs], axis=0).astype(jnp.int32)
    cnt_acc[...] = cnt_acc[...] + jnp.sum(self32, axis=1, keepdims=True)
    cnt_ref[...] = cnt_acc[...]


def _mix_call(x, g_mix, w_in, w_pool, pool_scale, sinks, rpe_table, bucket, w_out, g_ffn, w_rt, r_bias):
    batch, seq, _ = x.shape
    tokens = batch * seq
    TS = MIX_TOKENS
    ns = seq // TS
    const = lambda shape: pl.BlockSpec(shape, lambda b, s: (0,) * len(shape), pipeline_mode=pl.Buffered(1))
    smem = pl.BlockSpec(memory_space=pltpu.SMEM)
    tok3 = pl.BlockSpec((None, TS, D_MODEL), lambda b, s: (b, s, 0))
    lane_blk = pl.BlockSpec((TOP_K, TS), lambda b, s: (0, b * ns + s))
    return pl.pallas_call(
        _mix_kernel,
        grid=(batch, ns),
        in_specs=[
            tok3,
            const((1, D_MODEL)),
            const((D_MODEL, IN_WIDTH)),
            const((POOL_GROUPS, POOL_GROUP_DIM, POOL_GROUP_DIM)),
            const((1, POOL_WIDTH)),
            smem,
            smem,
            const((ATTN_BLOCK, 2 * ATTN_BLOCK)),
            const((MIX_WIDTH, D_MODEL)),
            const((1, D_MODEL)),
            const((N_EXPERTS, D_MODEL)),
            const((N_EXPERTS, 1)),
        ],
        out_specs=[
            tok3,
            pl.BlockSpec((TS * ROW_CHUNKS, LANES), lambda b, s: (b * ns + s, 0)),
            lane_blk,
            lane_blk,
            lane_blk,
            pl.BlockSpec((N_EXPERTS, LANES), lambda b, s: (0, 0)),
        ],
        out_shape=[
            jax.ShapeDtypeStruct((batch, seq, D_MODEL), F32),
            jax.ShapeDtypeStruct((tokens * ROW_CHUNKS, LANES), U32),
            jax.ShapeDtypeStruct((TOP_K, tokens), jnp.int32),
            jax.ShapeDtypeStruct((TOP_K, tokens), F32),
            jax.ShapeDtypeStruct((TOP_K, tokens), jnp.int32),
            jax.ShapeDtypeStruct((N_EXPERTS, LANES), F32),
        ],
        scratch_shapes=[
            pltpu.VMEM((POOL_HISTORY + TS, POOL_WIDTH), F32),
            pltpu.VMEM((ATTN_BLOCK + TS, KV_WIDTH), BF16),
            pltpu.VMEM((ATTN_BLOCK + TS, KV_WIDTH), BF16),
            pltpu.VMEM((N_HEADS * ATTN_BLOCK, 2 * ATTN_BLOCK), F32),
            pltpu.VMEM((TS, MIX_WIDTH), BF16),
            pltpu.VMEM((N_EXPERTS, LANES), F32),
        ],
        compiler_params=pltpu.CompilerParams(
            dimension_semantics=("arbitrary", "arbitrary"), vmem_limit_bytes=VMEM_LIMIT_BYTES),
        name="mix_router",
    )(x, g_mix, w_in, w_pool, pool_scale, sinks, rpe_table, bucket, w_out, g_ffn, w_rt, r_bias)


def _meta_kernel(dest_ref, meta_ref):
    base = pl.program_id(0) * META_TOKENS

    def body(t, c):
        for k in range(TOP_K):
            meta_ref[dest_ref[k, t]] = (base + t) * TOP_K + k
        return c

    lax.fori_loop(0, META_TOKENS, body, 0, unroll=4)


def _meta_call(dest3, n_rows):
    return pl.pallas_call(
        _meta_kernel,
        grid=(dest3.shape[0],),
        in_specs=[pl.BlockSpec((None, TOP_K, META_TOKENS), lambda i: (i, 0, 0), memory_space=pltpu.SMEM)],
        out_specs=pl.BlockSpec(memory_space=pltpu.SMEM),
        out_shape=jax.ShapeDtypeStruct((n_rows,), jnp.int32),
        compiler_params=pltpu.CompilerParams(dimension_semantics=("arbitrary",)),
        name="row_metadata",
    )(dest3)


def _expert_kernel(te_ref, tv_ref, tfirst_ref, tnext_ref, tslot_ref, tnv_ref, tlast_ref,
                   meta_ref, meta_next_ref, h2_hbm, wg_hbm, wu_hbm, wd_hbm, ys_hbm,
                   xg, yo, xb_ref, wg_stage, wu_stage, wd_stage, wg_bf, wu_bf, wd_bf, wsems, gsems, ssems,
                   *, dump_row):
    i = pl.program_id(0)
    n = pl.num_programs(0)
    slot = lax.rem(i, 2)
    tile_rows = ROW_TILE * ROW_CHUNKS

    def gather_copy(m_ref, n_valid, s, r):
        m = m_ref[0, r]
        src = jnp.where(r < n_valid, (m >> 3) << 3, 0)
        dst = s * tile_rows + r * ROW_CHUNKS
        return pltpu.make_async_copy(h2_hbm.at[pl.ds(pl.multiple_of(src, ROW_CHUNKS), ROW_CHUNKS)],
                                     xg.at[pl.ds(pl.multiple_of(dst, ROW_CHUNKS), ROW_CHUNKS)], gsems.at[s])

    def scatter_copy(m_ref, n_valid, s, r):
        m = m_ref[0, r]
        dst = jnp.where(r < n_valid, (m + (m >> 3)) * ROW_CHUNKS, dump_row + r * ROW_CHUNKS)
        src = s * tile_rows + r * ROW_CHUNKS
        return pltpu.make_async_copy(yo.at[pl.ds(pl.multiple_of(src, ROW_CHUNKS), ROW_CHUNKS)],
                                     ys_hbm.at[pl.ds(pl.multiple_of(dst, ROW_CHUNKS), ROW_CHUNKS)], ssems.at[s])

    def start_rows(make, m_ref, n_valid, s):
        def body(r, c):
            make(m_ref, n_valid, s, r).start(priority=0)
            make(m_ref, n_valid, s, r + ROW_TILE // 2).start(priority=1)
            return c
        lax.fori_loop(0, ROW_TILE // 2, body, 0, unroll=8)

    def wait_rows(src_ref, dst_ref, sem):
        for _ in range(ROW_TILE):
            pltpu.make_async_copy(src_ref.at[pl.ds(0, ROW_CHUNKS)], dst_ref.at[pl.ds(0, ROW_CHUNKS)], sem).wait()

    def weight_copies(e, s):
        return (pltpu.make_async_copy(wg_hbm.at[e], wg_stage.at[s], wsems.at[s, 0]),
                pltpu.make_async_copy(wu_hbm.at[e], wu_stage.at[s], wsems.at[s, 1]),
                pltpu.make_async_copy(wd_hbm.at[e], wd_stage.at[s], wsems.at[s, 2]))

    valid = tv_ref[i] == 1

    @pl.when((i == 0) & valid)
    def _prologue():
        for cp in weight_copies(te_ref[0], tslot_ref[0]):
            cp.start()
        start_rows(gather_copy, meta_ref, tnv_ref[0], 0)

    nxt = jnp.minimum(i + 1, n - 1)

    @pl.when(valid & (i + 1 < n) & (tv_ref[nxt] == 1))
    def _prefetch_rows():
        start_rows(gather_copy, meta_next_ref, tnv_ref[nxt], 1 - slot)

    @pl.when(valid & (tfirst_ref[i] == 1))
    def _new_expert():
        ws = tslot_ref[i]
        for cp in weight_copies(te_ref[i], ws):
            cp.wait()

        @pl.when(tnext_ref[i] >= 0)
        def _():
            for cp in weight_copies(tnext_ref[i], 1 - ws):
                cp.start()

        def cast_in(j, c):
            r = pl.multiple_of(j * WEIGHT_CAST_ROWS, WEIGHT_CAST_ROWS)
            wg_bf[pl.ds(r, WEIGHT_CAST_ROWS), :] = wg_stage[ws, pl.ds(r, WEIGHT_CAST_ROWS), :].astype(BF16)
            wu_bf[pl.ds(r, WEIGHT_CAST_ROWS), :] = wu_stage[ws, pl.ds(r, WEIGHT_CAST_ROWS), :].astype(BF16)
            return c

        def cast_out(j, c):
            r = pl.multiple_of(j * (WEIGHT_CAST_ROWS // 4), WEIGHT_CAST_ROWS // 4)
            wd_bf[pl.ds(r, WEIGHT_CAST_ROWS // 4), :] = wd_stage[ws, pl.ds(r, WEIGHT_CAST_ROWS // 4), :].astype(BF16)
            return c

        lax.fori_loop(0, D_MODEL // WEIGHT_CAST_ROWS, cast_in, 0)
        lax.fori_loop(0, EXPERT_FF // (WEIGHT_CAST_ROWS // 4), cast_out, 0)

    @pl.when(valid)
    def _tile():
        base = pl.multiple_of(slot * tile_rows, tile_rows)
        wait_rows(h2_hbm, xg, gsems.at[slot])
        for c in range(ROW_CHUNKS):
            lo, hi = _unpack_bf16_pair(xg[pl.ds(base + c, ROW_TILE, stride=ROW_CHUNKS), :])
            xb_ref[:, c * LANES:(c + 1) * LANES] = lo.astype(BF16)
            xb_ref[:, HALF_MODEL + c * LANES:HALF_MODEL + (c + 1) * LANES] = hi.astype(BF16)
        xb = xb_ref[...]
        g = _dot(xb, wg_bf[...])
        u = _dot(xb, wu_bf[...])
        hb = (g * _sigmoid(g) * u).astype(BF16)
        y = _dot(hb, wd_bf[...])

        @pl.when(i >= 2)
        def _():
            wait_rows(yo, ys_hbm, ssems.at[slot])

        for c in range(ROW_CHUNKS):
            yo[pl.ds(base + c, ROW_TILE, stride=ROW_CHUNKS), :] = _pack_bf16_pair(
                y[:, c * LANES:(c + 1) * LANES], y[:, HALF_MODEL + c * LANES:HALF_MODEL + (c + 1) * LANES])
        start_rows(scatter_copy, meta_ref, tnv_ref[i], slot)

        @pl.when(tlast_ref[i] == 1)
        def _drain():
            wait_rows(yo, ys_hbm, ssems.at[slot])

            @pl.when(i >= 1)
            def _():
                wait_rows(yo, ys_hbm, ssems.at[1 - slot])


def _expert_call(tile_meta, tile_b, meta, h2, w_gate, w_up, w_down, tokens):
    n_tiles = tile_b.shape[0]
    meta3 = meta.reshape(n_tiles, 1, ROW_TILE)
    dump_row = tokens * TOKEN_PITCH
    meta_blk = pl.BlockSpec((None, 1, ROW_TILE), lambda i, te, tv, tf, tn, ts, tnv, tl, tb: (tb[i], 0, 0),
                            memory_space=pltpu.SMEM)
    meta_next_blk = pl.BlockSpec(
        (None, 1, ROW_TILE), lambda i, te, tv, tf, tn, ts, tnv, tl, tb: (tb[jnp.minimum(i + 1, n_tiles - 1)], 0, 0),
        memory_space=pltpu.SMEM)
    any_spec = pl.BlockSpec(memory_space=pl.ANY)

    def body(te, tv, tf, tn, ts, tnv, tl, tb, *refs):
        _expert_kernel(te, tv, tf, tn, ts, tnv, tl, *refs, dump_row=dump_row)

    return pl.pallas_call(
        body,
        grid_spec=pltpu.PrefetchScalarGridSpec(
            num_scalar_prefetch=len(tile_meta) + 1,
            grid=(n_tiles,),
            in_specs=[meta_blk, meta_next_blk, any_spec, any_spec, any_spec, any_spec],
            out_specs=any_spec,
            scratch_shapes=[
                pltpu.VMEM((2 * ROW_TILE * ROW_CHUNKS, LANES), U32),
                pltpu.VMEM((2 * ROW_TILE * ROW_CHUNKS, LANES), U32),
                pltpu.VMEM((ROW_TILE, D_MODEL), BF16),
                pltpu.VMEM((2, D_MODEL, EXPERT_FF), F32),
                pltpu.VMEM((2, D_MODEL, EXPERT_FF), F32),
                pltpu.VMEM((2, EXPERT_FF, D_MODEL), F32),
                pltpu.VMEM((D_MODEL, EXPERT_FF), BF16),
                pltpu.VMEM((D_MODEL, EXPERT_FF), BF16),
                pltpu.VMEM((EXPERT_FF, D_MODEL), BF16),
                pltpu.SemaphoreType.DMA((2, 3)),
                pltpu.SemaphoreType.DMA((2,)),
                pltpu.SemaphoreType.DMA((2,)),
            ],
        ),
        out_shape=jax.ShapeDtypeStruct((dump_row + ROW_TILE * ROW_CHUNKS, LANES), U32),
        compiler_params=pltpu.CompilerParams(
            dimension_semantics=("arbitrary",), vmem_limit_bytes=VMEM_LIMIT_BYTES),
        name="routed_experts",
    )(*tile_meta, tile_b, meta3, meta3, h2, w_gate, w_up, w_down)


def _final_kernel(h_ref, gffn_ref, p_ref, wt_ref, ys_ref,
                  wsg_ref, wsu_ref, wsd_ref, wpp_ref, gple_ref, wpg_ref, gfin_ref,
                  out_ref, h3_ref):
    TB = FINAL_TOKENS
    h = h_ref[...]
    h2b = _rms(h, gffn_ref[...]).astype(BF16)
    g = _dot(h2b, wsg_ref[...])
    u = _dot(h2b, wsu_ref[...])
    shared = _dot((g * _sigmoid(g) * u).astype(BF16), wsd_ref[...])
    ple = _rms(_dot(p_ref[...].astype(BF16), wpp_ref[...]), gple_ref[...])

    wt = wt_ref[...]
    for c in range(ROW_CHUNKS):
        routed_lo = routed_hi = None
        for k in range(TOP_K):
            lo, hi = _unpack_bf16_pair(ys_ref[pl.ds(k * ROW_CHUNKS + c, TB, stride=TOKEN_PITCH), :])
            wk = wt[:, k:k + 1]
            routed_lo = lo * wk if k == 0 else routed_lo + lo * wk
            routed_hi = hi * wk if k == 0 else routed_hi + hi * wk
        for routed, c0 in ((routed_lo, c * LANES), (routed_hi, HALF_MODEL + c * LANES)):
            cs = slice(c0, c0 + LANES)
            h3_ref[:, cs] = h[:, cs] + (routed + shared[:, cs])

    h3 = h3_ref[...]
    gate = _sigmoid(_dot(h3.astype(BF16), wpg_ref[...]))
    h4 = h3 + gate * ple
    out_ref[...] = _rms(h4, gfin_ref[...])


def _final_call(h, g_ffn, p, wt, ys, ws_gate, ws_up, ws_down, w_pp, g_ple, w_pg, g_final):
    TB = FINAL_TOKENS
    tokens = h.shape[0]
    const = lambda shape: pl.BlockSpec(shape, lambda i: (0,) * len(shape), pipeline_mode=pl.Buffered(1))
    tok = pl.BlockSpec((TB, D_MODEL), lambda i: (i, 0))
    return pl.pallas_call(
        _final_kernel,
        grid=(tokens // TB,),
        in_specs=[
            tok,
            const((1, D_MODEL)),
            pl.BlockSpec((TB, PLE_DIM), lambda i: (i, 0)),
            pl.BlockSpec((TB, TOP_K), lambda i: (i, 0)),
            pl.BlockSpec((TB * TOKEN_PITCH, LANES), lambda i: (i, 0)),
            const((D_MODEL, SHARED_FF)),
            const((D_MODEL, SHARED_FF)),
            const((SHARED_FF, D_MODEL)),
            const((PLE_DIM, D_MODEL)),
            const((1, D_MODEL)),
            const((D_MODEL, D_MODEL)),
            const((1, D_MODEL)),
        ],
        out_specs=tok,
        out_shape=jax.ShapeDtypeStruct((tokens, D_MODEL), F32),
        scratch_shapes=[pltpu.VMEM((TB, D_MODEL), F32)],
        compiler_params=pltpu.CompilerParams(
            dimension_semantics=("arbitrary",), vmem_limit_bytes=VMEM_LIMIT_BYTES),
        name="combine_final",
    )(h, g_ffn, p, wt, ys, ws_gate, ws_up, ws_down, w_pp, g_ple, w_pg, g_final)


def _per_step(a, tokens_per_step):
    tokens = a.shape[1]
    return a.reshape(TOP_K, tokens // tokens_per_step, tokens_per_step).transpose(1, 0, 2)


def _tile_metadata(counts, n_tiles_max):
    i32 = jnp.int32
    padded = (counts + ROW_TILE - 1) // ROW_TILE * ROW_TILE
    pend = jnp.cumsum(padded).astype(i32)
    poff = pend - padded
    n_tiles = pend[-1] // ROW_TILE
    tile = jnp.arange(n_tiles_max, dtype=i32)
    tile_v = (tile < n_tiles).astype(i32)
    tile_b = jnp.minimum(tile, jnp.maximum(n_tiles - 1, 0))
    tile_e = jnp.minimum(jnp.sum(pend[None, :] <= (tile_b * ROW_TILE)[:, None], axis=1), N_EXPERTS - 1).astype(i32)
    onehot = tile_e[:, None] == jnp.arange(N_EXPERTS, dtype=i32)[None, :]

    def lookup(table):
        return jnp.sum(jnp.where(onehot, table[None, :], 0), axis=1).astype(i32)

    tile_first = (tile_v * (tile_b * ROW_TILE == lookup(poff))).astype(i32)
    nonempty = counts > 0
    order = jnp.cumsum(nonempty.astype(i32)) - 1
    ids = jnp.where(nonempty, jnp.arange(N_EXPERTS, dtype=i32), N_EXPERTS)
    later = jnp.flip(lax.cummin(jnp.flip(ids)))
    next_e = jnp.concatenate([later[1:], jnp.full((1,), N_EXPERTS, i32)])
    next_e = jnp.where(next_e == N_EXPERTS, -1, next_e)
    tile_next = lookup(next_e)
    tile_slot = lookup(order % 2)
    tile_nvalid = jnp.clip(lookup(poff + counts) - tile_b * ROW_TILE, 0, ROW_TILE).astype(i32)
    tile_last = (tile_v * (tile == n_tiles - 1)).astype(i32)
    return poff, tile_b, (tile_e, tile_v, tile_first, tile_next, tile_slot, tile_nvalid, tile_last)


def kernel(x, p, g_mix, w_in, w_pool, pool_scale, attn_sinks, rpe_table, w_out, g_ffn, w_router, router_bias,
           w_gate, w_up, w_down, ws_gate, ws_up, ws_down, w_ple_proj, g_ple, w_ple_gate, g_final):
    batch, seq, _ = x.shape
    tokens = batch * seq
    assert seq % MIX_TOKENS == 0 and tokens % META_TOKENS == 0 and tokens % FINAL_TOKENS == 0
    n_rows = tokens * TOP_K + N_EXPERTS * ROW_TILE
    bucket = jnp.asarray(_rpe_bucket_map())
    h, h2, idx, topw, rank, cnt = _mix_call(
        x, g_mix[0][None, :], w_in[0].astype(BF16), w_pool[0].astype(BF16), pool_scale[0][None, :],
        attn_sinks[0], rpe_table, bucket, w_out[0].astype(BF16), g_ffn[0][None, :],
        w_router[0].T, router_bias[0][:, None])

    poff, tile_b, tile_meta = _tile_metadata(cnt[:, 0].astype(jnp.int32), n_rows // ROW_TILE)
    experts = jnp.arange(N_EXPERTS, dtype=jnp.int32)[:, None, None]
    dest = jnp.sum(jnp.where(idx[None] == experts, poff[:, None, None], 0), axis=0) + rank
    meta = _meta_call(_per_step(dest, META_TOKENS), n_rows)
    ys = _expert_call(tile_meta, tile_b, meta, h2, w_gate[0], w_up[0], w_down[0], tokens)
    out = _final_call(
        h.reshape(tokens, D_MODEL), g_ffn[0][None, :], p[0].reshape(tokens, PLE_DIM), topw.T, ys,
        ws_gate[0].astype(BF16), ws_up[0].astype(BF16), ws_down[0].astype(BF16),
        w_ple_proj[0].astype(BF16), g_ple[0][None, :], w_ple_gate[0].astype(BF16), g_final[None, :])
    return out.reshape(batch, seq, D_MODEL)
```

```python
import math

import numpy as np
import jax
import jax.numpy as jnp
from jax import lax
from jax.experimental import pallas as pl
from jax.experimental.pallas import tpu as pltpu

F32 = jnp.float32
BF16 = jnp.bfloat16
NEG_INF = float("-inf")

D_MODEL = 2048
PLE_DIM = 256
POOL_WIDTH = 1024
POOL_GROUPS = 4
POOL_GROUP_DIM = POOL_WIDTH // POOL_GROUPS
POOL_WINDOWS = (2, 4, 8, 16)
N_HEADS = 16
N_KV_HEADS = 2
HEAD_DIM = 64
HEADS_PER_KV = N_HEADS // N_KV_HEADS
ATTN_WIDTH = N_HEADS * HEAD_DIM
KV_WIDTH = N_KV_HEADS * HEAD_DIM
MIX_WIDTH = POOL_WIDTH + ATTN_WIDTH
IN_WIDTH = POOL_WIDTH + ATTN_WIDTH + 2 * KV_WIDTH
Q_OFF = POOL_WIDTH
K_OFF = POOL_WIDTH + ATTN_WIDTH
V_OFF = K_OFF + KV_WIDTH
ATTN_BLOCK = 128
WINDOW = 128
RPE_BUCKETS = 32
RPE_MAX_EXACT = RPE_BUCKETS // 2
RPE_MAX_DISTANCE = 128
N_EXPERTS = 64
TOP_K = 8
N_EXPERT_GROUPS = 8
EXPERTS_PER_GROUP = N_EXPERTS // N_EXPERT_GROUPS
TOPK_GROUPS = 4
EXPERT_FF = 512
SHARED_FF = 512
ROUTED_SCALE = 2.5
EPS = 1e-6

LANES = 128
SUBLANES = 8
VMEM_LIMIT_BYTES = 58 * 1024 * 1024

MIX_TOKENS = 256
POOL_HISTORY = 16
ROW_TILE = 256
META_TOKENS = 512
FINAL_TOKENS = 256
WEIGHT_CAST_ROWS = 256
HALF_MODEL = D_MODEL // 2
ROW_CHUNKS = HALF_MODEL // LANES
TOKEN_PITCH = (TOP_K + 1) * ROW_CHUNKS
U32 = jnp.uint32


def _rms(x, g):
    return x * lax.rsqrt(jnp.mean(x * x, axis=-1, keepdims=True) + EPS) * g


def _sigmoid(x):
    return 1.0 / (1.0 + jnp.exp(-x))


def _pack_bf16_pair(lo, hi):
    ulo = lax.bitcast_convert_type(lo.astype(BF16).astype(F32), U32)
    uhi = lax.bitcast_convert_type(hi.astype(BF16).astype(F32), U32)
    return (ulo >> 16) | uhi


def _unpack_bf16_pair(w):
    lo = lax.bitcast_convert_type(w << 16, F32)
    hi = lax.bitcast_convert_type(w & jnp.uint32(0xFFFF0000), F32)
    return lo, hi


def _dot(a, b):
    return jnp.dot(a, b, preferred_element_type=F32)


def _dot_nt(a, b, precision=None):
    return lax.dot_general(a, b, (((1,), (1,)), ((), ())), preferred_element_type=F32, precision=precision)


def _rpe_bucket_map():
    i = np.arange(ATTN_BLOCK)[:, None]
    j = np.arange(2 * ATTN_BLOCK)[None, :]
    dist = i + ATTN_BLOCK - j
    n = np.maximum(dist, 0)
    nf = np.maximum(n, 1).astype(np.float32)
    large = RPE_MAX_EXACT + (np.log(nf / np.float32(RPE_MAX_EXACT)) / np.float32(math.log(RPE_MAX_DISTANCE / RPE_MAX_EXACT))
                             * np.float32(RPE_BUCKETS - RPE_MAX_EXACT)).astype(np.int32)
    large = np.minimum(large, RPE_BUCKETS - 1)
    bucket = np.where(n < RPE_MAX_EXACT, n, large)
    valid = (dist >= 0) & (dist < WINDOW)
    return np.where(valid, bucket, -1).astype(np.int32)


def _mix_kernel(x_ref, gmix_ref, win_ref, wpool_ref, pscale_ref, sinks_ref, rpe_ref, bucket_ref,
                wout_ref, gffn_ref, wrt_ref, rbias_ref,
                h_ref, h2_ref, idx_ref, topw_ref, rank_ref, cnt_ref,
                ubuf, kbuf, vbuf, bias_buf, ybuf, cnt_acc):
    TS = MIX_TOKENS
    b = pl.program_id(0)
    s = pl.program_id(1)

    @pl.when((b == 0) & (s == 0))
    def _init():
        bucket = bucket_ref[...]
        for h in range(N_HEADS):
            acc = jnp.full((ATTN_BLOCK, 2 * ATTN_BLOCK), NEG_INF, F32)
            for bk in range(RPE_BUCKETS):
                acc = jnp.where(bucket == bk, rpe_ref[bk, h], acc)
            bias_buf[h * ATTN_BLOCK:(h + 1) * ATTN_BLOCK, :] = acc
        cnt_acc[...] = jnp.zeros_like(cnt_acc)

    @pl.when(s == 0)
    def _reset_history():
        ubuf[0:POOL_HISTORY, :] = jnp.zeros((POOL_HISTORY, POOL_WIDTH), F32)
        kbuf[0:ATTN_BLOCK, :] = jnp.zeros((ATTN_BLOCK, KV_WIDTH), BF16)
        vbuf[0:ATTN_BLOCK, :] = jnp.zeros((ATTN_BLOCK, KV_WIDTH), BF16)

    x = x_ref[...]
    a = _rms(x, gmix_ref[...]).astype(BF16)
    z = _dot(a, win_ref[...])

    ubuf[POOL_HISTORY:POOL_HISTORY + TS, :] = z[:, 0:POOL_WIDTH]
    pos = s * TS + lax.broadcasted_iota(jnp.int32, (TS, 1), 0)
    for gi, w in enumerate(POOL_WINDOWS):
        c0, c1 = gi * POOL_GROUP_DIM, (gi + 1) * POOL_GROUP_DIM
        e = ubuf[:, c0:c1]
        shift = 1
        while shift < w:
            e = e + pltpu.roll(e, shift, axis=0)
            shift *= 2
        wsum = e[POOL_HISTORY:, :]
        count = jnp.minimum(pos + 1, w).astype(F32)
        pooled = wsum / count - z[:, c0:c1]
        yp = _dot(pooled.astype(BF16), wpool_ref[gi]) * pscale_ref[:, c0:c1]
        ybuf[:, c0:c1] = yp.astype(BF16)
    ubuf[0:POOL_HISTORY, :] = ubuf[TS:TS + POOL_HISTORY, :]

    kbuf[ATTN_BLOCK:ATTN_BLOCK + TS, :] = z[:, K_OFF:K_OFF + KV_WIDTH].astype(BF16)
    vbuf[ATTN_BLOCK:ATTN_BLOCK + TS, :] = z[:, V_OFF:V_OFF + KV_WIDTH].astype(BF16)
    lane = lax.broadcasted_iota(jnp.int32, (ATTN_BLOCK, LANES), 1)
    low_half = lane < HEAD_DIM
    high_half = jnp.logical_not(low_half)
    col = lax.broadcasted_iota(jnp.int32, (1, 2 * ATTN_BLOCK), 1)
    first_mask = jnp.where((col < ATTN_BLOCK) & (s == 0), NEG_INF, 0.0).astype(F32)
    for sb in range(TS // ATTN_BLOCK):
        r0 = sb * ATTN_BLOCK
        kband = kbuf[r0:r0 + 2 * ATTN_BLOCK, :]
        vband = vbuf[r0:r0 + 2 * ATTN_BLOCK, :]
        for p in range(N_HEADS // 2):
            qp = z[r0:r0 + ATTN_BLOCK, Q_OFF + p * LANES:Q_OFF + (p + 1) * LANES] * (HEAD_DIM ** -0.5)
            qr = pltpu.roll(qp, HEAD_DIM, axis=1)
            kvh = (2 * p) // HEADS_PER_KV
            kv_lanes = low_half if kvh == 0 else high_half
            outs = []
            for par in range(2):
                h = 2 * p + par
                qh = jnp.where(kv_lanes, qp if par == kvh else qr, 0.0).astype(BF16)
                lg = _dot_nt(qh, kband) + bias_buf[h * ATTN_BLOCK:(h + 1) * ATTN_BLOCK, :]
                if sb == 0:
                    lg = lg + first_mask
                sink = sinks_ref[h]
                m = jnp.maximum(jnp.max(lg, axis=-1, keepdims=True), sink)
                pe = jnp.exp(lg - m)
                den = jnp.sum(pe, axis=-1, keepdims=True) + jnp.exp(sink - m)
                outs.append(_dot(pe.astype(BF16), vband) / den)
            if kvh == 0:
                pair = jnp.where(low_half, outs[0], pltpu.roll(outs[1], HEAD_DIM, axis=1))
            else:
                pair = jnp.where(low_half, pltpu.roll(outs[0], HEAD_DIM, axis=1), outs[1])
            ybuf[r0:r0 + ATTN_BLOCK, POOL_WIDTH + p * LANES:POOL_WIDTH + (p + 1) * LANES] = pair.astype(BF16)
    kbuf[0:ATTN_BLOCK, :] = kbuf[TS:TS + ATTN_BLOCK, :]
    vbuf[0:ATTN_BLOCK, :] = vbuf[TS:TS + ATTN_BLOCK, :]

    h = x + _dot(ybuf[...], wout_ref[...])
    h_ref[...] = h
    h2 = _rms(h, gffn_ref[...])
    for c in range(ROW_CHUNKS):
        h2_ref[pl.ds(c, TS, stride=ROW_CHUNKS), :] = _pack_bf16_pair(
            h2[:, c * LANES:(c + 1) * LANES], h2[:, HALF_MODEL + c * LANES:HALF_MODEL + (c + 1) * LANES])

    logits = _dot_nt(wrt_ref[...], h2, precision=lax.Precision.HIGHEST)
    scores = _sigmoid(logits)
    biased = scores + rbias_ref[...]
    erow = lax.broadcasted_iota(jnp.int32, (N_EXPERTS, TS), 0)
    grow = lax.broadcasted_iota(jnp.int32, (EXPERTS_PER_GROUP, TS), 0)
    group_scores = []
    for g in range(N_EXPERT_GROUPS):
        blk = biased[g * EXPERTS_PER_GROUP:(g + 1) * EXPERTS_PER_GROUP, :]
        m1 = jnp.max(blk, axis=0, keepdims=True)
        i1 = jnp.min(jnp.where(blk == m1, grow, EXPERTS_PER_GROUP), axis=0, keepdims=True)
        m2 = jnp.max(jnp.where(grow == i1, NEG_INF, blk), axis=0, keepdims=True)
        group_scores.append(m1 + m2)
    cur = jnp.concatenate(group_scores, axis=0)
    gsel = jnp.zeros((N_EXPERT_GROUPS, TS), jnp.bool_)
    for _ in range(TOPK_GROUPS):
        m = jnp.max(cur, axis=0, keepdims=True)
        i = jnp.min(jnp.where(cur == m, grow, N_EXPERT_GROUPS), axis=0, keepdims=True)
        hit = grow == i
        gsel = jnp.logical_or(gsel, hit)
        cur = jnp.where(hit, NEG_INF, cur)
    gmask = jnp.concatenate(
        [jnp.broadcast_to(gsel[g:g + 1, :], (EXPERTS_PER_GROUP, TS)) for g in range(N_EXPERT_GROUPS)], axis=0)
    masked = jnp.where(gmask, biased, NEG_INF)
    sel = jnp.zeros((N_EXPERTS, TS), jnp.bool_)
    idxs, ws = [], []
    for _ in range(TOP_K):
        m = jnp.max(masked, axis=0, keepdims=True)
        i = jnp.min(jnp.where(masked == m, erow, N_EXPERTS), axis=0, keepdims=True)
        hit = erow == i
        idxs.append(i)
        ws.append(jnp.sum(jnp.where(hit, scores, 0.0), axis=0, keepdims=True))
        sel = jnp.logical_or(sel, hit)
        masked = jnp.where(hit, NEG_INF, masked)
    wtot = ws[0]
    for wk in ws[1:]:
        wtot = wtot + wk
    idx_ref[...] = jnp.concatenate(idxs, axis=0)
    topw_ref[...] = jnp.concatenate([wk / wtot * ROUTED_SCALE for wk in ws], axis=0)

    self32 = sel.astype(F32)
    ri = lax.broadcasted_iota(jnp.int32, (TS, TS), 0)
    ci = lax.broadcasted_iota(jnp.int32, (TS, TS), 1)
    before = (ri < ci).astype(BF16)
    running = _dot(self32.astype(BF16), before) + cnt_acc[:, 0:1]
    rank_ref[...] = jnp.concatenate(
        [jnp.sum(jnp.where(erow == i, running, 0.0), axis=0, keepdims=True) for i in idxs], axis=0).astype(jnp.int32)
    cnt_acc[...] = cnt_acc[...] + jnp.sum(self32, axis=1, keepdims=True)
    cnt_ref[...] = cnt_acc[...]


def _mix_call(x, g_mix, w_in, w_pool, pool_scale, sinks, rpe_table, bucket, w_out, g_ffn, w_rt, r_bias):
    batch, seq, _ = x.shape
    tokens = batch * seq
    TS = MIX_TOKENS
    ns = seq // TS
    const = lambda shape: pl.BlockSpec(shape, lambda b, s: (0,) * len(shape), pipeline_mode=pl.Buffered(1))
    smem = pl.BlockSpec(memory_space=pltpu.SMEM)
    tok3 = pl.BlockSpec((None, TS, D_MODEL), lambda b, s: (b, s, 0))
    lane_blk = pl.BlockSpec((TOP_K, TS), lambda b, s: (0, b * ns + s))
    return pl.pallas_call(
        _mix_kernel,
        grid=(batch, ns),
        in_specs=[
            tok3,
            const((1, D_MODEL)),
            const((D_MODEL, IN_WIDTH)),
            const((POOL_GROUPS, POOL_GROUP_DIM, POOL_GROUP_DIM)),
            const((1, POOL_WIDTH)),
            smem,
            smem,
            const((ATTN_BLOCK, 2 * ATTN_BLOCK)),
            const((MIX_WIDTH, D_MODEL)),
            const((1, D_MODEL)),
            const((N_EXPERTS, D_MODEL)),
            const((N_EXPERTS, 1)),
        ],
        out_specs=[
            tok3,
            pl.BlockSpec((TS * ROW_CHUNKS, LANES), lambda b, s: (b * ns + s, 0)),
            lane_blk,
            lane_blk,
            lane_blk,
            pl.BlockSpec((N_EXPERTS, LANES), lambda b, s: (0, 0)),
        ],
        out_shape=[
            jax.ShapeDtypeStruct((batch, seq, D_MODEL), F32),
            jax.ShapeDtypeStruct((tokens * ROW_CHUNKS, LANES), U32),
            jax.ShapeDtypeStruct((TOP_K, tokens), jnp.int32),
            jax.ShapeDtypeStruct((TOP_K, tokens), F32),
            jax.ShapeDtypeStruct((TOP_K, tokens), jnp.int32),
            jax.ShapeDtypeStruct((N_EXPERTS, LANES), F32),
        ],
        scratch_shapes=[
            pltpu.VMEM((POOL_HISTORY + TS, POOL_WIDTH), F32),
            pltpu.VMEM((ATTN_BLOCK + TS, KV_WIDTH), BF16),
            pltpu.VMEM((ATTN_BLOCK + TS, KV_WIDTH), BF16),
            pltpu.VMEM((N_HEADS * ATTN_BLOCK, 2 * ATTN_BLOCK), F32),
            pltpu.VMEM((TS, MIX_WIDTH), BF16),
            pltpu.VMEM((N_EXPERTS, LANES), F32),
        ],
        compiler_params=pltpu.CompilerParams(
            dimension_semantics=("arbitrary", "arbitrary"), vmem_limit_bytes=VMEM_LIMIT_BYTES),
        name="mix_router",
    )(x, g_mix, w_in, w_pool, pool_scale, sinks, rpe_table, bucket, w_out, g_ffn, w_rt, r_bias)


def _meta_kernel(dest_ref, meta_ref):
    base = pl.program_id(0) * META_TOKENS

    def body(t, c):
        for k in range(TOP_K):
            meta_ref[dest_ref[k, t]] = (base + t) * TOP_K + k
        return c

    lax.fori_loop(0, META_TOKENS, body, 0, unroll=4)


def _meta_call(dest3, n_rows):
    return pl.pallas_call(
        _meta_kernel,
        grid=(dest3.shape[0],),
        in_specs=[pl.BlockSpec((None, TOP_K, META_TOKENS), lambda i: (i, 0, 0), memory_space=pltpu.SMEM)],
        out_specs=pl.BlockSpec(memory_space=pltpu.SMEM),
        out_shape=jax.ShapeDtypeStruct((n_rows,), jnp.int32),
        compiler_params=pltpu.CompilerParams(dimension_semantics=("arbitrary",)),
        name="row_metadata",
    )(dest3)


def _expert_kernel(te_ref, tv_ref, tfirst_ref, tnext_ref, tslot_ref, tnv_ref, tlast_ref,
                   meta_ref, meta_next_ref, meta_prev_ref, h2_hbm, wg_hbm, wu_hbm, wd_hbm, ys_hbm,
                   xg, yo, xb_ref, wg_stage, wu_stage, wd_stage, wg_bf, wu_bf, wd_bf, wsems, gsems, ssems,
                   *, dump_row):
    i = pl.program_id(0)
    n = pl.num_programs(0)
    slot = lax.rem(i, 2)
    tile_rows = ROW_TILE * ROW_CHUNKS

    def gather_copy(m_ref, n_valid, s, r):
        m = m_ref[0, r]
        src = jnp.where(r < n_valid, (m >> 3) << 3, 0)
        dst = s * tile_rows + r * ROW_CHUNKS
        return pltpu.make_async_copy(h2_hbm.at[pl.ds(pl.multiple_of(src, ROW_CHUNKS), ROW_CHUNKS)],
                                     xg.at[pl.ds(pl.multiple_of(dst, ROW_CHUNKS), ROW_CHUNKS)], gsems.at[s])

    def scatter_copy(m_ref, n_valid, s, r):
        m = m_ref[0, r]
        dst = jnp.where(r < n_valid, (m + (m >> 3)) * ROW_CHUNKS, dump_row + r * ROW_CHUNKS)
        src = s * tile_rows + r * ROW_CHUNKS
        return pltpu.make_async_copy(yo.at[pl.ds(pl.multiple_of(src, ROW_CHUNKS), ROW_CHUNKS)],
                                     ys_hbm.at[pl.ds(pl.multiple_of(dst, ROW_CHUNKS), ROW_CHUNKS)], ssems.at[s])

    def start_rows(make, m_ref, n_valid, s):
        def body(r, c):
            make(m_ref, n_valid, s, r).start(priority=0)
            make(m_ref, n_valid, s, r + ROW_TILE // 2).start(priority=1)
            return c
        lax.fori_loop(0, ROW_TILE // 2, body, 0, unroll=8)

    def start_rows_inline(make, m_ref, n_valid, s):
        for r in range(ROW_TILE):
            make(m_ref, n_valid, s, r).start(priority=r % 2)

    def wait_rows(src_ref, dst_ref, sem):
        for _ in range(ROW_TILE):
            pltpu.make_async_copy(src_ref.at[pl.ds(0, ROW_CHUNKS)], dst_ref.at[pl.ds(0, ROW_CHUNKS)], sem).wait()

    def weight_copies(e, s):
        return (pltpu.make_async_copy(wg_hbm.at[e], wg_stage.at[s], wsems.at[s, 0]),
                pltpu.make_async_copy(wu_hbm.at[e], wu_stage.at[s], wsems.at[s, 1]),
                pltpu.make_async_copy(wd_hbm.at[e], wd_stage.at[s], wsems.at[s, 2]))

    valid = tv_ref[i] == 1

    @pl.when((i == 0) & valid)
    def _prologue():
        for cp in weight_copies(te_ref[0], tslot_ref[0]):
            cp.start()
        start_rows(gather_copy, meta_ref, tnv_ref[0], 0)
        yo[...] = jnp.zeros_like(yo)
        start_rows(scatter_copy, meta_ref, 0, 0)

    nxt = jnp.minimum(i + 1, n - 1)
    n_valid_prev = jnp.where(i >= 1, tnv_ref[jnp.maximum(i - 1, 0)], 0)

    @pl.when(valid & (tfirst_ref[i] == 1))
    def _new_expert():
        ws = tslot_ref[i]
        for cp in weight_copies(te_ref[i], ws):
            cp.wait()

        @pl.when(tnext_ref[i] >= 0)
        def _():
            for cp in weight_copies(tnext_ref[i], 1 - ws):
                cp.start()

        def cast_in(j, c):
            r = pl.multiple_of(j * WEIGHT_CAST_ROWS, WEIGHT_CAST_ROWS)
            wg_bf[pl.ds(r, WEIGHT_CAST_ROWS), :] = wg_stage[ws, pl.ds(r, WEIGHT_CAST_ROWS), :].astype(BF16)
            wu_bf[pl.ds(r, WEIGHT_CAST_ROWS), :] = wu_stage[ws, pl.ds(r, WEIGHT_CAST_ROWS), :].astype(BF16)
            return c

        def cast_out(j, c):
            r = pl.multiple_of(j * (WEIGHT_CAST_ROWS // 4), WEIGHT_CAST_ROWS // 4)
            wd_bf[pl.ds(r, WEIGHT_CAST_ROWS // 4), :] = wd_stage[ws, pl.ds(r, WEIGHT_CAST_ROWS // 4), :].astype(BF16)
            return c

        lax.fori_loop(0, D_MODEL // WEIGHT_CAST_ROWS, cast_in, 0)
        lax.fori_loop(0, EXPERT_FF // (WEIGHT_CAST_ROWS // 4), cast_out, 0)

    @pl.when(valid)
    def _tile():
        base = pl.multiple_of(slot * tile_rows, tile_rows)
        wait_rows(h2_hbm, xg, gsems.at[slot])
        for c in range(ROW_CHUNKS):
            lo, hi = _unpack_bf16_pair(xg[pl.ds(base + c, ROW_TILE, stride=ROW_CHUNKS), :])
            xb_ref[:, c * LANES:(c + 1) * LANES] = lo.astype(BF16)
            xb_ref[:, HALF_MODEL + c * LANES:HALF_MODEL + (c + 1) * LANES] = hi.astype(BF16)
        start_rows_inline(gather_copy, meta_next_ref, tnv_ref[nxt], 1 - slot)
        start_rows_inline(scatter_copy, meta_prev_ref, n_valid_prev, 1 - slot)
        xb = xb_ref[...]
        g = _dot(xb, wg_bf[...])
        u = _dot(xb, wu_bf[...])
        hb = (g * _sigmoid(g) * u).astype(BF16)
        y = _dot(hb, wd_bf[...])
        wait_rows(yo, ys_hbm, ssems.at[slot])
        for c in range(ROW_CHUNKS):
            yo[pl.ds(base + c, ROW_TILE, stride=ROW_CHUNKS), :] = _pack_bf16_pair(
                y[:, c * LANES:(c + 1) * LANES], y[:, HALF_MODEL + c * LANES:HALF_MODEL + (c + 1) * LANES])

    @pl.when(valid & (tlast_ref[i] == 1))
    def _drain():
        start_rows(scatter_copy, meta_ref, tnv_ref[i], slot)
        wait_rows(h2_hbm, xg, gsems.at[1 - slot])
        wait_rows(yo, ys_hbm, ssems.at[1 - slot])
        wait_rows(yo, ys_hbm, ssems.at[slot])


def _expert_call(tile_meta, tile_b, meta, h2, w_gate, w_up, w_down, tokens):
    n_tiles = tile_b.shape[0]
    meta3 = meta.reshape(n_tiles, 1, ROW_TILE)
    dump_row = tokens * TOKEN_PITCH
    meta_blk = pl.BlockSpec((None, 1, ROW_TILE), lambda i, te, tv, tf, tn, ts, tnv, tl, tb: (tb[i], 0, 0),
                            memory_space=pltpu.SMEM)
    meta_next_blk = pl.BlockSpec(
        (None, 1, ROW_TILE), lambda i, te, tv, tf, tn, ts, tnv, tl, tb: (tb[jnp.minimum(i + 1, n_tiles - 1)], 0, 0),
        memory_space=pltpu.SMEM)
    meta_prev_blk = pl.BlockSpec(
        (None, 1, ROW_TILE), lambda i, te, tv, tf, tn, ts, tnv, tl, tb: (tb[jnp.maximum(i - 1, 0)], 0, 0),
        memory_space=pltpu.SMEM)
    any_spec = pl.BlockSpec(memory_space=pl.ANY)

    def body(te, tv, tf, tn, ts, tnv, tl, tb, *refs):
        _expert_kernel(te, tv, tf, tn, ts, tnv, tl, *refs, dump_row=dump_row)

    return pl.pallas_call(
        body,
        grid_spec=pltpu.PrefetchScalarGridSpec(
            num_scalar_prefetch=len(tile_meta) + 1,
            grid=(n_tiles,),
            in_specs=[meta_blk, meta_next_blk, meta_prev_blk, any_spec, any_spec, any_spec, any_spec],
            out_specs=any_spec,
            scratch_shapes=[
                pltpu.VMEM((2 * ROW_TILE * ROW_CHUNKS, LANES), U32),
                pltpu.VMEM((2 * ROW_TILE * ROW_CHUNKS, LANES), U32),
                pltpu.VMEM((ROW_TILE, D_MODEL), BF16),
                pltpu.VMEM((2, D_MODEL, EXPERT_FF), F32),
                pltpu.VMEM((2, D_MODEL, EXPERT_FF), F32),
                pltpu.VMEM((2, EXPERT_FF, D_MODEL), F32),
                pltpu.VMEM((D_MODEL, EXPERT_FF), BF16),
                pltpu.VMEM((D_MODEL, EXPERT_FF), BF16),
                pltpu.VMEM((EXPERT_FF, D_MODEL), BF16),
                pltpu.SemaphoreType.DMA((2, 3)),
                pltpu.SemaphoreType.DMA((2,)),
                pltpu.SemaphoreType.DMA((2,)),
            ],
        ),
        out_shape=jax.ShapeDtypeStruct((dump_row + ROW_TILE * ROW_CHUNKS, LANES), U32),
        compiler_params=pltpu.CompilerParams(
            dimension_semantics=("arbitrary",), vmem_limit_bytes=VMEM_LIMIT_BYTES),
        name="routed_experts",
    )(*tile_meta, tile_b, meta3, meta3, meta3, h2, w_gate, w_up, w_down)


def _final_kernel(h_ref, gffn_ref, p_ref, wt_ref, ys_ref,
                  wsg_ref, wsu_ref, wsd_ref, wpp_ref, gple_ref, wpg_ref, gfin_ref,
                  out_ref, h3_ref):
    TB = FINAL_TOKENS
    h = h_ref[...]
    h2b = _rms(h, gffn_ref[...]).astype(BF16)
    g = _dot(h2b, wsg_ref[...])
    u = _dot(h2b, wsu_ref[...])
    shared = _dot((g * _sigmoid(g) * u).astype(BF16), wsd_ref[...])
    ple = _rms(_dot(p_ref[...].astype(BF16), wpp_ref[...]), gple_ref[...])

    wt = wt_ref[...]
    for c in range(ROW_CHUNKS):
        routed_lo = routed_hi = None
        for k in range(TOP_K):
            lo, hi = _unpack_bf16_pair(ys_ref[pl.ds(k * ROW_CHUNKS + c, TB, stride=TOKEN_PITCH), :])
            wk = wt[:, k:k + 1]
            routed_lo = lo * wk if k == 0 else routed_lo + lo * wk
            routed_hi = hi * wk if k == 0 else routed_hi + hi * wk
        for routed, c0 in ((routed_lo, c * LANES), (routed_hi, HALF_MODEL + c * LANES)):
            cs = slice(c0, c0 + LANES)
            h3_ref[:, cs] = h[:, cs] + (routed + shared[:, cs])

    h3 = h3_ref[...]
    gate = _sigmoid(_dot(h3.astype(BF16), wpg_ref[...]))
    h4 = h3 + gate * ple
    out_ref[...] = _rms(h4, gfin_ref[...])


def _final_call(h, g_ffn, p, wt, ys, ws_gate, ws_up, ws_down, w_pp, g_ple, w_pg, g_final):
    TB = FINAL_TOKENS
    tokens = h.shape[0]
    const = lambda shape: pl.BlockSpec(shape, lambda i: (0,) * len(shape), pipeline_mode=pl.Buffered(1))
    tok = pl.BlockSpec((TB, D_MODEL), lambda i: (i, 0))
    return pl.pallas_call(
        _final_kernel,
        grid=(tokens // TB,),
        in_specs=[
            tok,
            const((1, D_MODEL)),
            pl.BlockSpec((TB, PLE_DIM), lambda i: (i, 0)),
            pl.BlockSpec((TB, TOP_K), lambda i: (i, 0)),
            pl.BlockSpec((TB * TOKEN_PITCH, LANES), lambda i: (i, 0)),
            const((D_MODEL, SHARED_FF)),
            const((D_MODEL, SHARED_FF)),
            const((SHARED_FF, D_MODEL)),
            const((PLE_DIM, D_MODEL)),
            const((1, D_MODEL)),
            const((D_MODEL, D_MODEL)),
            const((1, D_MODEL)),
        ],
        out_specs=tok,
        out_shape=jax.ShapeDtypeStruct((tokens, D_MODEL), F32),
        scratch_shapes=[pltpu.VMEM((TB, D_MODEL), F32)],
        compiler_params=pltpu.CompilerParams(
            dimension_semantics=("arbitrary",), vmem_limit_bytes=VMEM_LIMIT_BYTES),
        name="combine_final",
    )(h, g_ffn, p, wt, ys, ws_gate, ws_up, ws_down, w_pp, g_ple, w_pg, g_final)


def _per_step(a, tokens_per_step):
    tokens = a.shape[1]
    return a.reshape(TOP_K, tokens // tokens_per_step, tokens_per_step).transpose(1, 0, 2)


def _tile_metadata(counts, n_tiles_max):
    i32 = jnp.int32
    padded = (counts + ROW_TILE - 1) // ROW_TILE * ROW_TILE
    pend = jnp.cumsum(padded).astype(i32)
    poff = pend - padded
    n_tiles = pend[-1] // ROW_TILE
    tile = jnp.arange(n_tiles_max, dtype=i32)
    tile_v = (tile < n_tiles).astype(i32)
    tile_b = jnp.minimum(tile, jnp.maximum(n_tiles - 1, 0))
    tile_e = jnp.minimum(jnp.sum(pend[None, :] <= (tile_b * ROW_TILE)[:, None], axis=1), N_EXPERTS - 1).astype(i32)
    onehot = tile_e[:, None] == jnp.arange(N_EXPERTS, dtype=i32)[None, :]

    def lookup(table):
        return jnp.sum(jnp.where(onehot, table[None, :], 0), axis=1).astype(i32)

    tile_first = (tile_v * (tile_b * ROW_TILE == lookup(poff))).astype(i32)
    nonempty = counts > 0
    order = jnp.cumsum(nonempty.astype(i32)) - 1
    ids = jnp.where(nonempty, jnp.arange(N_EXPERTS, dtype=i32), N_EXPERTS)
    later = jnp.flip(lax.cummin(jnp.flip(ids)))
    next_e = jnp.concatenate([later[1:], jnp.full((1,), N_EXPERTS, i32)])
    next_e = jnp.where(next_e == N_EXPERTS, -1, next_e)
    tile_next = lookup(next_e)
    tile_slot = lookup(order % 2)
    tile_nvalid = jnp.clip(lookup(poff + counts) - tile_b * ROW_TILE, 0, ROW_TILE).astype(i32)
    tile_last = (tile_v * (tile == n_tiles - 1)).astype(i32)
    return poff, tile_b, (tile_e, tile_v, tile_first, tile_next, tile_slot, tile_nvalid, tile_last)


def kernel(x, p, g_mix, w_in, w_pool, pool_scale, attn_sinks, rpe_table, w_out, g_ffn, w_router, router_bias,
           w_gate, w_up, w_down, ws_gate, ws_up, ws_down, w_ple_proj, g_ple, w_ple_gate, g_final):
    batch, seq, _ = x.shape
    tokens = batch * seq
    assert seq % MIX_TOKENS == 0 and tokens % META_TOKENS == 0 and tokens % FINAL_TOKENS == 0
    n_rows = tokens * TOP_K + N_EXPERTS * ROW_TILE
    bucket = jnp.asarray(_rpe_bucket_map())
    h, h2, idx, topw, rank, cnt = _mix_call(
        x, g_mix[0][None, :], w_in[0].astype(BF16), w_pool[0].astype(BF16), pool_scale[0][None, :],
        attn_sinks[0], rpe_table, bucket, w_out[0].astype(BF16), g_ffn[0][None, :],
        w_router[0].T, router_bias[0][:, None])

    poff, tile_b, tile_meta = _tile_metadata(cnt[:, 0].astype(jnp.int32), n_rows // ROW_TILE)
    experts = jnp.arange(N_EXPERTS, dtype=jnp.int32)[:, None, None]
    dest = jnp.sum(jnp.where(idx[None] == experts, poff[:, None, None], 0), axis=0) + rank
    meta = _meta_call(_per_step(dest, META_TOKENS), n_rows)
    ys = _expert_call(tile_meta, tile_b, meta, h2, w_gate[0], w_up[0], w_down[0], tokens)
    out = _final_call(
        h.reshape(tokens, D_MODEL), g_ffn[0][None, :], p[0].reshape(tokens, PLE_DIM), topw.T, ys,
        ws_gate[0].astype(BF16), ws_up[0].astype(BF16), ws_down[0].astype(BF16),
        w_ple_proj[0].astype(BF16), g_ple[0][None, :], w_ple_gate[0].astype(BF16), g_final[None, :])
    return out.reshape(batch, seq, D_MODEL)
```

```python
import math

import numpy as np
import jax
import jax.numpy as jnp
from jax import lax
from jax.experimental import pallas as pl
from jax.experimental.pallas import tpu as pltpu

F32 = jnp.float32
BF16 = jnp.bfloat16
NEG_INF = float("-inf")

D_MODEL = 2048
PLE_DIM = 256
POOL_WIDTH = 1024
POOL_GROUPS = 4
POOL_GROUP_DIM = POOL_WIDTH // POOL_GROUPS
POOL_WINDOWS = (2, 4, 8, 16)
N_HEADS = 16
N_KV_HEADS = 2
HEAD_DIM = 64
HEADS_PER_KV = N_HEADS // N_KV_HEADS
ATTN_WIDTH = N_HEADS * HEAD_DIM
KV_WIDTH = N_KV_HEADS * HEAD_DIM
MIX_WIDTH = POOL_WIDTH + ATTN_WIDTH
IN_WIDTH = POOL_WIDTH + ATTN_WIDTH + 2 * KV_WIDTH
Q_OFF = POOL_WIDTH
K_OFF = POOL_WIDTH + ATTN_WIDTH
V_OFF = K_OFF + KV_WIDTH
ATTN_BLOCK = 128
WINDOW = 128
RPE_BUCKETS = 32
RPE_MAX_EXACT = RPE_BUCKETS // 2
RPE_MAX_DISTANCE = 128
N_EXPERTS = 64
TOP_K = 8
N_EXPERT_GROUPS = 8
EXPERTS_PER_GROUP = N_EXPERTS // N_EXPERT_GROUPS
TOPK_GROUPS = 4
EXPERT_FF = 512
SHARED_FF = 512
ROUTED_SCALE = 2.5
EPS = 1e-6

LANES = 128
SUBLANES = 8
VMEM_LIMIT_BYTES = 58 * 1024 * 1024

MIX_TOKENS = 256
POOL_HISTORY = 16
ROW_TILE = 256
DISPATCH_TOKENS = 512
FINAL_TOKENS = 256
WEIGHT_CAST_ROWS = 256
HALF_MODEL = D_MODEL // 2
ROW_CHUNKS = HALF_MODEL // LANES
TOKEN_PITCH = (TOP_K + 1) * ROW_CHUNKS
U32 = jnp.uint32


def _rms(x, g):
    return x * lax.rsqrt(jnp.mean(x * x, axis=-1, keepdims=True) + EPS) * g


def _sigmoid(x):
    return 1.0 / (1.0 + jnp.exp(-x))


def _pack_bf16_pair(lo, hi):
    ulo = lax.bitcast_convert_type(lo.astype(BF16).astype(F32), U32)
    uhi = lax.bitcast_convert_type(hi.astype(BF16).astype(F32), U32)
    return (ulo >> 16) | uhi


def _unpack_bf16_pair(w):
    lo = lax.bitcast_convert_type(w << 16, F32)
    hi = lax.bitcast_convert_type(w & jnp.uint32(0xFFFF0000), F32)
    return lo, hi


def _dot(a, b):
    return jnp.dot(a, b, preferred_element_type=F32)


def _dot_nt(a, b, precision=None):
    return lax.dot_general(a, b, (((1,), (1,)), ((), ())), preferred_element_type=F32, precision=precision)


def _rpe_bucket_map():
    i = np.arange(ATTN_BLOCK)[:, None]
    j = np.arange(2 * ATTN_BLOCK)[None, :]
    dist = i + ATTN_BLOCK - j
    n = np.maximum(dist, 0)
    nf = np.maximum(n, 1).astype(np.float32)
    large = RPE_MAX_EXACT + (np.log(nf / np.float32(RPE_MAX_EXACT)) / np.float32(math.log(RPE_MAX_DISTANCE / RPE_MAX_EXACT))
                             * np.float32(RPE_BUCKETS - RPE_MAX_EXACT)).astype(np.int32)
    large = np.minimum(large, RPE_BUCKETS - 1)
    bucket = np.where(n < RPE_MAX_EXACT, n, large)
    valid = (dist >= 0) & (dist < WINDOW)
    return np.where(valid, bucket, -1).astype(np.int32)


def _mix_kernel(x_ref, gmix_ref, win_ref, wpool_ref, pscale_ref, sinks_ref, rpe_ref, bucket_ref,
                wout_ref, gffn_ref, wrt_ref, rbias_ref,
                h_ref, h2_ref, idx_ref, topw_ref, rank_ref, cnt_ref,
                ubuf, kbuf, vbuf, bias_buf, ybuf, cnt_acc):
    TS = MIX_TOKENS
    b = pl.program_id(0)
    s = pl.program_id(1)

    @pl.when((b == 0) & (s == 0))
    def _init():
        bucket = bucket_ref[...]
        for h in range(N_HEADS):
            acc = jnp.full((ATTN_BLOCK, 2 * ATTN_BLOCK), NEG_INF, F32)
            for bk in range(RPE_BUCKETS):
                acc = jnp.where(bucket == bk, rpe_ref[bk, h], acc)
            bias_buf[h * ATTN_BLOCK:(h + 1) * ATTN_BLOCK, :] = acc
        cnt_acc[...] = jnp.zeros_like(cnt_acc)

    @pl.when(s == 0)
    def _reset_history():
        ubuf[0:POOL_HISTORY, :] = jnp.zeros((POOL_HISTORY, POOL_WIDTH), F32)
        kbuf[0:ATTN_BLOCK, :] = jnp.zeros((ATTN_BLOCK, KV_WIDTH), BF16)
        vbuf[0:ATTN_BLOCK, :] = jnp.zeros((ATTN_BLOCK, KV_WIDTH), BF16)

    x = x_ref[...]
    a = _rms(x, gmix_ref[...]).astype(BF16)
    z = _dot(a, win_ref[...])

    ubuf[POOL_HISTORY:POOL_HISTORY + TS, :] = z[:, 0:POOL_WIDTH]
    pos = s * TS + lax.broadcasted_iota(jnp.int32, (TS, 1), 0)
    for gi, w in enumerate(POOL_WINDOWS):
        c0, c1 = gi * POOL_GROUP_DIM, (gi + 1) * POOL_GROUP_DIM
        e = ubuf[:, c0:c1]
        shift = 1
        while shift < w:
            e = e + pltpu.roll(e, shift, axis=0)
            shift *= 2
        wsum = e[POOL_HISTORY:, :]
        count = jnp.minimum(pos + 1, w).astype(F32)
        pooled = wsum / count - z[:, c0:c1]
        yp = _dot(pooled.astype(BF16), wpool_ref[gi]) * pscale_ref[:, c0:c1]
        ybuf[:, c0:c1] = yp.astype(BF16)
    ubuf[0:POOL_HISTORY, :] = ubuf[TS:TS + POOL_HISTORY, :]

    kbuf[ATTN_BLOCK:ATTN_BLOCK + TS, :] = z[:, K_OFF:K_OFF + KV_WIDTH].astype(BF16)
    vbuf[ATTN_BLOCK:ATTN_BLOCK + TS, :] = z[:, V_OFF:V_OFF + KV_WIDTH].astype(BF16)
    lane = lax.broadcasted_iota(jnp.int32, (ATTN_BLOCK, LANES), 1)
    low_half = lane < HEAD_DIM
    high_half = jnp.logical_not(low_half)
    col = lax.broadcasted_iota(jnp.int32, (1, 2 * ATTN_BLOCK), 1)
    first_mask = jnp.where((col < ATTN_BLOCK) & (s == 0), NEG_INF, 0.0).astype(F32)
    for sb in range(TS // ATTN_BLOCK):
        r0 = sb * ATTN_BLOCK
        kband = kbuf[r0:r0 + 2 * ATTN_BLOCK, :]
        vband = vbuf[r0:r0 + 2 * ATTN_BLOCK, :]
        for p in range(N_HEADS // 2):
            qp = z[r0:r0 + ATTN_BLOCK, Q_OFF + p * LANES:Q_OFF + (p + 1) * LANES] * (HEAD_DIM ** -0.5)
            qr = pltpu.roll(qp, HEAD_DIM, axis=1)
            kvh = (2 * p) // HEADS_PER_KV
            kv_lanes = low_half if kvh == 0 else high_half
            outs = []
            for par in range(2):
                h = 2 * p + par
                qh = jnp.where(kv_lanes, qp if par == kvh else qr, 0.0).astype(BF16)
                lg = _dot_nt(qh, kband) + bias_buf[h * ATTN_BLOCK:(h + 1) * ATTN_BLOCK, :]
                if sb == 0:
                    lg = lg + first_mask
                sink = sinks_ref[h]
                m = jnp.maximum(jnp.max(lg, axis=-1, keepdims=True), sink)
                pe = jnp.exp(lg - m)
                den = jnp.sum(pe, axis=-1, keepdims=True) + jnp.exp(sink - m)
                outs.append(_dot(pe.astype(BF16), vband) / den)
            if kvh == 0:
                pair = jnp.where(low_half, outs[0], pltpu.roll(outs[1], HEAD_DIM, axis=1))
            else:
                pair = jnp.where(low_half, pltpu.roll(outs[0], HEAD_DIM, axis=1), outs[1])
            ybuf[r0:r0 + ATTN_BLOCK, POOL_WIDTH + p * LANES:POOL_WIDTH + (p + 1) * LANES] = pair.astype(BF16)
    kbuf[0:ATTN_BLOCK, :] = kbuf[TS:TS + ATTN_BLOCK, :]
    vbuf[0:ATTN_BLOCK, :] = vbuf[TS:TS + ATTN_BLOCK, :]

    h = x + _dot(ybuf[...], wout_ref[...])
    h_ref[...] = h
    h2 = _rms(h, gffn_ref[...])
    for c in range(ROW_CHUNKS):
        h2_ref[pl.ds(c, TS, stride=ROW_CHUNKS), :] = _pack_bf16_pair(
            h2[:, c * LANES:(c + 1) * LANES], h2[:, HALF_MODEL + c * LANES:HALF_MODEL + (c + 1) * LANES])

    logits = _dot_nt(wrt_ref[...], h2, precision=lax.Precision.HIGHEST)
    scores = _sigmoid(logits)
    biased = scores + rbias_ref[...]
    erow = lax.broadcasted_iota(jnp.int32, (N_EXPERTS, TS), 0)
    grow = lax.broadcasted_iota(jnp.int32, (EXPERTS_PER_GROUP, TS), 0)
    group_scores = []
    for g in range(N_EXPERT_GROUPS):
        blk = biased[g * EXPERTS_PER_GROUP:(g + 1) * EXPERTS_PER_GROUP, :]
        m1 = jnp.max(blk, axis=0, keepdims=True)
        i1 = jnp.min(jnp.where(blk == m1, grow, EXPERTS_PER_GROUP), axis=0, keepdims=True)
        m2 = jnp.max(jnp.where(grow == i1, NEG_INF, blk), axis=0, keepdims=True)
        group_scores.append(m1 + m2)
    cur = jnp.concatenate(group_scores, axis=0)
    gsel = jnp.zeros((N_EXPERT_GROUPS, TS), jnp.bool_)
    for _ in range(TOPK_GROUPS):
        m = jnp.max(cur, axis=0, keepdims=True)
        i = jnp.min(jnp.where(cur == m, grow, N_EXPERT_GROUPS), axis=0, keepdims=True)
        hit = grow == i
        gsel = jnp.logical_or(gsel, hit)
        cur = jnp.where(hit, NEG_INF, cur)
    gmask = jnp.concatenate(
        [jnp.broadcast_to(gsel[g:g + 1, :], (EXPERTS_PER_GROUP, TS)) for g in range(N_EXPERT_GROUPS)], axis=0)
    masked = jnp.where(gmask, biased, NEG_INF)
    sel = jnp.zeros((N_EXPERTS, TS), jnp.bool_)
    idxs, ws = [], []
    for _ in range(TOP_K):
        m = jnp.max(masked, axis=0, keepdims=True)
        i = jnp.min(jnp.where(masked == m, erow, N_EXPERTS), axis=0, keepdims=True)
        hit = erow == i
        idxs.append(i)
        ws.append(jnp.sum(jnp.where(hit, scores, 0.0), axis=0, keepdims=True))
        sel = jnp.logical_or(sel, hit)
        masked = jnp.where(hit, NEG_INF, masked)
    wtot = ws[0]
    for wk in ws[1:]:
        wtot = wtot + wk
    idx_ref[...] = jnp.concatenate(idxs, axis=0)
    topw_ref[...] = jnp.concatenate([wk / wtot * ROUTED_SCALE for wk in ws], axis=0)

    self32 = sel.astype(F32)
    ri = lax.broadcasted_iota(jnp.int32, (TS, TS), 0)
    ci = lax.broadcasted_iota(jnp.int32, (TS, TS), 1)
    before = (ri < ci).astype(BF16)
    running = _dot(self32.astype(BF16), before) + cnt_acc[:, 0:1]
    rank_ref[...] = jnp.concatenate(
        [jnp.sum(jnp.where(erow == i, running, 0.0), axis=0, keepdims=True) for i in idxs], axis=0).astype(jnp.int32)
    cnt_acc[...] = cnt_acc[...] + jnp.sum(self32, axis=1, keepdims=True)
    cnt_ref[...] = cnt_acc[...]


def _mix_call(x, g_mix, w_in, w_pool, pool_scale, sinks, rpe_table, bucket, w_out, g_ffn, w_rt, r_bias):
    batch, seq, _ = x.shape
    tokens = batch * seq
    TS = MIX_TOKENS
    ns = seq // TS
    const = lambda shape: pl.BlockSpec(shape, lambda b, s: (0,) * len(shape), pipeline_mode=pl.Buffered(1))
    smem = pl.BlockSpec(memory_space=pltpu.SMEM)
    tok3 = pl.BlockSpec((None, TS, D_MODEL), lambda b, s: (b, s, 0))
    lane_blk = pl.BlockSpec((TOP_K, TS), lambda b, s: (0, b * ns + s))
    return pl.pallas_call(
        _mix_kernel,
        grid=(batch, ns),
        in_specs=[
            tok3,
            const((1, D_MODEL)),
            const((D_MODEL, IN_WIDTH)),
            const((POOL_GROUPS, POOL_GROUP_DIM, POOL_GROUP_DIM)),
            const((1, POOL_WIDTH)),
            smem,
            smem,
            const((ATTN_BLOCK, 2 * ATTN_BLOCK)),
            const((MIX_WIDTH, D_MODEL)),
            const((1, D_MODEL)),
            const((N_EXPERTS, D_MODEL)),
            const((N_EXPERTS, 1)),
        ],
        out_specs=[
            tok3,
            pl.BlockSpec((TS * ROW_CHUNKS, LANES), lambda b, s: (b * ns + s, 0)),
            lane_blk,
            lane_blk,
            lane_blk,
            pl.BlockSpec((N_EXPERTS, LANES), lambda b, s: (0, 0)),
        ],
        out_shape=[
            jax.ShapeDtypeStruct((batch, seq, D_MODEL), F32),
            jax.ShapeDtypeStruct((tokens * ROW_CHUNKS, LANES), U32),
            jax.ShapeDtypeStruct((TOP_K, tokens), jnp.int32),
            jax.ShapeDtypeStruct((TOP_K, tokens), F32),
            jax.ShapeDtypeStruct((TOP_K, tokens), jnp.int32),
            jax.ShapeDtypeStruct((N_EXPERTS, LANES), F32),
        ],
        scratch_shapes=[
            pltpu.VMEM((POOL_HISTORY + TS, POOL_WIDTH), F32),
            pltpu.VMEM((ATTN_BLOCK + TS, KV_WIDTH), BF16),
            pltpu.VMEM((ATTN_BLOCK + TS, KV_WIDTH), BF16),
            pltpu.VMEM((N_HEADS * ATTN_BLOCK, 2 * ATTN_BLOCK), F32),
            pltpu.VMEM((TS, MIX_WIDTH), BF16),
            pltpu.VMEM((N_EXPERTS, LANES), F32),
        ],
        compiler_params=pltpu.CompilerParams(
            dimension_semantics=("arbitrary", "arbitrary"), vmem_limit_bytes=VMEM_LIMIT_BYTES),
        name="mix_router",
    )(x, g_mix, w_in, w_pool, pool_scale, sinks, rpe_table, bucket, w_out, g_ffn, w_rt, r_bias)


def _dispatch_kernel(dest_ref, poff_ref, pend_ref, h2_ref, xs_hbm, meta_ref, zbuf, sem, zsem):
    step = pl.program_id(0)
    base = step * DISPATCH_TOKENS

    def zero_copy(e):
        start = pl.multiple_of((pend_ref[e] - ROW_TILE) * ROW_CHUNKS, ROW_CHUNKS)
        return pltpu.make_async_copy(zbuf, xs_hbm.at[pl.ds(start, ROW_TILE * ROW_CHUNKS)], zsem)

    @pl.when(step == 0)
    def _zero_tails():
        zbuf[...] = jnp.zeros_like(zbuf)

        def start(e, c):
            @pl.when(pend_ref[e] > poff_ref[e])
            def _():
                zero_copy(e).start()
            return c

        def wait(e, c):
            @pl.when(pend_ref[e] > poff_ref[e])
            def _():
                zero_copy(e).wait()
            return c

        lax.fori_loop(0, N_EXPERTS, start, 0)
        lax.fori_loop(0, N_EXPERTS, wait, 0)

    def row_copy(t, k):
        src = pl.multiple_of(t * ROW_CHUNKS, ROW_CHUNKS)
        dest = pl.multiple_of(dest_ref[k, t] * ROW_CHUNKS, ROW_CHUNKS)
        return pltpu.make_async_copy(h2_ref.at[pl.ds(src, ROW_CHUNKS)], xs_hbm.at[pl.ds(dest, ROW_CHUNKS)], sem)

    def start_tok(t, c):
        for k in range(TOP_K):
            row_copy(t, k).start(priority=k % 2)
            meta_ref[dest_ref[k, t]] = (base + t) * TOP_K + k
        return c

    def wait_tok(t, c):
        for k in range(TOP_K):
            row_copy(t, k).wait()
        return c

    lax.fori_loop(0, DISPATCH_TOKENS, start_tok, 0, unroll=4)
    lax.fori_loop(0, DISPATCH_TOKENS, wait_tok, 0)


def _dispatch_call(dest3, poff, pend, h2, n_rows):
    smem_blk = pl.BlockSpec((None, TOP_K, DISPATCH_TOKENS), lambda i: (i, 0, 0), memory_space=pltpu.SMEM)
    smem = pl.BlockSpec(memory_space=pltpu.SMEM)
    return pl.pallas_call(
        _dispatch_kernel,
        grid=(dest3.shape[0],),
        in_specs=[smem_blk, smem, smem,
                  pl.BlockSpec((DISPATCH_TOKENS * ROW_CHUNKS, LANES), lambda i: (i, 0))],
        out_specs=[pl.BlockSpec(memory_space=pl.ANY), smem],
        out_shape=[jax.ShapeDtypeStruct((n_rows * ROW_CHUNKS, LANES), U32),
                   jax.ShapeDtypeStruct((n_rows,), jnp.int32)],
        scratch_shapes=[
            pltpu.VMEM((ROW_TILE * ROW_CHUNKS, LANES), U32),
            pltpu.SemaphoreType.DMA,
            pltpu.SemaphoreType.DMA,
        ],
        compiler_params=pltpu.CompilerParams(dimension_semantics=("arbitrary",)),
        name="dispatch_rows",
    )(dest3, poff, pend, h2)


def _expert_kernel(te_ref, tv_ref, tfirst_ref, tnext_ref, tslot_ref, tnv_ref, tlast_ref,
                   meta_ref, meta_prev_ref, xs_ref, wg_hbm, wu_hbm, wd_hbm, ys_hbm,
                   yo, xb_ref, wg_stage, wu_stage, wd_stage, wg_bf, wu_bf, wd_bf, wsems, ssems,
                   *, dump_row):
    i = pl.program_id(0)
    slot = lax.rem(i, 2)
    tile_rows = ROW_TILE * ROW_CHUNKS

    def scatter_copy(m_ref, n_valid, s, r):
        m = m_ref[0, r]
        dst = jnp.where(r < n_valid, (m + (m >> 3)) * ROW_CHUNKS, dump_row + r * ROW_CHUNKS)
        src = s * tile_rows + r * ROW_CHUNKS
        return pltpu.make_async_copy(yo.at[pl.ds(pl.multiple_of(src, ROW_CHUNKS), ROW_CHUNKS)],
                                     ys_hbm.at[pl.ds(pl.multiple_of(dst, ROW_CHUNKS), ROW_CHUNKS)], ssems.at[s])

    def start_rows(make, m_ref, n_valid, s):
        def body(r, c):
            make(m_ref, n_valid, s, r).start(priority=0)
            make(m_ref, n_valid, s, r + ROW_TILE // 2).start(priority=1)
            return c
        lax.fori_loop(0, ROW_TILE // 2, body, 0, unroll=8)

    def start_rows_inline(make, m_ref, n_valid, s):
        for r in range(ROW_TILE):
            make(m_ref, n_valid, s, r).start(priority=r % 2)

    def wait_rows(src_ref, dst_ref, sem):
        for _ in range(ROW_TILE):
            pltpu.make_async_copy(src_ref.at[pl.ds(0, ROW_CHUNKS)], dst_ref.at[pl.ds(0, ROW_CHUNKS)], sem).wait()

    def weight_copies(e, s):
        return (pltpu.make_async_copy(wg_hbm.at[e], wg_stage.at[s], wsems.at[s, 0]),
                pltpu.make_async_copy(wu_hbm.at[e], wu_stage.at[s], wsems.at[s, 1]),
                pltpu.make_async_copy(wd_hbm.at[e], wd_stage.at[s], wsems.at[s, 2]))

    valid = tv_ref[i] == 1

    @pl.when((i == 0) & valid)
    def _prologue():
        for cp in weight_copies(te_ref[0], tslot_ref[0]):
            cp.start()
        yo[...] = jnp.zeros_like(yo)
        start_rows(scatter_copy, meta_ref, 0, 0)

    n_valid_prev = jnp.where(i >= 1, tnv_ref[jnp.maximum(i - 1, 0)], 0)

    @pl.when(valid & (tfirst_ref[i] == 1))
    def _new_expert():
        ws = tslot_ref[i]
        for cp in weight_copies(te_ref[i], ws):
            cp.wait()

        @pl.when(tnext_ref[i] >= 0)
        def _():
            for cp in weight_copies(tnext_ref[i], 1 - ws):
                cp.start()

        def cast_in(j, c):
            r = pl.multiple_of(j * WEIGHT_CAST_ROWS, WEIGHT_CAST_ROWS)
            wg_bf[pl.ds(r, WEIGHT_CAST_ROWS), :] = wg_stage[ws, pl.ds(r, WEIGHT_CAST_ROWS), :].astype(BF16)
            wu_bf[pl.ds(r, WEIGHT_CAST_ROWS), :] = wu_stage[ws, pl.ds(r, WEIGHT_CAST_ROWS), :].astype(BF16)
            return c

        def cast_out(j, c):
            r = pl.multiple_of(j * (WEIGHT_CAST_ROWS // 4), WEIGHT_CAST_ROWS // 4)
            wd_bf[pl.ds(r, WEIGHT_CAST_ROWS // 4), :] = wd_stage[ws, pl.ds(r, WEIGHT_CAST_ROWS // 4), :].astype(BF16)
            return c

        lax.fori_loop(0, D_MODEL // WEIGHT_CAST_ROWS, cast_in, 0)
        lax.fori_loop(0, EXPERT_FF // (WEIGHT_CAST_ROWS // 4), cast_out, 0)

    @pl.when(valid)
    def _tile():
        base = pl.multiple_of(slot * tile_rows, tile_rows)
        for c in range(ROW_CHUNKS):
            lo, hi = _unpack_bf16_pair(xs_ref[pl.ds(c, ROW_TILE, stride=ROW_CHUNKS), :])
            xb_ref[:, c * LANES:(c + 1) * LANES] = lo.astype(BF16)
            xb_ref[:, HALF_MODEL + c * LANES:HALF_MODEL + (c + 1) * LANES] = hi.astype(BF16)
        start_rows_inline(scatter_copy, meta_prev_ref, n_valid_prev, 1 - slot)
        xb = xb_ref[...]
        g = _dot(xb, wg_bf[...])
        u = _dot(xb, wu_bf[...])
        hb = (g * _sigmoid(g) * u).astype(BF16)
        y = _dot(hb, wd_bf[...])
        wait_rows(yo, ys_hbm, ssems.at[slot])
        for c in range(ROW_CHUNKS):
            yo[pl.ds(base + c, ROW_TILE, stride=ROW_CHUNKS), :] = _pack_bf16_pair(
                y[:, c * LANES:(c + 1) * LANES], y[:, HALF_MODEL + c * LANES:HALF_MODEL + (c + 1) * LANES])

    @pl.when(valid & (tlast_ref[i] == 1))
    def _drain():
        start_rows(scatter_copy, meta_ref, tnv_ref[i], slot)
        wait_rows(yo, ys_hbm, ssems.at[1 - slot])
        wait_rows(yo, ys_hbm, ssems.at[slot])


def _expert_call(tile_meta, tile_b, meta, xs, w_gate, w_up, w_down, tokens):
    n_tiles = tile_b.shape[0]
    meta3 = meta.reshape(n_tiles, 1, ROW_TILE)
    dump_row = tokens * TOKEN_PITCH
    meta_blk = pl.BlockSpec((None, 1, ROW_TILE), lambda i, te, tv, tf, tn, ts, tnv, tl, tb: (tb[i], 0, 0),
                            memory_space=pltpu.SMEM)
    row_blk = pl.BlockSpec((ROW_TILE * ROW_CHUNKS, LANES), lambda i, te, tv, tf, tn, ts, tnv, tl, tb: (tb[i], 0))
    meta_prev_blk = pl.BlockSpec(
        (None, 1, ROW_TILE), lambda i, te, tv, tf, tn, ts, tnv, tl, tb: (tb[jnp.maximum(i - 1, 0)], 0, 0),
        memory_space=pltpu.SMEM)
    any_spec = pl.BlockSpec(memory_space=pl.ANY)

    def body(te, tv, tf, tn, ts, tnv, tl, tb, *refs):
        _expert_kernel(te, tv, tf, tn, ts, tnv, tl, *refs, dump_row=dump_row)

    return pl.pallas_call(
        body,
        grid_spec=pltpu.PrefetchScalarGridSpec(
            num_scalar_prefetch=len(tile_meta) + 1,
            grid=(n_tiles,),
            in_specs=[meta_blk, meta_prev_blk, row_blk, any_spec, any_spec, any_spec],
            out_specs=any_spec,
            scratch_shapes=[
                pltpu.VMEM((2 * ROW_TILE * ROW_CHUNKS, LANES), U32),
                pltpu.VMEM((ROW_TILE, D_MODEL), BF16),
                pltpu.VMEM((2, D_MODEL, EXPERT_FF), F32),
                pltpu.VMEM((2, D_MODEL, EXPERT_FF), F32),
                pltpu.VMEM((2, EXPERT_FF, D_MODEL), F32),
                pltpu.VMEM((D_MODEL, EXPERT_FF), BF16),
                pltpu.VMEM((D_MODEL, EXPERT_FF), BF16),
                pltpu.VMEM((EXPERT_FF, D_MODEL), BF16),
                pltpu.SemaphoreType.DMA((2, 3)),
                pltpu.SemaphoreType.DMA((2,)),
            ],
        ),
        out_shape=jax.ShapeDtypeStruct((dump_row + ROW_TILE * ROW_CHUNKS, LANES), U32),
        compiler_params=pltpu.CompilerParams(
            dimension_semantics=("arbitrary",), vmem_limit_bytes=VMEM_LIMIT_BYTES),
        name="routed_experts",
    )(*tile_meta, tile_b, meta3, meta3, xs, w_gate, w_up, w_down)


def _final_kernel(h_ref, gffn_ref, p_ref, wt_ref, ys_ref,
                  wsg_ref, wsu_ref, wsd_ref, wpp_ref, gple_ref, wpg_ref, gfin_ref,
                  out_ref, h3_ref):
    TB = FINAL_TOKENS
    h = h_ref[...]
    h2b = _rms(h, gffn_ref[...]).astype(BF16)
    g = _dot(h2b, wsg_ref[...])
    u = _dot(h2b, wsu_ref[...])
    shared = _dot((g * _sigmoid(g) * u).astype(BF16), wsd_ref[...])
    ple = _rms(_dot(p_ref[...].astype(BF16), wpp_ref[...]), gple_ref[...])

    wt = wt_ref[...]
    for c in range(ROW_CHUNKS):
        routed_lo = routed_hi = None
        for k in range(TOP_K):
            lo, hi = _unpack_bf16_pair(ys_ref[pl.ds(k * ROW_CHUNKS + c, TB, stride=TOKEN_PITCH), :])
            wk = wt[:, k:k + 1]
            routed_lo = lo * wk if k == 0 else routed_lo + lo * wk
            routed_hi = hi * wk if k == 0 else routed_hi + hi * wk
        for routed, c0 in ((routed_lo, c * LANES), (routed_hi, HALF_MODEL + c * LANES)):
            cs = slice(c0, c0 + LANES)
            h3_ref[:, cs] = h[:, cs] + (routed + shared[:, cs])

    h3 = h3_ref[...]
    gate = _sigmoid(_dot(h3.astype(BF16), wpg_ref[...]))
    h4 = h3 + gate * ple
    out_ref[...] = _rms(h4, gfin_ref[...])


def _final_call(h, g_ffn, p, wt, ys, ws_gate, ws_up, ws_down, w_pp, g_ple, w_pg, g_final):
    TB = FINAL_TOKENS
    tokens = h.shape[0]
    const = lambda shape: pl.BlockSpec(shape, lambda i: (0,) * len(shape), pipeline_mode=pl.Buffered(1))
    tok = pl.BlockSpec((TB, D_MODEL), lambda i: (i, 0))
    return pl.pallas_call(
        _final_kernel,
        grid=(tokens // TB,),
        in_specs=[
            tok,
            const((1, D_MODEL)),
            pl.BlockSpec((TB, PLE_DIM), lambda i: (i, 0)),
            pl.BlockSpec((TB, TOP_K), lambda i: (i, 0)),
            pl.BlockSpec((TB * TOKEN_PITCH, LANES), lambda i: (i, 0)),
            const((D_MODEL, SHARED_FF)),
            const((D_MODEL, SHARED_FF)),
            const((SHARED_FF, D_MODEL)),
            const((PLE_DIM, D_MODEL)),
            const((1, D_MODEL)),
            const((D_MODEL, D_MODEL)),
            const((1, D_MODEL)),
        ],
        out_specs=tok,
        out_shape=jax.ShapeDtypeStruct((tokens, D_MODEL), F32),
        scratch_shapes=[pltpu.VMEM((TB, D_MODEL), F32)],
        compiler_params=pltpu.CompilerParams(
            dimension_semantics=("arbitrary",), vmem_limit_bytes=VMEM_LIMIT_BYTES),
        name="combine_final",
    )(h, g_ffn, p, wt, ys, ws_gate, ws_up, ws_down, w_pp, g_ple, w_pg, g_final)


def _per_step(a, tokens_per_step):
    tokens = a.shape[1]
    return a.reshape(TOP_K, tokens // tokens_per_step, tokens_per_step).transpose(1, 0, 2)


def _tile_metadata(counts, n_tiles_max):
    i32 = jnp.int32
    padded = (counts + ROW_TILE - 1) // ROW_TILE * ROW_TILE
    pend = jnp.cumsum(padded).astype(i32)
    poff = pend - padded
    n_tiles = pend[-1] // ROW_TILE
    tile = jnp.arange(n_tiles_max, dtype=i32)
    tile_v = (tile < n_tiles).astype(i32)
    tile_b = jnp.minimum(tile, jnp.maximum(n_tiles - 1, 0))
    tile_e = jnp.minimum(jnp.sum(pend[None, :] <= (tile_b * ROW_TILE)[:, None], axis=1), N_EXPERTS - 1).astype(i32)
    onehot = tile_e[:, None] == jnp.arange(N_EXPERTS, dtype=i32)[None, :]

    def lookup(table):
        return jnp.sum(jnp.where(onehot, table[None, :], 0), axis=1).astype(i32)

    tile_first = (tile_v * (tile_b * ROW_TILE == lookup(poff))).astype(i32)
    nonempty = counts > 0
    order = jnp.cumsum(nonempty.astype(i32)) - 1
    ids = jnp.where(nonempty, jnp.arange(N_EXPERTS, dtype=i32), N_EXPERTS)
    later = jnp.flip(lax.cummin(jnp.flip(ids)))
    next_e = jnp.concatenate([later[1:], jnp.full((1,), N_EXPERTS, i32)])
    next_e = jnp.where(next_e == N_EXPERTS, -1, next_e)
    tile_next = lookup(next_e)
    tile_slot = lookup(order % 2)
    tile_nvalid = jnp.clip(lookup(poff + counts) - tile_b * ROW_TILE, 0, ROW_TILE).astype(i32)
    tile_last = (tile_v * (tile == n_tiles - 1)).astype(i32)
    return poff, pend, tile_b, (tile_e, tile_v, tile_first, tile_next, tile_slot, tile_nvalid, tile_last)


def kernel(x, p, g_mix, w_in, w_pool, pool_scale, attn_sinks, rpe_table, w_out, g_ffn, w_router, router_bias,
           w_gate, w_up, w_down, ws_gate, ws_up, ws_down, w_ple_proj, g_ple, w_ple_gate, g_final):
    batch, seq, _ = x.shape
    tokens = batch * seq
    assert seq % MIX_TOKENS == 0 and tokens % DISPATCH_TOKENS == 0 and tokens % FINAL_TOKENS == 0
    n_rows = tokens * TOP_K + N_EXPERTS * ROW_TILE
    bucket = jnp.asarray(_rpe_bucket_map())
    h, h2, idx, topw, rank, cnt = _mix_call(
        x, g_mix[0][None, :], w_in[0].astype(BF16), w_pool[0].astype(BF16), pool_scale[0][None, :],
        attn_sinks[0], rpe_table, bucket, w_out[0].astype(BF16), g_ffn[0][None, :],
        w_router[0].T, router_bias[0][:, None])

    poff, pend, tile_b, tile_meta = _tile_metadata(cnt[:, 0].astype(jnp.int32), n_rows // ROW_TILE)
    experts = jnp.arange(N_EXPERTS, dtype=jnp.int32)[:, None, None]
    dest = jnp.sum(jnp.where(idx[None] == experts, poff[:, None, None], 0), axis=0) + rank
    xs, meta = _dispatch_call(_per_step(dest, DISPATCH_TOKENS), poff, pend, h2, n_rows)
    ys = _expert_call(tile_meta, tile_b, meta, xs, w_gate[0], w_up[0], w_down[0], tokens)
    out = _final_call(
        h.reshape(tokens, D_MODEL), g_ffn[0][None, :], p[0].reshape(tokens, PLE_DIM), topw.T, ys,
        ws_gate[0].astype(BF16), ws_up[0].astype(BF16), ws_down[0].astype(BF16),
        w_ple_proj[0].astype(BF16), g_ple[0][None, :], w_ple_gate[0].astype(BF16), g_final[None, :])
    return out.reshape(batch, seq, D_MODEL)
```

```python
import math

import numpy as np
import jax
import jax.numpy as jnp
from jax import lax
from jax.experimental import pallas as pl
from jax.experimental.pallas import tpu as pltpu

F32 = jnp.float32
BF16 = jnp.bfloat16
NEG_INF = float("-inf")

D_MODEL = 2048
PLE_DIM = 256
POOL_WIDTH = 1024
POOL_GROUPS = 4
POOL_GROUP_DIM = POOL_WIDTH // POOL_GROUPS
POOL_WINDOWS = (2, 4, 8, 16)
N_HEADS = 16
N_KV_HEADS = 2
HEAD_DIM = 64
HEADS_PER_KV = N_HEADS // N_KV_HEADS
ATTN_WIDTH = N_HEADS * HEAD_DIM
KV_WIDTH = N_KV_HEADS * HEAD_DIM
MIX_WIDTH = POOL_WIDTH + ATTN_WIDTH
IN_WIDTH = POOL_WIDTH + ATTN_WIDTH + 2 * KV_WIDTH
Q_OFF = POOL_WIDTH
K_OFF = POOL_WIDTH + ATTN_WIDTH
V_OFF = K_OFF + KV_WIDTH
ATTN_BLOCK = 128
WINDOW = 128
RPE_BUCKETS = 32
RPE_MAX_EXACT = RPE_BUCKETS // 2
RPE_MAX_DISTANCE = 128
N_EXPERTS = 64
TOP_K = 8
N_EXPERT_GROUPS = 8
EXPERTS_PER_GROUP = N_EXPERTS // N_EXPERT_GROUPS
TOPK_GROUPS = 4
EXPERT_FF = 512
SHARED_FF = 512
ROUTED_SCALE = 2.5
EPS = 1e-6

LANES = 128
SUBLANES = 8
VMEM_LIMIT_BYTES = 58 * 1024 * 1024

MIX_TOKENS = 256
POOL_HISTORY = 16
ROW_TILE = 256
DISPATCH_TOKENS = 512
FINAL_TOKENS = 256
WEIGHT_CAST_ROWS = 256
HALF_MODEL = D_MODEL // 2
ROW_CHUNKS = HALF_MODEL // LANES
TOKEN_PITCH = (TOP_K + 1) * ROW_CHUNKS
U32 = jnp.uint32


def _rms(x, g):
    return x * lax.rsqrt(jnp.mean(x * x, axis=-1, keepdims=True) + EPS) * g


def _sigmoid(x):
    return 1.0 / (1.0 + jnp.exp(-x))


def _pack_bf16_pair(lo, hi):
    ulo = lax.bitcast_convert_type(lo.astype(BF16).astype(F32), U32)
    uhi = lax.bitcast_convert_type(hi.astype(BF16).astype(F32), U32)
    return (ulo >> 16) | uhi


def _unpack_bf16_pair(w):
    lo = lax.bitcast_convert_type(w << 16, F32)
    hi = lax.bitcast_convert_type(w & jnp.uint32(0xFFFF0000), F32)
    return lo, hi


def _dot(a, b):
    return jnp.dot(a, b, preferred_element_type=F32)


def _dot_nt(a, b, precision=None):
    return lax.dot_general(a, b, (((1,), (1,)), ((), ())), preferred_element_type=F32, precision=precision)


def _rpe_bucket_map():
    i = np.arange(ATTN_BLOCK)[:, None]
    j = np.arange(2 * ATTN_BLOCK)[None, :]
    dist = i + ATTN_BLOCK - j
    n = np.maximum(dist, 0)
    nf = np.maximum(n, 1).astype(np.float32)
    large = RPE_MAX_EXACT + (np.log(nf / np.float32(RPE_MAX_EXACT)) / np.float32(math.log(RPE_MAX_DISTANCE / RPE_MAX_EXACT))
                             * np.float32(RPE_BUCKETS - RPE_MAX_EXACT)).astype(np.int32)
    large = np.minimum(large, RPE_BUCKETS - 1)
    bucket = np.where(n < RPE_MAX_EXACT, n, large)
    valid = (dist >= 0) & (dist < WINDOW)
    return np.where(valid, bucket, -1).astype(np.int32)


def _mix_kernel(x_ref, gmix_ref, win_ref, wpool_ref, pscale_ref, sinks_ref, rpe_ref, bucket_ref,
                wout_ref, gffn_ref, wrt_ref, rbias_ref,
                h_ref, h2_ref, idx_ref, topw_ref, rank_ref, cnt_ref,
                ubuf, kbuf, vbuf, bias_buf, ybuf, cnt_acc):
    TS = MIX_TOKENS
    b = pl.program_id(0)
    s = pl.program_id(1)

    @pl.when((b == 0) & (s == 0))
    def _init():
        bucket = bucket_ref[...]
        for h in range(N_HEADS):
            acc = jnp.full((ATTN_BLOCK, 2 * ATTN_BLOCK), NEG_INF, F32)
            for bk in range(RPE_BUCKETS):
                acc = jnp.where(bucket == bk, rpe_ref[bk, h], acc)
            bias_buf[h * ATTN_BLOCK:(h + 1) * ATTN_BLOCK, :] = acc
        cnt_acc[...] = jnp.zeros_like(cnt_acc)

    @pl.when(s == 0)
    def _reset_history():
        ubuf[0:POOL_HISTORY, :] = jnp.zeros((POOL_HISTORY, POOL_WIDTH), F32)
        kbuf[0:ATTN_BLOCK, :] = jnp.zeros((ATTN_BLOCK, KV_WIDTH), BF16)
        vbuf[0:ATTN_BLOCK, :] = jnp.zeros((ATTN_BLOCK, KV_WIDTH), BF16)

    x = x_ref[...]
    a = _rms(x, gmix_ref[...]).astype(BF16)
    z = _dot(a, win_ref[...])

    ubuf[POOL_HISTORY:POOL_HISTORY + TS, :] = z[:, 0:POOL_WIDTH]
    pos = s * TS + lax.broadcasted_iota(jnp.int32, (TS, 1), 0)
    for gi, w in enumerate(POOL_WINDOWS):
        c0, c1 = gi * POOL_GROUP_DIM, (gi + 1) * POOL_GROUP_DIM
        e = ubuf[:, c0:c1]
        shift = 1
        while shift < w:
            e = e + pltpu.roll(e, shift, axis=0)
            shift *= 2
        wsum = e[POOL_HISTORY:, :]
        count = jnp.minimum(pos + 1, w).astype(F32)
        pooled = wsum / count - z[:, c0:c1]
        yp = _dot(pooled.astype(BF16), wpool_ref[gi]) * pscale_ref[:, c0:c1]
        ybuf[:, c0:c1] = yp.astype(BF16)
    ubuf[0:POOL_HISTORY, :] = ubuf[TS:TS + POOL_HISTORY, :]

    kbuf[ATTN_BLOCK:ATTN_BLOCK + TS, :] = z[:, K_OFF:K_OFF + KV_WIDTH].astype(BF16)
    vbuf[ATTN_BLOCK:ATTN_BLOCK + TS, :] = z[:, V_OFF:V_OFF + KV_WIDTH].astype(BF16)
    lane = lax.broadcasted_iota(jnp.int32, (ATTN_BLOCK, LANES), 1)
    low_half = lane < HEAD_DIM
    high_half = jnp.logical_not(low_half)
    col = lax.broadcasted_iota(jnp.int32, (1, 2 * ATTN_BLOCK), 1)
    first_mask = jnp.where((col < ATTN_BLOCK) & (s == 0), NEG_INF, 0.0).astype(F32)
    for sb in range(TS // ATTN_BLOCK):
        r0 = sb * ATTN_BLOCK
        kband = kbuf[r0:r0 + 2 * ATTN_BLOCK, :]
        vband = vbuf[r0:r0 + 2 * ATTN_BLOCK, :]
        for p in range(N_HEADS // 2):
            qp = z[r0:r0 + ATTN_BLOCK, Q_OFF + p * LANES:Q_OFF + (p + 1) * LANES] * (HEAD_DIM ** -0.5)
            qr = pltpu.roll(qp, HEAD_DIM, axis=1)
            kvh = (2 * p) // HEADS_PER_KV
            kv_lanes = low_half if kvh == 0 else high_half
            outs = []
            for par in range(2):
                h = 2 * p + par
                qh = jnp.where(kv_lanes, qp if par == kvh else qr, 0.0).astype(BF16)
                lg = _dot_nt(qh, kband) + bias_buf[h * ATTN_BLOCK:(h + 1) * ATTN_BLOCK, :]
                if sb == 0:
                    lg = lg + first_mask
                sink = sinks_ref[h]
                m = jnp.maximum(jnp.max(lg, axis=-1, keepdims=True), sink)
                pe = jnp.exp(lg - m)
                den = jnp.sum(pe, axis=-1, keepdims=True) + jnp.exp(sink - m)
                outs.append(_dot(pe.astype(BF16), vband) / den)
            if kvh == 0:
                pair = jnp.where(low_half, outs[0], pltpu.roll(outs[1], HEAD_DIM, axis=1))
            else:
                pair = jnp.where(low_half, pltpu.roll(outs[0], HEAD_DIM, axis=1), outs[1])
            ybuf[r0:r0 + ATTN_BLOCK, POOL_WIDTH + p * LANES:POOL_WIDTH + (p + 1) * LANES] = pair.astype(BF16)
    kbuf[0:ATTN_BLOCK, :] = kbuf[TS:TS + ATTN_BLOCK, :]
    vbuf[0:ATTN_BLOCK, :] = vbuf[TS:TS + ATTN_BLOCK, :]

    h = x + _dot(ybuf[...], wout_ref[...])
    h_ref[...] = h
    h2 = _rms(h, gffn_ref[...])
    for c in range(ROW_CHUNKS):
        h2_ref[pl.ds(c, TS, stride=ROW_CHUNKS), :] = _pack_bf16_pair(
            h2[:, c * LANES:(c + 1) * LANES], h2[:, HALF_MODEL + c * LANES:HALF_MODEL + (c + 1) * LANES])

    logits = _dot_nt(wrt_ref[...], h2, precision=lax.Precision.HIGHEST)
    scores = _sigmoid(logits)
    biased = scores + rbias_ref[...]
    erow = lax.broadcasted_iota(jnp.int32, (N_EXPERTS, TS), 0)
    grow = lax.broadcasted_iota(jnp.int32, (EXPERTS_PER_GROUP, TS), 0)
    group_scores = []
    for g in range(N_EXPERT_GROUPS):
        blk = biased[g * EXPERTS_PER_GROUP:(g + 1) * EXPERTS_PER_GROUP, :]
        m1 = jnp.max(blk, axis=0, keepdims=True)
        i1 = jnp.min(jnp.where(blk == m1, grow, EXPERTS_PER_GROUP), axis=0, keepdims=True)
        m2 = jnp.max(jnp.where(grow == i1, NEG_INF, blk), axis=0, keepdims=True)
        group_scores.append(m1 + m2)
    cur = jnp.concatenate(group_scores, axis=0)
    gsel = jnp.zeros((N_EXPERT_GROUPS, TS), jnp.bool_)
    for _ in range(TOPK_GROUPS):
        m = jnp.max(cur, axis=0, keepdims=True)
        i = jnp.min(jnp.where(cur == m, grow, N_EXPERT_GROUPS), axis=0, keepdims=True)
        hit = grow == i
        gsel = jnp.logical_or(gsel, hit)
        cur = jnp.where(hit, NEG_INF, cur)
    gmask = jnp.concatenate(
        [jnp.broadcast_to(gsel[g:g + 1, :], (EXPERTS_PER_GROUP, TS)) for g in range(N_EXPERT_GROUPS)], axis=0)
    masked = jnp.where(gmask, biased, NEG_INF)
    sel = jnp.zeros((N_EXPERTS, TS), jnp.bool_)
    idxs, ws = [], []
    for _ in range(TOP_K):
        m = jnp.max(masked, axis=0, keepdims=True)
        i = jnp.min(jnp.where(masked == m, erow, N_EXPERTS), axis=0, keepdims=True)
        hit = erow == i
        idxs.append(i)
        ws.append(jnp.sum(jnp.where(hit, scores, 0.0), axis=0, keepdims=True))
        sel = jnp.logical_or(sel, hit)
        masked = jnp.where(hit, NEG_INF, masked)
    wtot = ws[0]
    for wk in ws[1:]:
        wtot = wtot + wk
    idx_ref[...] = jnp.concatenate(idxs, axis=0)
    topw_ref[...] = jnp.concatenate([wk / wtot * ROUTED_SCALE for wk in ws], axis=0)

    self32 = sel.astype(F32)
    ri = lax.broadcasted_iota(jnp.int32, (TS, TS), 0)
    ci = lax.broadcasted_iota(jnp.int32, (TS, TS), 1)
    before = (ri < ci).astype(BF16)
    running = _dot(self32.astype(BF16), before) + cnt_acc[:, 0:1]
    rank_ref[...] = jnp.concatenate(
        [jnp.sum(jnp.where(erow == i, running, 0.0), axis=0, keepdims=True) for i in idxs], axis=0).astype(jnp.int32)
    cnt_acc[...] = cnt_acc[...] + jnp.sum(self32, axis=1, keepdims=True)
    cnt_ref[...] = cnt_acc[...]


def _mix_call(x, g_mix, w_in, w_pool, pool_scale, sinks, rpe_table, bucket, w_out, g_ffn, w_rt, r_bias):
    batch, seq, _ = x.shape
    tokens = batch * seq
    TS = MIX_TOKENS
    ns = seq // TS
    const = lambda shape: pl.BlockSpec(shape, lambda b, s: (0,) * len(shape), pipeline_mode=pl.Buffered(1))
    smem = pl.BlockSpec(memory_space=pltpu.SMEM)
    tok3 = pl.BlockSpec((None, TS, D_MODEL), lambda b, s: (b, s, 0))
    lane_blk = pl.BlockSpec((TOP_K, TS), lambda b, s: (0, b * ns + s))
    return pl.pallas_call(
        _mix_kernel,
        grid=(batch, ns),
        in_specs=[
            tok3,
            const((1, D_MODEL)),
            const((D_MODEL, IN_WIDTH)),
            const((POOL_GROUPS, POOL_GROUP_DIM, POOL_GROUP_DIM)),
            const((1, POOL_WIDTH)),
            smem,
            smem,
            const((ATTN_BLOCK, 2 * ATTN_BLOCK)),
            const((MIX_WIDTH, D_MODEL)),
            const((1, D_MODEL)),
            const((N_EXPERTS, D_MODEL)),
            const((N_EXPERTS, 1)),
        ],
        out_specs=[
            tok3,
            pl.BlockSpec((TS * ROW_CHUNKS, LANES), lambda b, s: (b * ns + s, 0)),
            lane_blk,
            lane_blk,
            lane_blk,
            pl.BlockSpec((N_EXPERTS, LANES), lambda b, s: (0, 0)),
        ],
        out_shape=[
            jax.ShapeDtypeStruct((batch, seq, D_MODEL), F32),
            jax.ShapeDtypeStruct((tokens * ROW_CHUNKS, LANES), U32),
            jax.ShapeDtypeStruct((TOP_K, tokens), jnp.int32),
            jax.ShapeDtypeStruct((TOP_K, tokens), F32),
            jax.ShapeDtypeStruct((TOP_K, tokens), jnp.int32),
            jax.ShapeDtypeStruct((N_EXPERTS, LANES), F32),
        ],
        scratch_shapes=[
            pltpu.VMEM((POOL_HISTORY + TS, POOL_WIDTH), F32),
            pltpu.VMEM((ATTN_BLOCK + TS, KV_WIDTH), BF16),
            pltpu.VMEM((ATTN_BLOCK + TS, KV_WIDTH), BF16),
            pltpu.VMEM((N_HEADS * ATTN_BLOCK, 2 * ATTN_BLOCK), F32),
            pltpu.VMEM((TS, MIX_WIDTH), BF16),
            pltpu.VMEM((N_EXPERTS, LANES), F32),
        ],
        compiler_params=pltpu.CompilerParams(
            dimension_semantics=("arbitrary", "arbitrary"), vmem_limit_bytes=VMEM_LIMIT_BYTES),
        name="mix_router",
    )(x, g_mix, w_in, w_pool, pool_scale, sinks, rpe_table, bucket, w_out, g_ffn, w_rt, r_bias)


def _dispatch_kernel(dest_ref, poff_ref, pend_ref, h2_ref, xs_hbm, meta_ref, zbuf, sem, zsem):
    step = pl.program_id(0)
    base = step * DISPATCH_TOKENS

    def zero_copy(e):
        start = pl.multiple_of((pend_ref[e] - ROW_TILE) * ROW_CHUNKS, ROW_CHUNKS)
        return pltpu.make_async_copy(zbuf, xs_hbm.at[pl.ds(start, ROW_TILE * ROW_CHUNKS)], zsem)

    @pl.when(step == 0)
    def _zero_tails():
        zbuf[...] = jnp.zeros_like(zbuf)

        def start(e, c):
            @pl.when(pend_ref[e] > poff_ref[e])
            def _():
                zero_copy(e).start()
            return c

        def wait(e, c):
            @pl.when(pend_ref[e] > poff_ref[e])
            def _():
                zero_copy(e).wait()
            return c

        lax.fori_loop(0, N_EXPERTS, start, 0)
        lax.fori_loop(0, N_EXPERTS, wait, 0)

    def row_copy(t, k):
        src = pl.multiple_of(t * ROW_CHUNKS, ROW_CHUNKS)
        dest = pl.multiple_of(dest_ref[k, t] * ROW_CHUNKS, ROW_CHUNKS)
        return pltpu.make_async_copy(h2_ref.at[pl.ds(src, ROW_CHUNKS)], xs_hbm.at[pl.ds(dest, ROW_CHUNKS)], sem)

    def start_tok(t, c):
        for k in range(TOP_K):
            row_copy(t, k).start(priority=k % 2)
        return c

    def record_tok(t, c):
        for k in range(TOP_K):
            meta_ref[dest_ref[k, t]] = (base + t) * TOP_K + k
        return c

    def wait_tok(t, c):
        for k in range(TOP_K):
            row_copy(t, k).wait()
        return c

    lax.fori_loop(0, DISPATCH_TOKENS, start_tok, 0, unroll=4)
    lax.fori_loop(0, DISPATCH_TOKENS, record_tok, 0, unroll=4)
    lax.fori_loop(0, DISPATCH_TOKENS, wait_tok, 0)


def _dispatch_call(dest3, poff, pend, h2, n_rows):
    smem_blk = pl.BlockSpec((None, TOP_K, DISPATCH_TOKENS), lambda i: (i, 0, 0), memory_space=pltpu.SMEM)
    smem = pl.BlockSpec(memory_space=pltpu.SMEM)
    return pl.pallas_call(
        _dispatch_kernel,
        grid=(dest3.shape[0],),
        in_specs=[smem_blk, smem, smem,
                  pl.BlockSpec((DISPATCH_TOKENS * ROW_CHUNKS, LANES), lambda i: (i, 0))],
        out_specs=[pl.BlockSpec(memory_space=pl.ANY), smem],
        out_shape=[jax.ShapeDtypeStruct((n_rows * ROW_CHUNKS, LANES), U32),
                   jax.ShapeDtypeStruct((n_rows,), jnp.int32)],
        scratch_shapes=[
            pltpu.VMEM((ROW_TILE * ROW_CHUNKS, LANES), U32),
            pltpu.SemaphoreType.DMA,
            pltpu.SemaphoreType.DMA,
        ],
        compiler_params=pltpu.CompilerParams(dimension_semantics=("arbitrary",)),
        name="dispatch_rows",
    )(dest3, poff, pend, h2)


def _expert_kernel(te_ref, tv_ref, tfirst_ref, tnext_ref, tslot_ref, tnv_ref, tlast_ref,
                   meta_ref, meta_prev_ref, xs_ref, wg_hbm, wu_hbm, wd_hbm, ys_hbm,
                   yo, xb_ref, wg_stage, wu_stage, wd_stage, wg_bf, wu_bf, wd_bf, wsems, ssems,
                   *, dump_row):
    i = pl.program_id(0)
    slot = lax.rem(i, 2)
    tile_rows = ROW_TILE * ROW_CHUNKS

    def scatter_copy(m_ref, n_valid, s, r):
        m = m_ref[0, r]
        dst = jnp.where(r < n_valid, (m + (m >> 3)) * ROW_CHUNKS, dump_row + r * ROW_CHUNKS)
        src = s * tile_rows + r * ROW_CHUNKS
        return pltpu.make_async_copy(yo.at[pl.ds(pl.multiple_of(src, ROW_CHUNKS), ROW_CHUNKS)],
                                     ys_hbm.at[pl.ds(pl.multiple_of(dst, ROW_CHUNKS), ROW_CHUNKS)], ssems.at[s])

    def start_rows(make, m_ref, n_valid, s):
        def body(r, c):
            make(m_ref, n_valid, s, r).start(priority=0)
            make(m_ref, n_valid, s, r + ROW_TILE // 2).start(priority=1)
            return c
        lax.fori_loop(0, ROW_TILE // 2, body, 0, unroll=8)

    def start_rows_inline(make, m_ref, n_valid, s):
        for r in range(ROW_TILE):
            make(m_ref, n_valid, s, r).start(priority=1)

    def wait_rows(src_ref, dst_ref, sem):
        for _ in range(ROW_TILE):
            pltpu.make_async_copy(src_ref.at[pl.ds(0, ROW_CHUNKS)], dst_ref.at[pl.ds(0, ROW_CHUNKS)], sem).wait()

    def weight_copies(e, s):
        return (pltpu.make_async_copy(wg_hbm.at[e], wg_stage.at[s], wsems.at[s, 0]),
                pltpu.make_async_copy(wu_hbm.at[e], wu_stage.at[s], wsems.at[s, 1]),
                pltpu.make_async_copy(wd_hbm.at[e], wd_stage.at[s], wsems.at[s, 2]))

    valid = tv_ref[i] == 1

    @pl.when((i == 0) & valid)
    def _prologue():
        for cp in weight_copies(te_ref[0], tslot_ref[0]):
            cp.start()
        yo[...] = jnp.zeros_like(yo)
        start_rows(scatter_copy, meta_ref, 0, 0)

    n_valid_prev = jnp.where(i >= 1, tnv_ref[jnp.maximum(i - 1, 0)], 0)

    @pl.when(valid & (tfirst_ref[i] == 1))
    def _new_expert():
        ws = tslot_ref[i]
        for cp in weight_copies(te_ref[i], ws):
            cp.wait()

        @pl.when(tnext_ref[i] >= 0)
        def _():
            for cp in weight_copies(tnext_ref[i], 1 - ws):
                cp.start()

        def cast_in(j, c):
            r = pl.multiple_of(j * WEIGHT_CAST_ROWS, WEIGHT_CAST_ROWS)
            wg_bf[pl.ds(r, WEIGHT_CAST_ROWS), :] = wg_stage[ws, pl.ds(r, WEIGHT_CAST_ROWS), :].astype(BF16)
            wu_bf[pl.ds(r, WEIGHT_CAST_ROWS), :] = wu_stage[ws, pl.ds(r, WEIGHT_CAST_ROWS), :].astype(BF16)
            return c

        def cast_out(j, c):
            r = pl.multiple_of(j * (WEIGHT_CAST_ROWS // 4), WEIGHT_CAST_ROWS // 4)
            wd_bf[pl.ds(r, WEIGHT_CAST_ROWS // 4), :] = wd_stage[ws, pl.ds(r, WEIGHT_CAST_ROWS // 4), :].astype(BF16)
            return c

        lax.fori_loop(0, D_MODEL // WEIGHT_CAST_ROWS, cast_in, 0)
        lax.fori_loop(0, EXPERT_FF // (WEIGHT_CAST_ROWS // 4), cast_out, 0)

    @pl.when(valid)
    def _tile():
        base = pl.multiple_of(slot * tile_rows, tile_rows)
        for c in range(ROW_CHUNKS):
            lo, hi = _unpack_bf16_pair(xs_ref[pl.ds(c, ROW_TILE, stride=ROW_CHUNKS), :])
            xb_ref[:, c * LANES:(c + 1) * LANES] = lo.astype(BF16)
            xb_ref[:, HALF_MODEL + c * LANES:HALF_MODEL + (c + 1) * LANES] = hi.astype(BF16)
        start_rows_inline(scatter_copy, meta_prev_ref, n_valid_prev, 1 - slot)
        xb = xb_ref[...]
        g = _dot(xb, wg_bf[...])
        u = _dot(xb, wu_bf[...])
        hb = (g * _sigmoid(g) * u).astype(BF16)
        y = _dot(hb, wd_bf[...])
        wait_rows(yo, ys_hbm, ssems.at[slot])
        for c in range(ROW_CHUNKS):
            yo[pl.ds(base + c, ROW_TILE, stride=ROW_CHUNKS), :] = _pack_bf16_pair(
                y[:, c * LANES:(c + 1) * LANES], y[:, HALF_MODEL + c * LANES:HALF_MODEL + (c + 1) * LANES])

    @pl.when(valid & (tlast_ref[i] == 1))
    def _drain():
        start_rows(scatter_copy, meta_ref, tnv_ref[i], slot)
        wait_rows(yo, ys_hbm, ssems.at[1 - slot])
        wait_rows(yo, ys_hbm, ssems.at[slot])


def _expert_call(tile_meta, tile_b, meta, xs, w_gate, w_up, w_down, tokens):
    n_tiles = tile_b.shape[0]
    meta3 = meta.reshape(n_tiles, 1, ROW_TILE)
    dump_row = tokens * TOKEN_PITCH
    meta_blk = pl.BlockSpec((None, 1, ROW_TILE), lambda i, te, tv, tf, tn, ts, tnv, tl, tb: (tb[i], 0, 0),
                            memory_space=pltpu.SMEM)
    row_blk = pl.BlockSpec((ROW_TILE * ROW_CHUNKS, LANES), lambda i, te, tv, tf, tn, ts, tnv, tl, tb: (tb[i], 0))
    meta_prev_blk = pl.BlockSpec(
        (None, 1, ROW_TILE), lambda i, te, tv, tf, tn, ts, tnv, tl, tb: (tb[jnp.maximum(i - 1, 0)], 0, 0),
        memory_space=pltpu.SMEM)
    any_spec = pl.BlockSpec(memory_space=pl.ANY)

    def body(te, tv, tf, tn, ts, tnv, tl, tb, *refs):
        _expert_kernel(te, tv, tf, tn, ts, tnv, tl, *refs, dump_row=dump_row)

    return pl.pallas_call(
        body,
        grid_spec=pltpu.PrefetchScalarGridSpec(
            num_scalar_prefetch=len(tile_meta) + 1,
            grid=(n_tiles,),
            in_specs=[meta_blk, meta_prev_blk, row_blk, any_spec, any_spec, any_spec],
            out_specs=any_spec,
            scratch_shapes=[
                pltpu.VMEM((2 * ROW_TILE * ROW_CHUNKS, LANES), U32),
                pltpu.VMEM((ROW_TILE, D_MODEL), BF16),
                pltpu.VMEM((2, D_MODEL, EXPERT_FF), F32),
                pltpu.VMEM((2, D_MODEL, EXPERT_FF), F32),
                pltpu.VMEM((2, EXPERT_FF, D_MODEL), F32),
                pltpu.VMEM((D_MODEL, EXPERT_FF), BF16),
                pltpu.VMEM((D_MODEL, EXPERT_FF), BF16),
                pltpu.VMEM((EXPERT_FF, D_MODEL), BF16),
                pltpu.SemaphoreType.DMA((2, 3)),
                pltpu.SemaphoreType.DMA((2,)),
            ],
        ),
        out_shape=jax.ShapeDtypeStruct((dump_row + ROW_TILE * ROW_CHUNKS, LANES), U32),
        compiler_params=pltpu.CompilerParams(
            dimension_semantics=("arbitrary",), vmem_limit_bytes=VMEM_LIMIT_BYTES),
        name="routed_experts",
    )(*tile_meta, tile_b, meta3, meta3, xs, w_gate, w_up, w_down)


def _final_kernel(h_ref, gffn_ref, p_ref, wt_ref, ys_ref,
                  wsg_ref, wsu_ref, wsd_ref, wpp_ref, gple_ref, wpg_ref, gfin_ref,
                  out_ref, h3_ref):
    TB = FINAL_TOKENS
    h = h_ref[...]
    h2b = _rms(h, gffn_ref[...]).astype(BF16)
    g = _dot(h2b, wsg_ref[...])
    u = _dot(h2b, wsu_ref[...])
    shared = _dot((g * _sigmoid(g) * u).astype(BF16), wsd_ref[...])
    ple = _rms(_dot(p_ref[...].astype(BF16), wpp_ref[...]), gple_ref[...])

    wt = wt_ref[...]
    for c in range(ROW_CHUNKS):
        routed_lo = routed_hi = None
        for k in range(TOP_K):
            lo, hi = _unpack_bf16_pair(ys_ref[pl.ds(k * ROW_CHUNKS + c, TB, stride=TOKEN_PITCH), :])
            wk = wt[:, k:k + 1]
            routed_lo = lo * wk if k == 0 else routed_lo + lo * wk
            routed_hi = hi * wk if k == 0 else routed_hi + hi * wk
        for routed, c0 in ((routed_lo, c * LANES), (routed_hi, HALF_MODEL + c * LANES)):
            cs = slice(c0, c0 + LANES)
            h3_ref[:, cs] = h[:, cs] + (routed + shared[:, cs])

    h3 = h3_ref[...]
    gate = _sigmoid(_dot(h3.astype(BF16), wpg_ref[...]))
    h4 = h3 + gate * ple
    out_ref[...] = _rms(h4, gfin_ref[...])


def _final_call(h, g_ffn, p, wt, ys, ws_gate, ws_up, ws_down, w_pp, g_ple, w_pg, g_final):
    TB = FINAL_TOKENS
    tokens = h.shape[0]
    const = lambda shape: pl.BlockSpec(shape, lambda i: (0,) * len(shape), pipeline_mode=pl.Buffered(1))
    tok = pl.BlockSpec((TB, D_MODEL), lambda i: (i, 0))
    return pl.pallas_call(
        _final_kernel,
        grid=(tokens // TB,),
        in_specs=[
            tok,
            const((1, D_MODEL)),
            pl.BlockSpec((TB, PLE_DIM), lambda i: (i, 0)),
            pl.BlockSpec((TB, TOP_K), lambda i: (i, 0)),
            pl.BlockSpec((TB * TOKEN_PITCH, LANES), lambda i: (i, 0)),
            const((D_MODEL, SHARED_FF)),
            const((D_MODEL, SHARED_FF)),
            const((SHARED_FF, D_MODEL)),
            const((PLE_DIM, D_MODEL)),
            const((1, D_MODEL)),
            const((D_MODEL, D_MODEL)),
            const((1, D_MODEL)),
        ],
        out_specs=tok,
        out_shape=jax.ShapeDtypeStruct((tokens, D_MODEL), F32),
        scratch_shapes=[pltpu.VMEM((TB, D_MODEL), F32)],
        compiler_params=pltpu.CompilerParams(
            dimension_semantics=("arbitrary",), vmem_limit_bytes=VMEM_LIMIT_BYTES),
        name="combine_final",
    )(h, g_ffn, p, wt, ys, ws_gate, ws_up, ws_down, w_pp, g_ple, w_pg, g_final)


def _per_step(a, tokens_per_step):
    tokens = a.shape[1]
    return a.reshape(TOP_K, tokens // tokens_per_step, tokens_per_step).transpose(1, 0, 2)


def _tile_metadata(counts, n_tiles_max):
    i32 = jnp.int32
    padded = (counts + ROW_TILE - 1) // ROW_TILE * ROW_TILE
    pend = jnp.cumsum(padded).astype(i32)
    poff = pend - padded
    n_tiles = pend[-1] // ROW_TILE
    tile = jnp.arange(n_tiles_max, dtype=i32)
    tile_v = (tile < n_tiles).astype(i32)
    tile_b = jnp.minimum(tile, jnp.maximum(n_tiles - 1, 0))
    tile_e = jnp.minimum(jnp.sum(pend[None, :] <= (tile_b * ROW_TILE)[:, None], axis=1), N_EXPERTS - 1).astype(i32)
    onehot = tile_e[:, None] == jnp.arange(N_EXPERTS, dtype=i32)[None, :]

    def lookup(table):
        return jnp.sum(jnp.where(onehot, table[None, :], 0), axis=1).astype(i32)

    tile_first = (tile_v * (tile_b * ROW_TILE == lookup(poff))).astype(i32)
    nonempty = counts > 0
    order = jnp.cumsum(nonempty.astype(i32)) - 1
    ids = jnp.where(nonempty, jnp.arange(N_EXPERTS, dtype=i32), N_EXPERTS)
    later = jnp.flip(lax.cummin(jnp.flip(ids)))
    next_e = jnp.concatenate([later[1:], jnp.full((1,), N_EXPERTS, i32)])
    next_e = jnp.where(next_e == N_EXPERTS, -1, next_e)
    tile_next = lookup(next_e)
    tile_slot = lookup(order % 2)
    tile_nvalid = jnp.clip(lookup(poff + counts) - tile_b * ROW_TILE, 0, ROW_TILE).astype(i32)
    tile_last = (tile_v * (tile == n_tiles - 1)).astype(i32)
    return poff, pend, tile_b, (tile_e, tile_v, tile_first, tile_next, tile_slot, tile_nvalid, tile_last)


def kernel(x, p, g_mix, w_in, w_pool, pool_scale, attn_sinks, rpe_table, w_out, g_ffn, w_router, router_bias,
           w_gate, w_up, w_down, ws_gate, ws_up, ws_down, w_ple_proj, g_ple, w_ple_gate, g_final):
    batch, seq, _ = x.shape
    tokens = batch * seq
    assert seq % MIX_TOKENS == 0 and tokens % DISPATCH_TOKENS == 0 and tokens % FINAL_TOKENS == 0
    n_rows = tokens * TOP_K + N_EXPERTS * ROW_TILE
    bucket = jnp.asarray(_rpe_bucket_map())
    h, h2, idx, topw, rank, cnt = _mix_call(
        x, g_mix[0][None, :], w_in[0].astype(BF16), w_pool[0].astype(BF16), pool_scale[0][None, :],
        attn_sinks[0], rpe_table, bucket, w_out[0].astype(BF16), g_ffn[0][None, :],
        w_router[0].T, router_bias[0][:, None])

    poff, pend, tile_b, tile_meta = _tile_metadata(cnt[:, 0].astype(jnp.int32), n_rows // ROW_TILE)
    experts = jnp.arange(N_EXPERTS, dtype=jnp.int32)[:, None, None]
    dest = jnp.sum(jnp.where(idx[None] == experts, poff[:, None, None], 0), axis=0) + rank
    xs, meta = _dispatch_call(_per_step(dest, DISPATCH_TOKENS), poff, pend, h2, n_rows)
    ys = _expert_call(tile_meta, tile_b, meta, xs, w_gate[0], w_up[0], w_down[0], tokens)
    out = _final_call(
        h.reshape(tokens, D_MODEL), g_ffn[0][None, :], p[0].reshape(tokens, PLE_DIM), topw.T, ys,
        ws_gate[0].astype(BF16), ws_up[0].astype(BF16), ws_down[0].astype(BF16),
        w_ple_proj[0].astype(BF16), g_ple[0][None, :], w_ple_gate[0].astype(BF16), g_final[None, :])
    return out.reshape(batch, seq, D_MODEL)
```

```python
import math

import numpy as np
import jax
import jax.numpy as jnp
from jax import lax
from jax.experimental import pallas as pl
from jax.experimental.pallas import tpu as pltpu

F32 = jnp.float32
BF16 = jnp.bfloat16
NEG_INF = float("-inf")

D_MODEL = 2048
PLE_DIM = 256
POOL_WIDTH = 1024
POOL_GROUPS = 4
POOL_GROUP_DIM = POOL_WIDTH // POOL_GROUPS
POOL_WINDOWS = (2, 4, 8, 16)
N_HEADS = 16
N_KV_HEADS = 2
HEAD_DIM = 64
HEADS_PER_KV = N_HEADS // N_KV_HEADS
ATTN_WIDTH = N_HEADS * HEAD_DIM
KV_WIDTH = N_KV_HEADS * HEAD_DIM
MIX_WIDTH = POOL_WIDTH + ATTN_WIDTH
IN_WIDTH = POOL_WIDTH + ATTN_WIDTH + 2 * KV_WIDTH
Q_OFF = POOL_WIDTH
K_OFF = POOL_WIDTH + ATTN_WIDTH
V_OFF = K_OFF + KV_WIDTH
ATTN_BLOCK = 128
WINDOW = 128
RPE_BUCKETS = 32
RPE_MAX_EXACT = RPE_BUCKETS // 2
RPE_MAX_DISTANCE = 128
N_EXPERTS = 64
TOP_K = 8
N_EXPERT_GROUPS = 8
EXPERTS_PER_GROUP = N_EXPERTS // N_EXPERT_GROUPS
TOPK_GROUPS = 4
EXPERT_FF = 512
SHARED_FF = 512
ROUTED_SCALE = 2.5
EPS = 1e-6

LANES = 128
SUBLANES = 8
VMEM_LIMIT_BYTES = 58 * 1024 * 1024

MIX_TOKENS = 256
POOL_HISTORY = 16
ROW_TILE = 256
DISPATCH_TOKENS = 512
FINAL_TOKENS = 256
WEIGHT_CAST_ROWS = 256
HALF_MODEL = D_MODEL // 2
ROW_CHUNKS = HALF_MODEL // LANES
TOKEN_PITCH = (TOP_K + 1) * ROW_CHUNKS
U32 = jnp.uint32


def _rms(x, g):
    return x * lax.rsqrt(jnp.mean(x * x, axis=-1, keepdims=True) + EPS) * g


def _sigmoid(x):
    return 1.0 / (1.0 + jnp.exp(-x))


def _pack_bf16_pair(lo, hi):
    ulo = lax.bitcast_convert_type(lo.astype(BF16).astype(F32), U32)
    uhi = lax.bitcast_convert_type(hi.astype(BF16).astype(F32), U32)
    return (ulo >> 16) | uhi


def _unpack_bf16_pair(w):
    lo = lax.bitcast_convert_type(w << 16, F32)
    hi = lax.bitcast_convert_type(w & jnp.uint32(0xFFFF0000), F32)
    return lo, hi


def _dot(a, b):
    return jnp.dot(a, b, preferred_element_type=F32)


def _dot_nt(a, b, precision=None):
    return lax.dot_general(a, b, (((1,), (1,)), ((), ())), preferred_element_type=F32, precision=precision)


def _rpe_bucket_map():
    i = np.arange(ATTN_BLOCK)[:, None]
    j = np.arange(2 * ATTN_BLOCK)[None, :]
    dist = i + ATTN_BLOCK - j
    n = np.maximum(dist, 0)
    nf = np.maximum(n, 1).astype(np.float32)
    large = RPE_MAX_EXACT + (np.log(nf / np.float32(RPE_MAX_EXACT)) / np.float32(math.log(RPE_MAX_DISTANCE / RPE_MAX_EXACT))
                             * np.float32(RPE_BUCKETS - RPE_MAX_EXACT)).astype(np.int32)
    large = np.minimum(large, RPE_BUCKETS - 1)
    bucket = np.where(n < RPE_MAX_EXACT, n, large)
    valid = (dist >= 0) & (dist < WINDOW)
    return np.where(valid, bucket, -1).astype(np.int32)


def _mix_kernel(x_ref, gmix_ref, win_ref, wpool_ref, pscale_ref, sinks_ref, rpe_ref, bucket_ref,
                wout_ref, gffn_ref, wrt_ref, rbias_ref,
                h_ref, h2_ref, idx_ref, topw_ref, rank_ref, cnt_ref,
                ubuf, kbuf, vbuf, bias_buf, ybuf, cnt_acc):
    TS = MIX_TOKENS
    b = pl.program_id(0)
    s = pl.program_id(1)

    @pl.when((b == 0) & (s == 0))
    def _init():
        bucket = bucket_ref[...]
        for h in range(N_HEADS):
            acc = jnp.full((ATTN_BLOCK, 2 * ATTN_BLOCK), NEG_INF, F32)
            for bk in range(RPE_BUCKETS):
                acc = jnp.where(bucket == bk, rpe_ref[bk, h], acc)
            bias_buf[h * ATTN_BLOCK:(h + 1) * ATTN_BLOCK, :] = acc
        cnt_acc[...] = jnp.zeros_like(cnt_acc)

    @pl.when(s == 0)
    def _reset_history():
        ubuf[0:POOL_HISTORY, :] = jnp.zeros((POOL_HISTORY, POOL_WIDTH), F32)
        kbuf[0:ATTN_BLOCK, :] = jnp.zeros((ATTN_BLOCK, KV_WIDTH), BF16)
        vbuf[0:ATTN_BLOCK, :] = jnp.zeros((ATTN_BLOCK, KV_WIDTH), BF16)

    x = x_ref[...]
    a = _rms(x, gmix_ref[...]).astype(BF16)
    z = _dot(a, win_ref[...])

    ubuf[POOL_HISTORY:POOL_HISTORY + TS, :] = z[:, 0:POOL_WIDTH]
    pos = s * TS + lax.broadcasted_iota(jnp.int32, (TS, 1), 0)
    for gi, w in enumerate(POOL_WINDOWS):
        c0, c1 = gi * POOL_GROUP_DIM, (gi + 1) * POOL_GROUP_DIM
        e = ubuf[:, c0:c1]
        shift = 1
        while shift < w:
            e = e + pltpu.roll(e, shift, axis=0)
            shift *= 2
        wsum = e[POOL_HISTORY:, :]
        count = jnp.minimum(pos + 1, w).astype(F32)
        pooled = wsum / count - z[:, c0:c1]
        yp = _dot(pooled.astype(BF16), wpool_ref[gi]) * pscale_ref[:, c0:c1]
        ybuf[:, c0:c1] = yp.astype(BF16)
    ubuf[0:POOL_HISTORY, :] = ubuf[TS:TS + POOL_HISTORY, :]

    kbuf[ATTN_BLOCK:ATTN_BLOCK + TS, :] = z[:, K_OFF:K_OFF + KV_WIDTH].astype(BF16)
    vbuf[ATTN_BLOCK:ATTN_BLOCK + TS, :] = z[:, V_OFF:V_OFF + KV_WIDTH].astype(BF16)
    lane = lax.broadcasted_iota(jnp.int32, (ATTN_BLOCK, LANES), 1)
    low_half = lane < HEAD_DIM
    high_half = jnp.logical_not(low_half)
    col = lax.broadcasted_iota(jnp.int32, (1, 2 * ATTN_BLOCK), 1)
    first_mask = jnp.where((col < ATTN_BLOCK) & (s == 0), NEG_INF, 0.0).astype(F32)
    for sb in range(TS // ATTN_BLOCK):
        r0 = sb * ATTN_BLOCK
        kband = kbuf[r0:r0 + 2 * ATTN_BLOCK, :]
        vband = vbuf[r0:r0 + 2 * ATTN_BLOCK, :]
        for p in range(N_HEADS // 2):
            qp = z[r0:r0 + ATTN_BLOCK, Q_OFF + p * LANES:Q_OFF + (p + 1) * LANES] * (HEAD_DIM ** -0.5)
            qr = pltpu.roll(qp, HEAD_DIM, axis=1)
            kvh = (2 * p) // HEADS_PER_KV
            kv_lanes = low_half if kvh == 0 else high_half
            outs = []
            for par in range(2):
                h = 2 * p + par
                qh = jnp.where(kv_lanes, qp if par == kvh else qr, 0.0).astype(BF16)
                lg = _dot_nt(qh, kband) + bias_buf[h * ATTN_BLOCK:(h + 1) * ATTN_BLOCK, :]
                if sb == 0:
                    lg = lg + first_mask
                sink = sinks_ref[h]
                m = jnp.maximum(jnp.max(lg, axis=-1, keepdims=True), sink)
                pe = jnp.exp(lg - m)
                den = jnp.sum(pe, axis=-1, keepdims=True) + jnp.exp(sink - m)
                outs.append(_dot(pe.astype(BF16), vband) / den)
            if kvh == 0:
                pair = jnp.where(low_half, outs[0], pltpu.roll(outs[1], HEAD_DIM, axis=1))
            else:
                pair = jnp.where(low_half, pltpu.roll(outs[0], HEAD_DIM, axis=1), outs[1])
            ybuf[r0:r0 + ATTN_BLOCK, POOL_WIDTH + p * LANES:POOL_WIDTH + (p + 1) * LANES] = pair.astype(BF16)
    kbuf[0:ATTN_BLOCK, :] = kbuf[TS:TS + ATTN_BLOCK, :]
    vbuf[0:ATTN_BLOCK, :] = vbuf[TS:TS + ATTN_BLOCK, :]

    h = x + _dot(ybuf[...], wout_ref[...])
    h_ref[...] = h
    h2 = _rms(h, gffn_ref[...])
    for c in range(ROW_CHUNKS):
        h2_ref[pl.ds(c, TS, stride=ROW_CHUNKS), :] = _pack_bf16_pair(
            h2[:, c * LANES:(c + 1) * LANES], h2[:, HALF_MODEL + c * LANES:HALF_MODEL + (c + 1) * LANES])

    logits = _dot_nt(wrt_ref[...], h2, precision=lax.Precision.HIGHEST)
    scores = _sigmoid(logits)
    biased = scores + rbias_ref[...]
    erow = lax.broadcasted_iota(jnp.int32, (N_EXPERTS, TS), 0)
    grow = lax.broadcasted_iota(jnp.int32, (EXPERTS_PER_GROUP, TS), 0)
    group_scores = []
    for g in range(N_EXPERT_GROUPS):
        blk = biased[g * EXPERTS_PER_GROUP:(g + 1) * EXPERTS_PER_GROUP, :]
        m1 = jnp.max(blk, axis=0, keepdims=True)
        i1 = jnp.min(jnp.where(blk == m1, grow, EXPERTS_PER_GROUP), axis=0, keepdims=True)
        m2 = jnp.max(jnp.where(grow == i1, NEG_INF, blk), axis=0, keepdims=True)
        group_scores.append(m1 + m2)
    cur = jnp.concatenate(group_scores, axis=0)
    gsel = jnp.zeros((N_EXPERT_GROUPS, TS), jnp.bool_)
    for _ in range(TOPK_GROUPS):
        m = jnp.max(cur, axis=0, keepdims=True)
        i = jnp.min(jnp.where(cur == m, grow, N_EXPERT_GROUPS), axis=0, keepdims=True)
        hit = grow == i
        gsel = jnp.logical_or(gsel, hit)
        cur = jnp.where(hit, NEG_INF, cur)
    gmask = jnp.concatenate(
        [jnp.broadcast_to(gsel[g:g + 1, :], (EXPERTS_PER_GROUP, TS)) for g in range(N_EXPERT_GROUPS)], axis=0)
    masked = jnp.where(gmask, biased, NEG_INF)
    sel = jnp.zeros((N_EXPERTS, TS), jnp.bool_)
    idxs, ws = [], []
    for _ in range(TOP_K):
        m = jnp.max(masked, axis=0, keepdims=True)
        i = jnp.min(jnp.where(masked == m, erow, N_EXPERTS), axis=0, keepdims=True)
        hit = erow == i
        idxs.append(i)
        ws.append(jnp.sum(jnp.where(hit, scores, 0.0), axis=0, keepdims=True))
        sel = jnp.logical_or(sel, hit)
        masked = jnp.where(hit, NEG_INF, masked)
    wtot = ws[0]
    for wk in ws[1:]:
        wtot = wtot + wk
    idx_ref[...] = jnp.concatenate(idxs, axis=0)
    topw_ref[...] = jnp.concatenate([wk / wtot * ROUTED_SCALE for wk in ws], axis=0)

    self32 = sel.astype(F32)
    ri = lax.broadcasted_iota(jnp.int32, (TS, TS), 0)
    ci = lax.broadcasted_iota(jnp.int32, (TS, TS), 1)
    before = (ri < ci).astype(BF16)
    running = _dot(self32.astype(BF16), before) + cnt_acc[:, 0:1]
    rank_ref[...] = jnp.concatenate(
        [jnp.sum(jnp.where(erow == i, running, 0.0), axis=0, keepdims=True) for i in idxs], axis=0).astype(jnp.int32)
    cnt_acc[...] = cnt_acc[...] + jnp.sum(self32, axis=1, keepdims=True)
    cnt_ref[...] = cnt_acc[...]


def _mix_call(x, g_mix, w_in, w_pool, pool_scale, sinks, rpe_table, bucket, w_out, g_ffn, w_rt, r_bias):
    batch, seq, _ = x.shape
    tokens = batch * seq
    TS = MIX_TOKENS
    ns = seq // TS
    const = lambda shape: pl.BlockSpec(shape, lambda b, s: (0,) * len(shape), pipeline_mode=pl.Buffered(1))
    smem = pl.BlockSpec(memory_space=pltpu.SMEM)
    tok3 = pl.BlockSpec((None, TS, D_MODEL), lambda b, s: (b, s, 0))
    lane_blk = pl.BlockSpec((TOP_K, TS), lambda b, s: (0, b * ns + s))
    return pl.pallas_call(
        _mix_kernel,
        grid=(batch, ns),
        in_specs=[
            tok3,
            const((1, D_MODEL)),
            const((D_MODEL, IN_WIDTH)),
            const((POOL_GROUPS, POOL_GROUP_DIM, POOL_GROUP_DIM)),
            const((1, POOL_WIDTH)),
            smem,
            smem,
            const((ATTN_BLOCK, 2 * ATTN_BLOCK)),
            const((MIX_WIDTH, D_MODEL)),
            const((1, D_MODEL)),
            const((N_EXPERTS, D_MODEL)),
            const((N_EXPERTS, 1)),
        ],
        out_specs=[
            tok3,
            pl.BlockSpec((TS * ROW_CHUNKS, LANES), lambda b, s: (b * ns + s, 0)),
            lane_blk,
            lane_blk,
            lane_blk,
            pl.BlockSpec((N_EXPERTS, LANES), lambda b, s: (0, 0)),
        ],
        out_shape=[
            jax.ShapeDtypeStruct((batch, seq, D_MODEL), F32),
            jax.ShapeDtypeStruct((tokens * ROW_CHUNKS, LANES), U32),
            jax.ShapeDtypeStruct((TOP_K, tokens), jnp.int32),
            jax.ShapeDtypeStruct((TOP_K, tokens), F32),
            jax.ShapeDtypeStruct((TOP_K, tokens), jnp.int32),
            jax.ShapeDtypeStruct((N_EXPERTS, LANES), F32),
        ],
        scratch_shapes=[
            pltpu.VMEM((POOL_HISTORY + TS, POOL_WIDTH), F32),
            pltpu.VMEM((ATTN_BLOCK + TS, KV_WIDTH), BF16),
            pltpu.VMEM((ATTN_BLOCK + TS, KV_WIDTH), BF16),
            pltpu.VMEM((N_HEADS * ATTN_BLOCK, 2 * ATTN_BLOCK), F32),
            pltpu.VMEM((TS, MIX_WIDTH), BF16),
            pltpu.VMEM((N_EXPERTS, LANES), F32),
        ],
        compiler_params=pltpu.CompilerParams(
            dimension_semantics=("arbitrary", "arbitrary"), vmem_limit_bytes=VMEM_LIMIT_BYTES),
        name="mix_router",
    )(x, g_mix, w_in, w_pool, pool_scale, sinks, rpe_table, bucket, w_out, g_ffn, w_rt, r_bias)


def _dispatch_kernel(dest_ref, poff_ref, pend_ref, h2_ref, xs_hbm, meta_ref, zbuf, sem, zsem):
    step = pl.program_id(0)
    base = step * DISPATCH_TOKENS

    def zero_copy(e):
        start = pl.multiple_of((pend_ref[e] - ROW_TILE) * ROW_CHUNKS, ROW_CHUNKS)
        return pltpu.make_async_copy(zbuf, xs_hbm.at[pl.ds(start, ROW_TILE * ROW_CHUNKS)], zsem)

    @pl.when(step == 0)
    def _zero_tails():
        zbuf[...] = jnp.zeros_like(zbuf)

        def start(e, c):
            @pl.when(pend_ref[e] > poff_ref[e])
            def _():
                zero_copy(e).start()
            return c

        def wait(e, c):
            @pl.when(pend_ref[e] > poff_ref[e])
            def _():
                zero_copy(e).wait()
            return c

        lax.fori_loop(0, N_EXPERTS, start, 0)
        lax.fori_loop(0, N_EXPERTS, wait, 0)

    def row_copy(t, k):
        src = pl.multiple_of(t * ROW_CHUNKS, ROW_CHUNKS)
        dest = pl.multiple_of(dest_ref[k, t] * ROW_CHUNKS, ROW_CHUNKS)
        return pltpu.make_async_copy(h2_ref.at[pl.ds(src, ROW_CHUNKS)], xs_hbm.at[pl.ds(dest, ROW_CHUNKS)], sem)

    def start_tok(t, c):
        for k in range(TOP_K):
            row_copy(t, k).start(priority=k % 2)
            meta_ref[dest_ref[k, t]] = (base + t) * TOP_K + k
        return c

    def wait_tok(t, c):
        for k in range(TOP_K):
            row_copy(t, k).wait()
        return c

    lax.fori_loop(0, DISPATCH_TOKENS, start_tok, 0, unroll=4)
    lax.fori_loop(0, DISPATCH_TOKENS, wait_tok, 0)


def _dispatch_call(dest3, poff, pend, h2, n_rows):
    smem_blk = pl.BlockSpec((None, TOP_K, DISPATCH_TOKENS), lambda i: (i, 0, 0), memory_space=pltpu.SMEM)
    smem = pl.BlockSpec(memory_space=pltpu.SMEM)
    return pl.pallas_call(
        _dispatch_kernel,
        grid=(dest3.shape[0],),
        in_specs=[smem_blk, smem, smem,
                  pl.BlockSpec((DISPATCH_TOKENS * ROW_CHUNKS, LANES), lambda i: (i, 0))],
        out_specs=[pl.BlockSpec(memory_space=pl.ANY), smem],
        out_shape=[jax.ShapeDtypeStruct((n_rows * ROW_CHUNKS, LANES), U32),
                   jax.ShapeDtypeStruct((n_rows,), jnp.int32)],
        scratch_shapes=[
            pltpu.VMEM((ROW_TILE * ROW_CHUNKS, LANES), U32),
            pltpu.SemaphoreType.DMA,
            pltpu.SemaphoreType.DMA,
        ],
        compiler_params=pltpu.CompilerParams(dimension_semantics=("arbitrary",)),
        name="dispatch_rows",
    )(dest3, poff, pend, h2)


def _expert_kernel(te_ref, tv_ref, tfirst_ref, tnext_ref, tslot_ref, tnv_ref, tlast_ref,
                   meta_ref, meta_prev_ref, xs_ref, wg_hbm, wu_hbm, wd_hbm, ys_hbm,
                   yo, xb_ref, wg_stage, wu_stage, wd_stage, wg_bf, wu_bf, wd_bf, wsems, ssems,
                   *, dump_row):
    i = pl.program_id(0)
    slot = lax.rem(i, 2)
    tile_rows = ROW_TILE * ROW_CHUNKS

    def scatter_copy_of(m, n_valid, s, r):
        dst = jnp.where(r < n_valid, (m + (m >> 3)) * ROW_CHUNKS, dump_row + r * ROW_CHUNKS)
        src = s * tile_rows + r * ROW_CHUNKS
        return pltpu.make_async_copy(yo.at[pl.ds(pl.multiple_of(src, ROW_CHUNKS), ROW_CHUNKS)],
                                     ys_hbm.at[pl.ds(pl.multiple_of(dst, ROW_CHUNKS), ROW_CHUNKS)], ssems.at[s])

    def scatter_copy(m_ref, n_valid, s, r):
        return scatter_copy_of(m_ref[0, r], n_valid, s, r)

    def start_rows(make, m_ref, n_valid, s):
        def body(r, c):
            make(m_ref, n_valid, s, r).start(priority=0)
            make(m_ref, n_valid, s, r + ROW_TILE // 2).start(priority=1)
            return c
        lax.fori_loop(0, ROW_TILE // 2, body, 0, unroll=8)

    def start_scatter_inline(m_ref, n_valid, s, zero):
        z = zero
        for r in range(ROW_TILE):
            m = m_ref[0, r + z]
            scatter_copy_of(m, n_valid, s, r).start(priority=r % 2)
            z = jnp.where(r < n_valid, m >> 31, 0)

    def wait_rows(src_ref, dst_ref, sem):
        for _ in range(ROW_TILE):
            pltpu.make_async_copy(src_ref.at[pl.ds(0, ROW_CHUNKS)], dst_ref.at[pl.ds(0, ROW_CHUNKS)], sem).wait()

    def weight_copies(e, s):
        return (pltpu.make_async_copy(wg_hbm.at[e], wg_stage.at[s], wsems.at[s, 0]),
                pltpu.make_async_copy(wu_hbm.at[e], wu_stage.at[s], wsems.at[s, 1]),
                pltpu.make_async_copy(wd_hbm.at[e], wd_stage.at[s], wsems.at[s, 2]))

    valid = tv_ref[i] == 1

    @pl.when((i == 0) & valid)
    def _prologue():
        for cp in weight_copies(te_ref[0], tslot_ref[0]):
            cp.start()
        yo[...] = jnp.zeros_like(yo)
        start_rows(scatter_copy, meta_ref, 0, 0)

    n_valid_prev = jnp.where(i >= 1, tnv_ref[jnp.maximum(i - 1, 0)], 0)

    @pl.when(valid & (tfirst_ref[i] == 1))
    def _new_expert():
        ws = tslot_ref[i]
        for cp in weight_copies(te_ref[i], ws):
            cp.wait()

        @pl.when(tnext_ref[i] >= 0)
        def _():
            for cp in weight_copies(tnext_ref[i], 1 - ws):
                cp.start()

        def cast_in(j, c):
            r = pl.multiple_of(j * WEIGHT_CAST_ROWS, WEIGHT_CAST_ROWS)
            wg_bf[pl.ds(r, WEIGHT_CAST_ROWS), :] = wg_stage[ws, pl.ds(r, WEIGHT_CAST_ROWS), :].astype(BF16)
            wu_bf[pl.ds(r, WEIGHT_CAST_ROWS), :] = wu_stage[ws, pl.ds(r, WEIGHT_CAST_ROWS), :].astype(BF16)
            return c

        def cast_out(j, c):
            r = pl.multiple_of(j * (WEIGHT_CAST_ROWS // 4), WEIGHT_CAST_ROWS // 4)
            wd_bf[pl.ds(r, WEIGHT_CAST_ROWS // 4), :] = wd_stage[ws, pl.ds(r, WEIGHT_CAST_ROWS // 4), :].astype(BF16)
            return c

        lax.fori_loop(0, D_MODEL // WEIGHT_CAST_ROWS, cast_in, 0)
        lax.fori_loop(0, EXPERT_FF // (WEIGHT_CAST_ROWS // 4), cast_out, 0)

    @pl.when(valid)
    def _tile():
        base = pl.multiple_of(slot * tile_rows, tile_rows)
        for c in range(ROW_CHUNKS):
            lo, hi = _unpack_bf16_pair(xs_ref[pl.ds(c, ROW_TILE, stride=ROW_CHUNKS), :])
            xb_ref[:, c * LANES:(c + 1) * LANES] = lo.astype(BF16)
            xb_ref[:, HALF_MODEL + c * LANES:HALF_MODEL + (c + 1) * LANES] = hi.astype(BF16)
        start_scatter_inline(meta_prev_ref, n_valid_prev, 1 - slot, tv_ref[i] - 1)
        xb = xb_ref[...]
        g = _dot(xb, wg_bf[...])
        u = _dot(xb, wu_bf[...])
        hb = (g * _sigmoid(g) * u).astype(BF16)
        y = _dot(hb, wd_bf[...])
        wait_rows(yo, ys_hbm, ssems.at[slot])
        for c in range(ROW_CHUNKS):
            yo[pl.ds(base + c, ROW_TILE, stride=ROW_CHUNKS), :] = _pack_bf16_pair(
                y[:, c * LANES:(c + 1) * LANES], y[:, HALF_MODEL + c * LANES:HALF_MODEL + (c + 1) * LANES])

    @pl.when(valid & (tlast_ref[i] == 1))
    def _drain():
        start_rows(scatter_copy, meta_ref, tnv_ref[i], slot)
        wait_rows(yo, ys_hbm, ssems.at[1 - slot])
        wait_rows(yo, ys_hbm, ssems.at[slot])


def _expert_call(tile_meta, tile_b, meta, xs, w_gate, w_up, w_down, tokens):
    n_tiles = tile_b.shape[0]
    meta3 = meta.reshape(n_tiles, 1, ROW_TILE)
    dump_row = tokens * TOKEN_PITCH
    meta_blk = pl.BlockSpec((None, 1, ROW_TILE), lambda i, te, tv, tf, tn, ts, tnv, tl, tb: (tb[i], 0, 0),
                            memory_space=pltpu.SMEM)
    row_blk = pl.BlockSpec((ROW_TILE * ROW_CHUNKS, LANES), lambda i, te, tv, tf, tn, ts, tnv, tl, tb: (tb[i], 0))
    meta_prev_blk = pl.BlockSpec(
        (None, 1, ROW_TILE), lambda i, te, tv, tf, tn, ts, tnv, tl, tb: (tb[jnp.maximum(i - 1, 0)], 0, 0),
        memory_space=pltpu.SMEM)
    any_spec = pl.BlockSpec(memory_space=pl.ANY)

    def body(te, tv, tf, tn, ts, tnv, tl, tb, *refs):
        _expert_kernel(te, tv, tf, tn, ts, tnv, tl, *refs, dump_row=dump_row)

    return pl.pallas_call(
        body,
        grid_spec=pltpu.PrefetchScalarGridSpec(
            num_scalar_prefetch=len(tile_meta) + 1,
            grid=(n_tiles,),
            in_specs=[meta_blk, meta_prev_blk, row_blk, any_spec, any_spec, any_spec],
            out_specs=any_spec,
            scratch_shapes=[
                pltpu.VMEM((2 * ROW_TILE * ROW_CHUNKS, LANES), U32),
                pltpu.VMEM((ROW_TILE, D_MODEL), BF16),
                pltpu.VMEM((2, D_MODEL, EXPERT_FF), F32),
                pltpu.VMEM((2, D_MODEL, EXPERT_FF), F32),
                pltpu.VMEM((2, EXPERT_FF, D_MODEL), F32),
                pltpu.VMEM((D_MODEL, EXPERT_FF), BF16),
                pltpu.VMEM((D_MODEL, EXPERT_FF), BF16),
                pltpu.VMEM((EXPERT_FF, D_MODEL), BF16),
                pltpu.SemaphoreType.DMA((2, 3)),
                pltpu.SemaphoreType.DMA((2,)),
            ],
        ),
        out_shape=jax.ShapeDtypeStruct((dump_row + ROW_TILE * ROW_CHUNKS, LANES), U32),
        compiler_params=pltpu.CompilerParams(
            dimension_semantics=("arbitrary",), vmem_limit_bytes=VMEM_LIMIT_BYTES),
        name="routed_experts",
    )(*tile_meta, tile_b, meta3, meta3, xs, w_gate, w_up, w_down)


def _final_kernel(h_ref, gffn_ref, p_ref, wt_ref, ys_ref,
                  wsg_ref, wsu_ref, wsd_ref, wpp_ref, gple_ref, wpg_ref, gfin_ref,
                  out_ref, h3_ref):
    TB = FINAL_TOKENS
    h = h_ref[...]
    h2b = _rms(h, gffn_ref[...]).astype(BF16)
    g = _dot(h2b, wsg_ref[...])
    u = _dot(h2b, wsu_ref[...])
    shared = _dot((g * _sigmoid(g) * u).astype(BF16), wsd_ref[...])
    ple = _rms(_dot(p_ref[...].astype(BF16), wpp_ref[...]), gple_ref[...])

    wt = wt_ref[...]
    for c in range(ROW_CHUNKS):
        routed_lo = routed_hi = None
        for k in range(TOP_K):
            lo, hi = _unpack_bf16_pair(ys_ref[pl.ds(k * ROW_CHUNKS + c, TB, stride=TOKEN_PITCH), :])
            wk = wt[:, k:k + 1]
            routed_lo = lo * wk if k == 0 else routed_lo + lo * wk
            routed_hi = hi * wk if k == 0 else routed_hi + hi * wk
        for routed, c0 in ((routed_lo, c * LANES), (routed_hi, HALF_MODEL + c * LANES)):
            cs = slice(c0, c0 + LANES)
            h3_ref[:, cs] = h[:, cs] + (routed + shared[:, cs])

    h3 = h3_ref[...]
    gate = _sigmoid(_dot(h3.astype(BF16), wpg_ref[...]))
    h4 = h3 + gate * ple
    out_ref[...] = _rms(h4, gfin_ref[...])


def _final_call(h, g_ffn, p, wt, ys, ws_gate, ws_up, ws_down, w_pp, g_ple, w_pg, g_final):
    TB = FINAL_TOKENS
    tokens = h.shape[0]
    const = lambda shape: pl.BlockSpec(shape, lambda i: (0,) * len(shape), pipeline_mode=pl.Buffered(1))
    tok = pl.BlockSpec((TB, D_MODEL), lambda i: (i, 0))
    return pl.pallas_call(
        _final_kernel,
        grid=(tokens // TB,),
        in_specs=[
            tok,
            const((1, D_MODEL)),
            pl.BlockSpec((TB, PLE_DIM), lambda i: (i, 0)),
            pl.BlockSpec((TB, TOP_K), lambda i: (i, 0)),
            pl.BlockSpec((TB * TOKEN_PITCH, LANES), lambda i: (i, 0)),
            const((D_MODEL, SHARED_FF)),
            const((D_MODEL, SHARED_FF)),
            const((SHARED_FF, D_MODEL)),
            const((PLE_DIM, D_MODEL)),
            const((1, D_MODEL)),
            const((D_MODEL, D_MODEL)),
            const((1, D_MODEL)),
        ],
        out_specs=tok,
        out_shape=jax.ShapeDtypeStruct((tokens, D_MODEL), F32),
        scratch_shapes=[pltpu.VMEM((TB, D_MODEL), F32)],
        compiler_params=pltpu.CompilerParams(
            dimension_semantics=("arbitrary",), vmem_limit_bytes=VMEM_LIMIT_BYTES),
        name="combine_final",
    )(h, g_ffn, p, wt, ys, ws_gate, ws_up, ws_down, w_pp, g_ple, w_pg, g_final)


def _per_step(a, tokens_per_step):
    tokens = a.shape[1]
    return a.reshape(TOP_K, tokens // tokens_per_step, tokens_per_step).transpose(1, 0, 2)


def _tile_metadata(counts, n_tiles_max):
    i32 = jnp.int32
    padded = (counts + ROW_TILE - 1) // ROW_TILE * ROW_TILE
    pend = jnp.cumsum(padded).astype(i32)
    poff = pend - padded
    n_tiles = pend[-1] // ROW_TILE
    tile = jnp.arange(n_tiles_max, dtype=i32)
    tile_v = (tile < n_tiles).astype(i32)
    tile_b = jnp.minimum(tile, jnp.maximum(n_tiles - 1, 0))
    tile_e = jnp.minimum(jnp.sum(pend[None, :] <= (tile_b * ROW_TILE)[:, None], axis=1), N_EXPERTS - 1).astype(i32)
    onehot = tile_e[:, None] == jnp.arange(N_EXPERTS, dtype=i32)[None, :]

    def lookup(table):
        return jnp.sum(jnp.where(onehot, table[None, :], 0), axis=1).astype(i32)

    tile_first = (tile_v * (tile_b * ROW_TILE == lookup(poff))).astype(i32)
    nonempty = counts > 0
    order = jnp.cumsum(nonempty.astype(i32)) - 1
    ids = jnp.where(nonempty, jnp.arange(N_EXPERTS, dtype=i32), N_EXPERTS)
    later = jnp.flip(lax.cummin(jnp.flip(ids)))
    next_e = jnp.concatenate([later[1:], jnp.full((1,), N_EXPERTS, i32)])
    next_e = jnp.where(next_e == N_EXPERTS, -1, next_e)
    tile_next = lookup(next_e)
    tile_slot = lookup(order % 2)
    tile_nvalid = jnp.clip(lookup(poff + counts) - tile_b * ROW_TILE, 0, ROW_TILE).astype(i32)
    tile_last = (tile_v * (tile == n_tiles - 1)).astype(i32)
    return poff, pend, tile_b, (tile_e, tile_v, tile_first, tile_next, tile_slot, tile_nvalid, tile_last)


def kernel(x, p, g_mix, w_in, w_pool, pool_scale, attn_sinks, rpe_table, w_out, g_ffn, w_router, router_bias,
           w_gate, w_up, w_down, ws_gate, ws_up, ws_down, w_ple_proj, g_ple, w_ple_gate, g_final):
    batch, seq, _ = x.shape
    tokens = batch * seq
    assert seq % MIX_TOKENS == 0 and tokens % DISPATCH_TOKENS == 0 and tokens % FINAL_TOKENS == 0
    n_rows = tokens * TOP_K + N_EXPERTS * ROW_TILE
    bucket = jnp.asarray(_rpe_bucket_map())
    h, h2, idx, topw, rank, cnt = _mix_call(
        x, g_mix[0][None, :], w_in[0].astype(BF16), w_pool[0].astype(BF16), pool_scale[0][None, :],
        attn_sinks[0], rpe_table, bucket, w_out[0].astype(BF16), g_ffn[0][None, :],
        w_router[0].T, router_bias[0][:, None])

    poff, pend, tile_b, tile_meta = _tile_metadata(cnt[:, 0].astype(jnp.int32), n_rows // ROW_TILE)
    experts = jnp.arange(N_EXPERTS, dtype=jnp.int32)[:, None, None]
    dest = jnp.sum(jnp.where(idx[None] == experts, poff[:, None, None], 0), axis=0) + rank
    xs, meta = _dispatch_call(_per_step(dest, DISPATCH_TOKENS), poff, pend, h2, n_rows)
    ys = _expert_call(tile_meta, tile_b, meta, xs, w_gate[0], w_up[0], w_down[0], tokens)
    out = _final_call(
        h.reshape(tokens, D_MODEL), g_ffn[0][None, :], p[0].reshape(tokens, PLE_DIM), topw.T, ys,
        ws_gate[0].astype(BF16), ws_up[0].astype(BF16), ws_down[0].astype(BF16),
        w_ple_proj[0].astype(BF16), g_ple[0][None, :], w_ple_gate[0].astype(BF16), g_final[None, :])
    return out.reshape(batch, seq, D_MODEL)
```

```python
import math

import numpy as np
import jax
import jax.numpy as jnp
from jax import lax
from jax.experimental import pallas as pl
from jax.experimental.pallas import tpu as pltpu

F32 = jnp.float32
BF16 = jnp.bfloat16
NEG_INF = float("-inf")

D_MODEL = 2048
PLE_DIM = 256
POOL_WIDTH = 1024
POOL_GROUPS = 4
POOL_GROUP_DIM = POOL_WIDTH // POOL_GROUPS
POOL_WINDOWS = (2, 4, 8, 16)
N_HEADS = 16
N_KV_HEADS = 2
HEAD_DIM = 64
HEADS_PER_KV = N_HEADS // N_KV_HEADS
ATTN_WIDTH = N_HEADS * HEAD_DIM
KV_WIDTH = N_KV_HEADS * HEAD_DIM
MIX_WIDTH = POOL_WIDTH + ATTN_WIDTH
IN_WIDTH = POOL_WIDTH + ATTN_WIDTH + 2 * KV_WIDTH
Q_OFF = POOL_WIDTH
K_OFF = POOL_WIDTH + ATTN_WIDTH
V_OFF = K_OFF + KV_WIDTH
ATTN_BLOCK = 128
WINDOW = 128
RPE_BUCKETS = 32
RPE_MAX_EXACT = RPE_BUCKETS // 2
RPE_MAX_DISTANCE = 128
N_EXPERTS = 64
TOP_K = 8
N_EXPERT_GROUPS = 8
EXPERTS_PER_GROUP = N_EXPERTS // N_EXPERT_GROUPS
TOPK_GROUPS = 4
EXPERT_FF = 512
SHARED_FF = 512
ROUTED_SCALE = 2.5
EPS = 1e-6

LANES = 128
SUBLANES = 8
VMEM_LIMIT_BYTES = 58 * 1024 * 1024

MIX_TOKENS = 256
POOL_HISTORY = 16
ROW_TILE = 256
DISPATCH_TOKENS = 512
FINAL_TOKENS = 256
WEIGHT_CAST_ROWS = 256
WEIGHT_DMA_PRIORITY = 1
HALF_MODEL = D_MODEL // 2
ROW_CHUNKS = HALF_MODEL // LANES
TOKEN_PITCH = (TOP_K + 1) * ROW_CHUNKS
U32 = jnp.uint32


def _rms(x, g):
    return x * lax.rsqrt(jnp.mean(x * x, axis=-1, keepdims=True) + EPS) * g


def _sigmoid(x):
    return 1.0 / (1.0 + jnp.exp(-x))


def _pack_bf16_pair(lo, hi):
    ulo = lax.bitcast_convert_type(lo.astype(BF16).astype(F32), U32)
    uhi = lax.bitcast_convert_type(hi.astype(BF16).astype(F32), U32)
    return (ulo >> 16) | uhi


def _unpack_bf16_pair(w):
    lo = lax.bitcast_convert_type(w << 16, F32)
    hi = lax.bitcast_convert_type(w & jnp.uint32(0xFFFF0000), F32)
    return lo, hi


def _dot(a, b):
    return jnp.dot(a, b, preferred_element_type=F32)


def _dot_nt(a, b, precision=None):
    return lax.dot_general(a, b, (((1,), (1,)), ((), ())), preferred_element_type=F32, precision=precision)


def _rpe_bucket_map():
    i = np.arange(ATTN_BLOCK)[:, None]
    j = np.arange(2 * ATTN_BLOCK)[None, :]
    dist = i + ATTN_BLOCK - j
    n = np.maximum(dist, 0)
    nf = np.maximum(n, 1).astype(np.float32)
    large = RPE_MAX_EXACT + (np.log(nf / np.float32(RPE_MAX_EXACT)) / np.float32(math.log(RPE_MAX_DISTANCE / RPE_MAX_EXACT))
                             * np.float32(RPE_BUCKETS - RPE_MAX_EXACT)).astype(np.int32)
    large = np.minimum(large, RPE_BUCKETS - 1)
    bucket = np.where(n < RPE_MAX_EXACT, n, large)
    valid = (dist >= 0) & (dist < WINDOW)
    return np.where(valid, bucket, -1).astype(np.int32)


def _mix_kernel(x_ref, gmix_ref, win_ref, wpool_ref, pscale_ref, sinks_ref, rpe_ref, bucket_ref,
                wout_ref, gffn_ref, wrt_ref, rbias_ref,
                h_ref, h2_ref, idx_ref, topw_ref, rank_ref, cnt_ref,
                ubuf, kbuf, vbuf, bias_buf, ybuf, cnt_acc):
    TS = MIX_TOKENS
    b = pl.program_id(0)
    s = pl.program_id(1)

    @pl.when((b == 0) & (s == 0))
    def _init():
        bucket = bucket_ref[...]
        for h in range(N_HEADS):
            acc = jnp.full((ATTN_BLOCK, 2 * ATTN_BLOCK), NEG_INF, F32)
            for bk in range(RPE_BUCKETS):
                acc = jnp.where(bucket == bk, rpe_ref[bk, h], acc)
            bias_buf[h * ATTN_BLOCK:(h + 1) * ATTN_BLOCK, :] = acc
        cnt_acc[...] = jnp.zeros_like(cnt_acc)

    @pl.when(s == 0)
    def _reset_history():
        ubuf[0:POOL_HISTORY, :] = jnp.zeros((POOL_HISTORY, POOL_WIDTH), F32)
        kbuf[0:ATTN_BLOCK, :] = jnp.zeros((ATTN_BLOCK, KV_WIDTH), BF16)
        vbuf[0:ATTN_BLOCK, :] = jnp.zeros((ATTN_BLOCK, KV_WIDTH), BF16)

    x = x_ref[...]
    a = _rms(x, gmix_ref[...]).astype(BF16)
    z = _dot(a, win_ref[...])

    ubuf[POOL_HISTORY:POOL_HISTORY + TS, :] = z[:, 0:POOL_WIDTH]
    pos = s * TS + lax.broadcasted_iota(jnp.int32, (TS, 1), 0)
    for gi, w in enumerate(POOL_WINDOWS):
        c0, c1 = gi * POOL_GROUP_DIM, (gi + 1) * POOL_GROUP_DIM
        e = ubuf[:, c0:c1]
        shift = 1
        while shift < w:
            e = e + pltpu.roll(e, shift, axis=0)
            shift *= 2
        wsum = e[POOL_HISTORY:, :]
        count = jnp.minimum(pos + 1, w).astype(F32)
        pooled = wsum / count - z[:, c0:c1]
        yp = _dot(pooled.astype(BF16), wpool_ref[gi]) * pscale_ref[:, c0:c1]
        ybuf[:, c0:c1] = yp.astype(BF16)
    ubuf[0:POOL_HISTORY, :] = ubuf[TS:TS + POOL_HISTORY, :]

    kbuf[ATTN_BLOCK:ATTN_BLOCK + TS, :] = z[:, K_OFF:K_OFF + KV_WIDTH].astype(BF16)
    vbuf[ATTN_BLOCK:ATTN_BLOCK + TS, :] = z[:, V_OFF:V_OFF + KV_WIDTH].astype(BF16)
    lane = lax.broadcasted_iota(jnp.int32, (ATTN_BLOCK, LANES), 1)
    low_half = lane < HEAD_DIM
    high_half = jnp.logical_not(low_half)
    col = lax.broadcasted_iota(jnp.int32, (1, 2 * ATTN_BLOCK), 1)
    first_mask = jnp.where((col < ATTN_BLOCK) & (s == 0), NEG_INF, 0.0).astype(F32)
    for sb in range(TS // ATTN_BLOCK):
        r0 = sb * ATTN_BLOCK
        kband = kbuf[r0:r0 + 2 * ATTN_BLOCK, :]
        vband = vbuf[r0:r0 + 2 * ATTN_BLOCK, :]
        for p in range(N_HEADS // 2):
            qp = z[r0:r0 + ATTN_BLOCK, Q_OFF + p * LANES:Q_OFF + (p + 1) * LANES] * (HEAD_DIM ** -0.5)
            qr = pltpu.roll(qp, HEAD_DIM, axis=1)
            kvh = (2 * p) // HEADS_PER_KV
            kv_lanes = low_half if kvh == 0 else high_half
            outs = []
            for par in range(2):
                h = 2 * p + par
                qh = jnp.where(kv_lanes, qp if par == kvh else qr, 0.0).astype(BF16)
                lg = _dot_nt(qh, kband) + bias_buf[h * ATTN_BLOCK:(h + 1) * ATTN_BLOCK, :]
                if sb == 0:
                    lg = lg + first_mask
                sink = sinks_ref[h]
                m = jnp.maximum(jnp.max(lg, axis=-1, keepdims=True), sink)
                pe = jnp.exp(lg - m)
                den = jnp.sum(pe, axis=-1, keepdims=True) + jnp.exp(sink - m)
                outs.append(_dot(pe.astype(BF16), vband) / den)
            if kvh == 0:
                pair = jnp.where(low_half, outs[0], pltpu.roll(outs[1], HEAD_DIM, axis=1))
            else:
                pair = jnp.where(low_half, pltpu.roll(outs[0], HEAD_DIM, axis=1), outs[1])
            ybuf[r0:r0 + ATTN_BLOCK, POOL_WIDTH + p * LANES:POOL_WIDTH + (p + 1) * LANES] = pair.astype(BF16)
    kbuf[0:ATTN_BLOCK, :] = kbuf[TS:TS + ATTN_BLOCK, :]
    vbuf[0:ATTN_BLOCK, :] = vbuf[TS:TS + ATTN_BLOCK, :]

    h = x + _dot(ybuf[...], wout_ref[...])
    h_ref[...] = h
    h2 = _rms(h, gffn_ref[...])
    for c in range(ROW_CHUNKS):
        h2_ref[pl.ds(c, TS, stride=ROW_CHUNKS), :] = _pack_bf16_pair(
            h2[:, c * LANES:(c + 1) * LANES], h2[:, HALF_MODEL + c * LANES:HALF_MODEL + (c + 1) * LANES])

    logits = _dot_nt(wrt_ref[...], h2, precision=lax.Precision.HIGHEST)
    scores = _sigmoid(logits)
    biased = scores + rbias_ref[...]
    erow = lax.broadcasted_iota(jnp.int32, (N_EXPERTS, TS), 0)
    grow = lax.broadcasted_iota(jnp.int32, (EXPERTS_PER_GROUP, TS), 0)
    group_scores = []
    for g in range(N_EXPERT_GROUPS):
        blk = biased[g * EXPERTS_PER_GROUP:(g + 1) * EXPERTS_PER_GROUP, :]
        m1 = jnp.max(blk, axis=0, keepdims=True)
        i1 = jnp.min(jnp.where(blk == m1, grow, EXPERTS_PER_GROUP), axis=0, keepdims=True)
        m2 = jnp.max(jnp.where(grow == i1, NEG_INF, blk), axis=0, keepdims=True)
        group_scores.append(m1 + m2)
    cur = jnp.concatenate(group_scores, axis=0)
    gsel = jnp.zeros((N_EXPERT_GROUPS, TS), jnp.bool_)
    for _ in range(TOPK_GROUPS):
        m = jnp.max(cur, axis=0, keepdims=True)
        i = jnp.min(jnp.where(cur == m, grow, N_EXPERT_GROUPS), axis=0, keepdims=True)
        hit = grow == i
        gsel = jnp.logical_or(gsel, hit)
        cur = jnp.where(hit, NEG_INF, cur)
    gmask = jnp.concatenate(
        [jnp.broadcast_to(gsel[g:g + 1, :], (EXPERTS_PER_GROUP, TS)) for g in range(N_EXPERT_GROUPS)], axis=0)
    masked = jnp.where(gmask, biased, NEG_INF)
    sel = jnp.zeros((N_EXPERTS, TS), jnp.bool_)
    idxs, ws = [], []
    for _ in range(TOP_K):
        m = jnp.max(masked, axis=0, keepdims=True)
        i = jnp.min(jnp.where(masked == m, erow, N_EXPERTS), axis=0, keepdims=True)
        hit = erow == i
        idxs.append(i)
        ws.append(jnp.sum(jnp.where(hit, scores, 0.0), axis=0, keepdims=True))
        sel = jnp.logical_or(sel, hit)
        masked = jnp.where(hit, NEG_INF, masked)
    wtot = ws[0]
    for wk in ws[1:]:
        wtot = wtot + wk
    idx_ref[...] = jnp.concatenate(idxs, axis=0)
    topw_ref[...] = jnp.concatenate([wk / wtot * ROUTED_SCALE for wk in ws], axis=0)

    self32 = sel.astype(F32)
    ri = lax.broadcasted_iota(jnp.int32, (TS, TS), 0)
    ci = lax.broadcasted_iota(jnp.int32, (TS, TS), 1)
    before = (ri < ci).astype(BF16)
    running = _dot(self32.astype(BF16), before) + cnt_acc[:, 0:1]
    rank_ref[...] = jnp.concatenate(
        [jnp.sum(jnp.where(erow == i, running, 0.0), axis=0, keepdims=True) for i in idxs], axis=0).astype(jnp.int32)
    cnt_acc[...] = cnt_acc[...] + jnp.sum(self32, axis=1, keepdims=True)
    cnt_ref[...] = cnt_acc[...]


def _mix_call(x, g_mix, w_in, w_pool, pool_scale, sinks, rpe_table, bucket, w_out, g_ffn, w_rt, r_bias):
    batch, seq, _ = x.shape
    tokens = batch * seq
    TS = MIX_TOKENS
    ns = seq // TS
    const = lambda shape: pl.BlockSpec(shape, lambda b, s: (0,) * len(shape), pipeline_mode=pl.Buffered(1))
    smem = pl.BlockSpec(memory_space=pltpu.SMEM)
    tok3 = pl.BlockSpec((None, TS, D_MODEL), lambda b, s: (b, s, 0))
    lane_blk = pl.BlockSpec((TOP_K, TS), lambda b, s: (0, b * ns + s))
    return pl.pallas_call(
        _mix_kernel,
        grid=(batch, ns),
        in_specs=[
            tok3,
            const((1, D_MODEL)),
            const((D_MODEL, IN_WIDTH)),
            const((POOL_GROUPS, POOL_GROUP_DIM, POOL_GROUP_DIM)),
            const((1, POOL_WIDTH)),
            smem,
            smem,
            const((ATTN_BLOCK, 2 * ATTN_BLOCK)),
            const((MIX_WIDTH, D_MODEL)),
            const((1, D_MODEL)),
            const((N_EXPERTS, D_MODEL)),
            const((N_EXPERTS, 1)),
        ],
        out_specs=[
            tok3,
            pl.BlockSpec((TS * ROW_CHUNKS, LANES), lambda b, s: (b * ns + s, 0)),
            lane_blk,
            lane_blk,
            lane_blk,
            pl.BlockSpec((N_EXPERTS, LANES), lambda b, s: (0, 0)),
        ],
        out_shape=[
            jax.ShapeDtypeStruct((batch, seq, D_MODEL), F32),
            jax.ShapeDtypeStruct((tokens * ROW_CHUNKS, LANES), U32),
            jax.ShapeDtypeStruct((TOP_K, tokens), jnp.int32),
            jax.ShapeDtypeStruct((TOP_K, tokens), F32),
            jax.ShapeDtypeStruct((TOP_K, tokens), jnp.int32),
            jax.ShapeDtypeStruct((N_EXPERTS, LANES), F32),
        ],
        scratch_shapes=[
            pltpu.VMEM((POOL_HISTORY + TS, POOL_WIDTH), F32),
            pltpu.VMEM((ATTN_BLOCK + TS, KV_WIDTH), BF16),
            pltpu.VMEM((ATTN_BLOCK + TS, KV_WIDTH), BF16),
            pltpu.VMEM((N_HEADS * ATTN_BLOCK, 2 * ATTN_BLOCK), F32),
            pltpu.VMEM((TS, MIX_WIDTH), BF16),
            pltpu.VMEM((N_EXPERTS, LANES), F32),
        ],
        compiler_params=pltpu.CompilerParams(
            dimension_semantics=("arbitrary", "arbitrary"), vmem_limit_bytes=VMEM_LIMIT_BYTES),
        name="mix_router",
    )(x, g_mix, w_in, w_pool, pool_scale, sinks, rpe_table, bucket, w_out, g_ffn, w_rt, r_bias)


def _dispatch_kernel(dest_ref, poff_ref, pend_ref, h2_ref, xs_hbm, meta_ref, zbuf, sem, zsem):
    step = pl.program_id(0)
    base = step * DISPATCH_TOKENS

    def zero_copy(e):
        start = pl.multiple_of((pend_ref[e] - ROW_TILE) * ROW_CHUNKS, ROW_CHUNKS)
        return pltpu.make_async_copy(zbuf, xs_hbm.at[pl.ds(start, ROW_TILE * ROW_CHUNKS)], zsem)

    @pl.when(step == 0)
    def _zero_tails():
        zbuf[...] = jnp.zeros_like(zbuf)

        def start(e, c):
            @pl.when(pend_ref[e] > poff_ref[e])
            def _():
                zero_copy(e).start()
            return c

        def wait(e, c):
            @pl.when(pend_ref[e] > poff_ref[e])
            def _():
                zero_copy(e).wait()
            return c

        lax.fori_loop(0, N_EXPERTS, start, 0)
        lax.fori_loop(0, N_EXPERTS, wait, 0)

    def row_copy(t, k):
        src = pl.multiple_of(t * ROW_CHUNKS, ROW_CHUNKS)
        dest = pl.multiple_of(dest_ref[k, t] * ROW_CHUNKS, ROW_CHUNKS)
        return pltpu.make_async_copy(h2_ref.at[pl.ds(src, ROW_CHUNKS)], xs_hbm.at[pl.ds(dest, ROW_CHUNKS)], sem)

    def start_tok(t, c):
        for k in range(TOP_K):
            row_copy(t, k).start(priority=k % 2)
            meta_ref[dest_ref[k, t]] = (base + t) * TOP_K + k
        return c

    def wait_tok(t, c):
        for k in range(TOP_K):
            row_copy(t, k).wait()
        return c

    lax.fori_loop(0, DISPATCH_TOKENS, start_tok, 0, unroll=4)
    lax.fori_loop(0, DISPATCH_TOKENS, wait_tok, 0)


def _dispatch_call(dest3, poff, pend, h2, n_rows):
    smem_blk = pl.BlockSpec((None, TOP_K, DISPATCH_TOKENS), lambda i: (i, 0, 0), memory_space=pltpu.SMEM)
    smem = pl.BlockSpec(memory_space=pltpu.SMEM)
    return pl.pallas_call(
        _dispatch_kernel,
        grid=(dest3.shape[0],),
        in_specs=[smem_blk, smem, smem,
                  pl.BlockSpec((DISPATCH_TOKENS * ROW_CHUNKS, LANES), lambda i: (i, 0))],
        out_specs=[pl.BlockSpec(memory_space=pl.ANY), smem],
        out_shape=[jax.ShapeDtypeStruct((n_rows * ROW_CHUNKS, LANES), U32),
                   jax.ShapeDtypeStruct((n_rows,), jnp.int32)],
        scratch_shapes=[
            pltpu.VMEM((ROW_TILE * ROW_CHUNKS, LANES), U32),
            pltpu.SemaphoreType.DMA,
            pltpu.SemaphoreType.DMA,
        ],
        compiler_params=pltpu.CompilerParams(dimension_semantics=("arbitrary",)),
        name="dispatch_rows",
    )(dest3, poff, pend, h2)


def _expert_kernel(te_ref, tv_ref, tfirst_ref, tnext_ref, tslot_ref, tnv_ref, tlast_ref,
                   meta_ref, meta_prev_ref, xs_ref, wg_hbm, wu_hbm, wd_hbm, ys_hbm,
                   yo, xb_ref, wg_stage, wu_stage, wd_stage, wg_bf, wu_bf, wd_bf, wsems, ssems,
                   *, dump_row):
    i = pl.program_id(0)
    slot = lax.rem(i, 2)
    tile_rows = ROW_TILE * ROW_CHUNKS

    def scatter_copy(m_ref, n_valid, s, r):
        m = m_ref[0, r]
        dst = jnp.where(r < n_valid, (m + (m >> 3)) * ROW_CHUNKS, dump_row + r * ROW_CHUNKS)
        src = s * tile_rows + r * ROW_CHUNKS
        return pltpu.make_async_copy(yo.at[pl.ds(pl.multiple_of(src, ROW_CHUNKS), ROW_CHUNKS)],
                                     ys_hbm.at[pl.ds(pl.multiple_of(dst, ROW_CHUNKS), ROW_CHUNKS)], ssems.at[s])

    def start_rows(make, m_ref, n_valid, s):
        def body(r, c):
            make(m_ref, n_valid, s, r).start(priority=0)
            make(m_ref, n_valid, s, r + ROW_TILE // 2).start(priority=1)
            return c
        lax.fori_loop(0, ROW_TILE // 2, body, 0, unroll=8)

    def start_rows_inline(make, m_ref, n_valid, s):
        for r in range(ROW_TILE):
            make(m_ref, n_valid, s, r).start(priority=r % 2)

    def wait_rows(src_ref, dst_ref, sem):
        for _ in range(ROW_TILE):
            pltpu.make_async_copy(src_ref.at[pl.ds(0, ROW_CHUNKS)], dst_ref.at[pl.ds(0, ROW_CHUNKS)], sem).wait()

    def weight_copies(e, s):
        return (pltpu.make_async_copy(wg_hbm.at[e], wg_stage.at[s], wsems.at[s, 0]),
                pltpu.make_async_copy(wu_hbm.at[e], wu_stage.at[s], wsems.at[s, 1]),
                pltpu.make_async_copy(wd_hbm.at[e], wd_stage.at[s], wsems.at[s, 2]))

    valid = tv_ref[i] == 1

    @pl.when((i == 0) & valid)
    def _prologue():
        for cp in weight_copies(te_ref[0], tslot_ref[0]):
            cp.start(priority=WEIGHT_DMA_PRIORITY)
        yo[...] = jnp.zeros_like(yo)
        start_rows(scatter_copy, meta_ref, 0, 0)

    n_valid_prev = jnp.where(i >= 1, tnv_ref[jnp.maximum(i - 1, 0)], 0)

    @pl.when(valid & (tfirst_ref[i] == 1))
    def _new_expert():
        ws = tslot_ref[i]
        for cp in weight_copies(te_ref[i], ws):
            cp.wait()

        @pl.when(tnext_ref[i] >= 0)
        def _():
            for cp in weight_copies(tnext_ref[i], 1 - ws):
                cp.start(priority=WEIGHT_DMA_PRIORITY)

        def cast_in(j, c):
            r = pl.multiple_of(j * WEIGHT_CAST_ROWS, WEIGHT_CAST_ROWS)
            wg_bf[pl.ds(r, WEIGHT_CAST_ROWS), :] = wg_stage[ws, pl.ds(r, WEIGHT_CAST_ROWS), :].astype(BF16)
            wu_bf[pl.ds(r, WEIGHT_CAST_ROWS), :] = wu_stage[ws, pl.ds(r, WEIGHT_CAST_ROWS), :].astype(BF16)
            return c

        def cast_out(j, c):
            r = pl.multiple_of(j * (WEIGHT_CAST_ROWS // 4), WEIGHT_CAST_ROWS // 4)
            wd_bf[pl.ds(r, WEIGHT_CAST_ROWS // 4), :] = wd_stage[ws, pl.ds(r, WEIGHT_CAST_ROWS // 4), :].astype(BF16)
            return c

        lax.fori_loop(0, D_MODEL // WEIGHT_CAST_ROWS, cast_in, 0)
        lax.fori_loop(0, EXPERT_FF // (WEIGHT_CAST_ROWS // 4), cast_out, 0)

    @pl.when(valid)
    def _tile():
        base = pl.multiple_of(slot * tile_rows, tile_rows)
        for c in range(ROW_CHUNKS):
            lo, hi = _unpack_bf16_pair(xs_ref[pl.ds(c, ROW_TILE, stride=ROW_CHUNKS), :])
            xb_ref[:, c * LANES:(c + 1) * LANES] = lo.astype(BF16)
            xb_ref[:, HALF_MODEL + c * LANES:HALF_MODEL + (c + 1) * LANES] = hi.astype(BF16)
        start_rows_inline(scatter_copy, meta_prev_ref, n_valid_prev, 1 - slot)
        xb = xb_ref[...]
        g = _dot(xb, wg_bf[...])
        u = _dot(xb, wu_bf[...])
        hb = (g * _sigmoid(g) * u).astype(BF16)
        y = _dot(hb, wd_bf[...])
        wait_rows(yo, ys_hbm, ssems.at[slot])
        for c in range(ROW_CHUNKS):
            yo[pl.ds(base + c, ROW_TILE, stride=ROW_CHUNKS), :] = _pack_bf16_pair(
                y[:, c * LANES:(c + 1) * LANES], y[:, HALF_MODEL + c * LANES:HALF_MODEL + (c + 1) * LANES])

    @pl.when(valid & (tlast_ref[i] == 1))
    def _drain():
        start_rows(scatter_copy, meta_ref, tnv_ref[i], slot)
        wait_rows(yo, ys_hbm, ssems.at[1 - slot])
        wait_rows(yo, ys_hbm, ssems.at[slot])


def _expert_call(tile_meta, tile_b, meta, xs, w_gate, w_up, w_down, tokens):
    n_tiles = tile_b.shape[0]
    meta3 = meta.reshape(n_tiles, 1, ROW_TILE)
    dump_row = tokens * TOKEN_PITCH
    meta_blk = pl.BlockSpec((None, 1, ROW_TILE), lambda i, te, tv, tf, tn, ts, tnv, tl, tb: (tb[i], 0, 0),
                            memory_space=pltpu.SMEM)
    row_blk = pl.BlockSpec((ROW_TILE * ROW_CHUNKS, LANES), lambda i, te, tv, tf, tn, ts, tnv, tl, tb: (tb[i], 0))
    meta_prev_blk = pl.BlockSpec(
        (None, 1, ROW_TILE), lambda i, te, tv, tf, tn, ts, tnv, tl, tb: (tb[jnp.maximum(i - 1, 0)], 0, 0),
        memory_space=pltpu.SMEM)
    any_spec = pl.BlockSpec(memory_space=pl.ANY)

    def body(te, tv, tf, tn, ts, tnv, tl, tb, *refs):
        _expert_kernel(te, tv, tf, tn, ts, tnv, tl, *refs, dump_row=dump_row)

    return pl.pallas_call(
        body,
        grid_spec=pltpu.PrefetchScalarGridSpec(
            num_scalar_prefetch=len(tile_meta) + 1,
            grid=(n_tiles,),
            in_specs=[meta_blk, meta_prev_blk, row_blk, any_spec, any_spec, any_spec],
            out_specs=any_spec,
            scratch_shapes=[
                pltpu.VMEM((2 * ROW_TILE * ROW_CHUNKS, LANES), U32),
                pltpu.VMEM((ROW_TILE, D_MODEL), BF16),
                pltpu.VMEM((2, D_MODEL, EXPERT_FF), F32),
                pltpu.VMEM((2, D_MODEL, EXPERT_FF), F32),
                pltpu.VMEM((2, EXPERT_FF, D_MODEL), F32),
                pltpu.VMEM((D_MODEL, EXPERT_FF), BF16),
                pltpu.VMEM((D_MODEL, EXPERT_FF), BF16),
                pltpu.VMEM((EXPERT_FF, D_MODEL), BF16),
                pltpu.SemaphoreType.DMA((2, 3)),
                pltpu.SemaphoreType.DMA((2,)),
            ],
        ),
        out_shape=jax.ShapeDtypeStruct((dump_row + ROW_TILE * ROW_CHUNKS, LANES), U32),
        compiler_params=pltpu.CompilerParams(
            dimension_semantics=("arbitrary",), vmem_limit_bytes=VMEM_LIMIT_BYTES),
        name="routed_experts",
    )(*tile_meta, tile_b, meta3, meta3, xs, w_gate, w_up, w_down)


def _final_kernel(h_ref, gffn_ref, p_ref, wt_ref, ys_ref,
                  wsg_ref, wsu_ref, wsd_ref, wpp_ref, gple_ref, wpg_ref, gfin_ref,
                  out_ref, h3_ref):
    TB = FINAL_TOKENS
    h = h_ref[...]
    h2b = _rms(h, gffn_ref[...]).astype(BF16)
    g = _dot(h2b, wsg_ref[...])
    u = _dot(h2b, wsu_ref[...])
    shared = _dot((g * _sigmoid(g) * u).astype(BF16), wsd_ref[...])
    ple = _rms(_dot(p_ref[...].astype(BF16), wpp_ref[...]), gple_ref[...])

    wt = wt_ref[...]
    for c in range(ROW_CHUNKS):
        routed_lo = routed_hi = None
        for k in range(TOP_K):
            lo, hi = _unpack_bf16_pair(ys_ref[pl.ds(k * ROW_CHUNKS + c, TB, stride=TOKEN_PITCH), :])
            wk = wt[:, k:k + 1]
            routed_lo = lo * wk if k == 0 else routed_lo + lo * wk
            routed_hi = hi * wk if k == 0 else routed_hi + hi * wk
        for routed, c0 in ((routed_lo, c * LANES), (routed_hi, HALF_MODEL + c * LANES)):
            cs = slice(c0, c0 + LANES)
            h3_ref[:, cs] = h[:, cs] + (routed + shared[:, cs])

    h3 = h3_ref[...]
    gate = _sigmoid(_dot(h3.astype(BF16), wpg_ref[...]))
    h4 = h3 + gate * ple
    out_ref[...] = _rms(h4, gfin_ref[...])


def _final_call(h, g_ffn, p, wt, ys, ws_gate, ws_up, ws_down, w_pp, g_ple, w_pg, g_final):
    TB = FINAL_TOKENS
    tokens = h.shape[0]
    const = lambda shape: pl.BlockSpec(shape, lambda i: (0,) * len(shape), pipeline_mode=pl.Buffered(1))
    tok = pl.BlockSpec((TB, D_MODEL), lambda i: (i, 0))
    return pl.pallas_call(
        _final_kernel,
        grid=(tokens // TB,),
        in_specs=[
            tok,
            const((1, D_MODEL)),
            pl.BlockSpec((TB, PLE_DIM), lambda i: (i, 0)),
            pl.BlockSpec((TB, TOP_K), lambda i: (i, 0)),
            pl.BlockSpec((TB * TOKEN_PITCH, LANES), lambda i: (i, 0)),
            const((D_MODEL, SHARED_FF)),
            const((D_MODEL, SHARED_FF)),
            const((SHARED_FF, D_MODEL)),
            const((PLE_DIM, D_MODEL)),
            const((1, D_MODEL)),
            const((D_MODEL, D_MODEL)),
            const((1, D_MODEL)),
        ],
        out_specs=tok,
        out_shape=jax.ShapeDtypeStruct((tokens, D_MODEL), F32),
        scratch_shapes=[pltpu.VMEM((TB, D_MODEL), F32)],
        compiler_params=pltpu.CompilerParams(
            dimension_semantics=("arbitrary",), vmem_limit_bytes=VMEM_LIMIT_BYTES),
        name="combine_final",
    )(h, g_ffn, p, wt, ys, ws_gate, ws_up, ws_down, w_pp, g_ple, w_pg, g_final)


def _per_step(a, tokens_per_step):
    tokens = a.shape[1]
    return a.reshape(TOP_K, tokens // tokens_per_step, tokens_per_step).transpose(1, 0, 2)


def _tile_metadata(counts, n_tiles_max):
    i32 = jnp.int32
    padded = (counts + ROW_TILE - 1) // ROW_TILE * ROW_TILE
    pend = jnp.cumsum(padded).astype(i32)
    poff = pend - padded
    n_tiles = pend[-1] // ROW_TILE
    tile = jnp.arange(n_tiles_max, dtype=i32)
    tile_v = (tile < n_tiles).astype(i32)
    tile_b = jnp.minimum(tile, jnp.maximum(n_tiles - 1, 0))
    tile_e = jnp.minimum(jnp.sum(pend[None, :] <= (tile_b * ROW_TILE)[:, None], axis=1), N_EXPERTS - 1).astype(i32)
    onehot = tile_e[:, None] == jnp.arange(N_EXPERTS, dtype=i32)[None, :]

    def lookup(table):
        return jnp.sum(jnp.where(onehot, table[None, :], 0), axis=1).astype(i32)

    tile_first = (tile_v * (tile_b * ROW_TILE == lookup(poff))).astype(i32)
    nonempty = counts > 0
    order = jnp.cumsum(nonempty.astype(i32)) - 1
    ids = jnp.where(nonempty, jnp.arange(N_EXPERTS, dtype=i32), N_EXPERTS)
    later = jnp.flip(lax.cummin(jnp.flip(ids)))
    next_e = jnp.concatenate([later[1:], jnp.full((1,), N_EXPERTS, i32)])
    next_e = jnp.where(next_e == N_EXPERTS, -1, next_e)
    tile_next = lookup(next_e)
    tile_slot = lookup(order % 2)
    tile_nvalid = jnp.clip(lookup(poff + counts) - tile_b * ROW_TILE, 0, ROW_TILE).astype(i32)
    tile_last = (tile_v * (tile == n_tiles - 1)).astype(i32)
    return poff, pend, tile_b, (tile_e, tile_v, tile_first, tile_next, tile_slot, tile_nvalid, tile_last)


def kernel(x, p, g_mix, w_in, w_pool, pool_scale, attn_sinks, rpe_table, w_out, g_ffn, w_router, router_bias,
           w_gate, w_up, w_down, ws_gate, ws_up, ws_down, w_ple_proj, g_ple, w_ple_gate, g_final):
    batch, seq, _ = x.shape
    tokens = batch * seq
    assert seq % MIX_TOKENS == 0 and tokens % DISPATCH_TOKENS == 0 and tokens % FINAL_TOKENS == 0
    n_rows = tokens * TOP_K + N_EXPERTS * ROW_TILE
    bucket = jnp.asarray(_rpe_bucket_map())
    h, h2, idx, topw, rank, cnt = _mix_call(
        x, g_mix[0][None, :], w_in[0].astype(BF16), w_pool[0].astype(BF16), pool_scale[0][None, :],
        attn_sinks[0], rpe_table, bucket, w_out[0].astype(BF16), g_ffn[0][None, :],
        w_router[0].T, router_bias[0][:, None])

    poff, pend, tile_b, tile_meta = _tile_metadata(cnt[:, 0].astype(jnp.int32), n_rows // ROW_TILE)
    experts = jnp.arange(N_EXPERTS, dtype=jnp.int32)[:, None, None]
    dest = jnp.sum(jnp.where(idx[None] == experts, poff[:, None, None], 0), axis=0) + rank
    xs, meta = _dispatch_call(_per_step(dest, DISPATCH_TOKENS), poff, pend, h2, n_rows)
    ys = _expert_call(tile_meta, tile_b, meta, xs, w_gate[0], w_up[0], w_down[0], tokens)
    out = _final_call(
        h.reshape(tokens, D_MODEL), g_ffn[0][None, :], p[0].reshape(tokens, PLE_DIM), topw.T, ys,
        ws_gate[0].astype(BF16), ws_up[0].astype(BF16), ws_down[0].astype(BF16),
        w_ple_proj[0].astype(BF16), g_ple[0][None, :], w_ple_gate[0].astype(BF16), g_final[None, :])
    return out.reshape(batch, seq, D_MODEL)
```

```python
import math

import numpy as np
import jax
import jax.numpy as jnp
from jax import lax
from jax.experimental import pallas as pl
from jax.experimental.pallas import tpu as pltpu

F32 = jnp.float32
BF16 = jnp.bfloat16
NEG_INF = float("-inf")

D_MODEL = 2048
PLE_DIM = 256
POOL_WIDTH = 1024
POOL_GROUPS = 4
POOL_GROUP_DIM = POOL_WIDTH // POOL_GROUPS
POOL_WINDOWS = (2, 4, 8, 16)
N_HEADS = 16
N_KV_HEADS = 2
HEAD_DIM = 64
HEADS_PER_KV = N_HEADS // N_KV_HEADS
ATTN_WIDTH = N_HEADS * HEAD_DIM
KV_WIDTH = N_KV_HEADS * HEAD_DIM
MIX_WIDTH = POOL_WIDTH + ATTN_WIDTH
IN_WIDTH = POOL_WIDTH + ATTN_WIDTH + 2 * KV_WIDTH
Q_OFF = POOL_WIDTH
K_OFF = POOL_WIDTH + ATTN_WIDTH
V_OFF = K_OFF + KV_WIDTH
ATTN_BLOCK = 128
WINDOW = 128
RPE_BUCKETS = 32
RPE_MAX_EXACT = RPE_BUCKETS // 2
RPE_MAX_DISTANCE = 128
N_EXPERTS = 64
TOP_K = 8
N_EXPERT_GROUPS = 8
EXPERTS_PER_GROUP = N_EXPERTS // N_EXPERT_GROUPS
TOPK_GROUPS = 4
EXPERT_FF = 512
SHARED_FF = 512
ROUTED_SCALE = 2.5
EPS = 1e-6

LANES = 128
SUBLANES = 8
VMEM_LIMIT_BYTES = 58 * 1024 * 1024

MIX_TOKENS = 512
POOL_HISTORY = 16
ROW_TILE = 256
DISPATCH_TOKENS = 512
FINAL_TOKENS = 256
WEIGHT_CAST_ROWS = 256
WEIGHT_DMA_PRIORITY = 1
HALF_MODEL = D_MODEL // 2
ROW_CHUNKS = HALF_MODEL // LANES
TOKEN_PITCH = (TOP_K + 1) * ROW_CHUNKS
U32 = jnp.uint32


def _rms(x, g):
    return x * lax.rsqrt(jnp.mean(x * x, axis=-1, keepdims=True) + EPS) * g


def _sigmoid(x):
    return 1.0 / (1.0 + jnp.exp(-x))


def _pack_bf16_pair(lo, hi):
    ulo = lax.bitcast_convert_type(lo.astype(BF16).astype(F32), U32)
    uhi = lax.bitcast_convert_type(hi.astype(BF16).astype(F32), U32)
    return (ulo >> 16) | uhi


def _unpack_bf16_pair(w):
    lo = lax.bitcast_convert_type(w << 16, F32)
    hi = lax.bitcast_convert_type(w & jnp.uint32(0xFFFF0000), F32)
    return lo, hi


def _dot(a, b):
    return jnp.dot(a, b, preferred_element_type=F32)


def _dot_nt(a, b, precision=None):
    return lax.dot_general(a, b, (((1,), (1,)), ((), ())), preferred_element_type=F32, precision=precision)


def _rpe_bucket_map():
    i = np.arange(ATTN_BLOCK)[:, None]
    j = np.arange(2 * ATTN_BLOCK)[None, :]
    dist = i + ATTN_BLOCK - j
    n = np.maximum(dist, 0)
    nf = np.maximum(n, 1).astype(np.float32)
    large = RPE_MAX_EXACT + (np.log(nf / np.float32(RPE_MAX_EXACT)) / np.float32(math.log(RPE_MAX_DISTANCE / RPE_MAX_EXACT))
                             * np.float32(RPE_BUCKETS - RPE_MAX_EXACT)).astype(np.int32)
    large = np.minimum(large, RPE_BUCKETS - 1)
    bucket = np.where(n < RPE_MAX_EXACT, n, large)
    valid = (dist >= 0) & (dist < WINDOW)
    return np.where(valid, bucket, -1).astype(np.int32)


def _mix_kernel(x_ref, gmix_ref, win_ref, wpool_ref, pscale_ref, sinks_ref, rpe_ref, bucket_ref,
                wout_ref, gffn_ref, wrt_ref, rbias_ref,
                h_ref, h2_ref, idx_ref, topw_ref, rank_ref, cnt_ref,
                ubuf, kbuf, vbuf, bias_buf, ybuf, cnt_acc):
    TS = MIX_TOKENS
    b = pl.program_id(0)
    s = pl.program_id(1)

    @pl.when((b == 0) & (s == 0))
    def _init():
        bucket = bucket_ref[...]
        for h in range(N_HEADS):
            acc = jnp.full((ATTN_BLOCK, 2 * ATTN_BLOCK), NEG_INF, F32)
            for bk in range(RPE_BUCKETS):
                acc = jnp.where(bucket == bk, rpe_ref[bk, h], acc)
            bias_buf[h * ATTN_BLOCK:(h + 1) * ATTN_BLOCK, :] = acc
        cnt_acc[...] = jnp.zeros_like(cnt_acc)

    @pl.when(s == 0)
    def _reset_history():
        ubuf[0:POOL_HISTORY, :] = jnp.zeros((POOL_HISTORY, POOL_WIDTH), F32)
        kbuf[0:ATTN_BLOCK, :] = jnp.zeros((ATTN_BLOCK, KV_WIDTH), BF16)
        vbuf[0:ATTN_BLOCK, :] = jnp.zeros((ATTN_BLOCK, KV_WIDTH), BF16)

    x = x_ref[...]
    a = _rms(x, gmix_ref[...]).astype(BF16)
    z = _dot(a, win_ref[...])

    ubuf[POOL_HISTORY:POOL_HISTORY + TS, :] = z[:, 0:POOL_WIDTH]
    pos = s * TS + lax.broadcasted_iota(jnp.int32, (TS, 1), 0)
    for gi, w in enumerate(POOL_WINDOWS):
        c0, c1 = gi * POOL_GROUP_DIM, (gi + 1) * POOL_GROUP_DIM
        e = ubuf[:, c0:c1]
        shift = 1
        while shift < w:
            e = e + pltpu.roll(e, shift, axis=0)
            shift *= 2
        wsum = e[POOL_HISTORY:, :]
        count = jnp.minimum(pos + 1, w).astype(F32)
        pooled = wsum / count - z[:, c0:c1]
        yp = _dot(pooled.astype(BF16), wpool_ref[gi]) * pscale_ref[:, c0:c1]
        ybuf[:, c0:c1] = yp.astype(BF16)
    ubuf[0:POOL_HISTORY, :] = ubuf[TS:TS + POOL_HISTORY, :]

    kbuf[ATTN_BLOCK:ATTN_BLOCK + TS, :] = z[:, K_OFF:K_OFF + KV_WIDTH].astype(BF16)
    vbuf[ATTN_BLOCK:ATTN_BLOCK + TS, :] = z[:, V_OFF:V_OFF + KV_WIDTH].astype(BF16)
    lane = lax.broadcasted_iota(jnp.int32, (ATTN_BLOCK, LANES), 1)
    low_half = lane < HEAD_DIM
    high_half = jnp.logical_not(low_half)
    col = lax.broadcasted_iota(jnp.int32, (1, 2 * ATTN_BLOCK), 1)
    first_mask = jnp.where((col < ATTN_BLOCK) & (s == 0), NEG_INF, 0.0).astype(F32)
    for sb in range(TS // ATTN_BLOCK):
        r0 = sb * ATTN_BLOCK
        kband = kbuf[r0:r0 + 2 * ATTN_BLOCK, :]
        vband = vbuf[r0:r0 + 2 * ATTN_BLOCK, :]
        for p in range(N_HEADS // 2):
            qp = z[r0:r0 + ATTN_BLOCK, Q_OFF + p * LANES:Q_OFF + (p + 1) * LANES] * (HEAD_DIM ** -0.5)
            qr = pltpu.roll(qp, HEAD_DIM, axis=1)
            kvh = (2 * p) // HEADS_PER_KV
            kv_lanes = low_half if kvh == 0 else high_half
            outs = []
            for par in range(2):
                h = 2 * p + par
                qh = jnp.where(kv_lanes, qp if par == kvh else qr, 0.0).astype(BF16)
                lg = _dot_nt(qh, kband) + bias_buf[h * ATTN_BLOCK:(h + 1) * ATTN_BLOCK, :]
                if sb == 0:
                    lg = lg + first_mask
                sink = sinks_ref[h]
                m = jnp.maximum(jnp.max(lg, axis=-1, keepdims=True), sink)
                pe = jnp.exp(lg - m)
                den = jnp.sum(pe, axis=-1, keepdims=True) + jnp.exp(sink - m)
                outs.append(_dot(pe.astype(BF16), vband) / den)
            if kvh == 0:
                pair = jnp.where(low_half, outs[0], pltpu.roll(outs[1], HEAD_DIM, axis=1))
            else:
                pair = jnp.where(low_half, pltpu.roll(outs[0], HEAD_DIM, axis=1), outs[1])
            ybuf[r0:r0 + ATTN_BLOCK, POOL_WIDTH + p * LANES:POOL_WIDTH + (p + 1) * LANES] = pair.astype(BF16)
    kbuf[0:ATTN_BLOCK, :] = kbuf[TS:TS + ATTN_BLOCK, :]
    vbuf[0:ATTN_BLOCK, :] = vbuf[TS:TS + ATTN_BLOCK, :]

    h = x + _dot(ybuf[...], wout_ref[...])
    h_ref[...] = h
    h2 = _rms(h, gffn_ref[...])
    for c in range(ROW_CHUNKS):
        h2_ref[pl.ds(c, TS, stride=ROW_CHUNKS), :] = _pack_bf16_pair(
            h2[:, c * LANES:(c + 1) * LANES], h2[:, HALF_MODEL + c * LANES:HALF_MODEL + (c + 1) * LANES])

    logits = _dot_nt(wrt_ref[...], h2, precision=lax.Precision.HIGHEST)
    scores = _sigmoid(logits)
    biased = scores + rbias_ref[...]
    erow = lax.broadcasted_iota(jnp.int32, (N_EXPERTS, TS), 0)
    grow = lax.broadcasted_iota(jnp.int32, (EXPERTS_PER_GROUP, TS), 0)
    group_scores = []
    for g in range(N_EXPERT_GROUPS):
        blk = biased[g * EXPERTS_PER_GROUP:(g + 1) * EXPERTS_PER_GROUP, :]
        m1 = jnp.max(blk, axis=0, keepdims=True)
        i1 = jnp.min(jnp.where(blk == m1, grow, EXPERTS_PER_GROUP), axis=0, keepdims=True)
        m2 = jnp.max(jnp.where(grow == i1, NEG_INF, blk), axis=0, keepdims=True)
        group_scores.append(m1 + m2)
    cur = jnp.concatenate(group_scores, axis=0)
    gsel = jnp.zeros((N_EXPERT_GROUPS, TS), jnp.bool_)
    for _ in range(TOPK_GROUPS):
        m = jnp.max(cur, axis=0, keepdims=True)
        i = jnp.min(jnp.where(cur == m, grow, N_EXPERT_GROUPS), axis=0, keepdims=True)
        hit = grow == i
        gsel = jnp.logical_or(gsel, hit)
        cur = jnp.where(hit, NEG_INF, cur)
    gmask = jnp.concatenate(
        [jnp.broadcast_to(gsel[g:g + 1, :], (EXPERTS_PER_GROUP, TS)) for g in range(N_EXPERT_GROUPS)], axis=0)
    masked = jnp.where(gmask, biased, NEG_INF)
    sel = jnp.zeros((N_EXPERTS, TS), jnp.bool_)
    idxs, ws = [], []
    for _ in range(TOP_K):
        m = jnp.max(masked, axis=0, keepdims=True)
        i = jnp.min(jnp.where(masked == m, erow, N_EXPERTS), axis=0, keepdims=True)
        hit = erow == i
        idxs.append(i)
        ws.append(jnp.sum(jnp.where(hit, scores, 0.0), axis=0, keepdims=True))
        sel = jnp.logical_or(sel, hit)
        masked = jnp.where(hit, NEG_INF, masked)
    wtot = ws[0]
    for wk in ws[1:]:
        wtot = wtot + wk
    idx_ref[...] = jnp.concatenate(idxs, axis=0)
    topw_ref[...] = jnp.concatenate([wk / wtot * ROUTED_SCALE for wk in ws], axis=0)

    self32 = sel.astype(F32)
    ri = lax.broadcasted_iota(jnp.int32, (TS, TS), 0)
    ci = lax.broadcasted_iota(jnp.int32, (TS, TS), 1)
    before = (ri < ci).astype(BF16)
    running = _dot(self32.astype(BF16), before) + cnt_acc[:, 0:1]
    rank_ref[...] = jnp.concatenate(
        [jnp.sum(jnp.where(erow == i, running, 0.0), axis=0, keepdims=True) for i in idxs], axis=0).astype(jnp.int32)
    cnt_acc[...] = cnt_acc[...] + jnp.sum(self32, axis=1, keepdims=True)
    cnt_ref[...] = cnt_acc[...]


def _mix_call(x, g_mix, w_in, w_pool, pool_scale, sinks, rpe_table, bucket, w_out, g_ffn, w_rt, r_bias):
    batch, seq, _ = x.shape
    tokens = batch * seq
    TS = MIX_TOKENS
    ns = seq // TS
    const = lambda shape: pl.BlockSpec(shape, lambda b, s: (0,) * len(shape), pipeline_mode=pl.Buffered(1))
    smem = pl.BlockSpec(memory_space=pltpu.SMEM)
    tok3 = pl.BlockSpec((None, TS, D_MODEL), lambda b, s: (b, s, 0))
    lane_blk = pl.BlockSpec((TOP_K, TS), lambda b, s: (0, b * ns + s))
    return pl.pallas_call(
        _mix_kernel,
        grid=(batch, ns),
        in_specs=[
            tok3,
            const((1, D_MODEL)),
            const((D_MODEL, IN_WIDTH)),
            const((POOL_GROUPS, POOL_GROUP_DIM, POOL_GROUP_DIM)),
            const((1, POOL_WIDTH)),
            smem,
            smem,
            const((ATTN_BLOCK, 2 * ATTN_BLOCK)),
            const((MIX_WIDTH, D_MODEL)),
            const((1, D_MODEL)),
            const((N_EXPERTS, D_MODEL)),
            const((N_EXPERTS, 1)),
        ],
        out_specs=[
            pl.BlockSpec((None, TS, D_MODEL), lambda b, s: (b, s, 0), pipeline_mode=pl.Buffered(1)),
            pl.BlockSpec((TS * ROW_CHUNKS, LANES), lambda b, s: (b * ns + s, 0), pipeline_mode=pl.Buffered(1)),
            lane_blk,
            lane_blk,
            lane_blk,
            pl.BlockSpec((N_EXPERTS, LANES), lambda b, s: (0, 0)),
        ],
        out_shape=[
            jax.ShapeDtypeStruct((batch, seq, D_MODEL), F32),
            jax.ShapeDtypeStruct((tokens * ROW_CHUNKS, LANES), U32),
            jax.ShapeDtypeStruct((TOP_K, tokens), jnp.int32),
            jax.ShapeDtypeStruct((TOP_K, tokens), F32),
            jax.ShapeDtypeStruct((TOP_K, tokens), jnp.int32),
            jax.ShapeDtypeStruct((N_EXPERTS, LANES), F32),
        ],
        scratch_shapes=[
            pltpu.VMEM((POOL_HISTORY + TS, POOL_WIDTH), F32),
            pltpu.VMEM((ATTN_BLOCK + TS, KV_WIDTH), BF16),
            pltpu.VMEM((ATTN_BLOCK + TS, KV_WIDTH), BF16),
            pltpu.VMEM((N_HEADS * ATTN_BLOCK, 2 * ATTN_BLOCK), F32),
            pltpu.VMEM((TS, MIX_WIDTH), BF16),
            pltpu.VMEM((N_EXPERTS, LANES), F32),
        ],
        compiler_params=pltpu.CompilerParams(
            dimension_semantics=("arbitrary", "arbitrary"), vmem_limit_bytes=VMEM_LIMIT_BYTES),
        name="mix_router",
    )(x, g_mix, w_in, w_pool, pool_scale, sinks, rpe_table, bucket, w_out, g_ffn, w_rt, r_bias)


def _dispatch_kernel(dest_ref, poff_ref, pend_ref, h2_ref, xs_hbm, meta_ref, zbuf, sem, zsem):
    step = pl.program_id(0)
    base = step * DISPATCH_TOKENS

    def zero_copy(e):
        start = pl.multiple_of((pend_ref[e] - ROW_TILE) * ROW_CHUNKS, ROW_CHUNKS)
        return pltpu.make_async_copy(zbuf, xs_hbm.at[pl.ds(start, ROW_TILE * ROW_CHUNKS)], zsem)

    @pl.when(step == 0)
    def _zero_tails():
        zbuf[...] = jnp.zeros_like(zbuf)

        def start(e, c):
            @pl.when(pend_ref[e] > poff_ref[e])
            def _():
                zero_copy(e).start()
            return c

        def wait(e, c):
            @pl.when(pend_ref[e] > poff_ref[e])
            def _():
                zero_copy(e).wait()
            return c

        lax.fori_loop(0, N_EXPERTS, start, 0)
        lax.fori_loop(0, N_EXPERTS, wait, 0)

    def row_copy(t, k):
        src = pl.multiple_of(t * ROW_CHUNKS, ROW_CHUNKS)
        dest = pl.multiple_of(dest_ref[k, t] * ROW_CHUNKS, ROW_CHUNKS)
        return pltpu.make_async_copy(h2_ref.at[pl.ds(src, ROW_CHUNKS)], xs_hbm.at[pl.ds(dest, ROW_CHUNKS)], sem)

    def start_tok(t, c):
        for k in range(TOP_K):
            row_copy(t, k).start(priority=k % 2)
            meta_ref[dest_ref[k, t]] = (base + t) * TOP_K + k
        return c

    def wait_tok(t, c):
        for k in range(TOP_K):
            row_copy(t, k).wait()
        return c

    lax.fori_loop(0, DISPATCH_TOKENS, start_tok, 0, unroll=4)
    lax.fori_loop(0, DISPATCH_TOKENS, wait_tok, 0)


def _dispatch_call(dest3, poff, pend, h2, n_rows):
    smem_blk = pl.BlockSpec((None, TOP_K, DISPATCH_TOKENS), lambda i: (i, 0, 0), memory_space=pltpu.SMEM)
    smem = pl.BlockSpec(memory_space=pltpu.SMEM)
    return pl.pallas_call(
        _dispatch_kernel,
        grid=(dest3.shape[0],),
        in_specs=[smem_blk, smem, smem,
                  pl.BlockSpec((DISPATCH_TOKENS * ROW_CHUNKS, LANES), lambda i: (i, 0))],
        out_specs=[pl.BlockSpec(memory_space=pl.ANY), smem],
        out_shape=[jax.ShapeDtypeStruct((n_rows * ROW_CHUNKS, LANES), U32),
                   jax.ShapeDtypeStruct((n_rows,), jnp.int32)],
        scratch_shapes=[
            pltpu.VMEM((ROW_TILE * ROW_CHUNKS, LANES), U32),
            pltpu.SemaphoreType.DMA,
            pltpu.SemaphoreType.DMA,
        ],
        compiler_params=pltpu.CompilerParams(dimension_semantics=("arbitrary",)),
        name="dispatch_rows",
    )(dest3, poff, pend, h2)


def _expert_kernel(te_ref, tv_ref, tfirst_ref, tnext_ref, tslot_ref, tnv_ref, tlast_ref,
                   meta_ref, meta_prev_ref, xs_ref, wg_hbm, wu_hbm, wd_hbm, ys_hbm,
                   yo, xb_ref, wg_stage, wu_stage, wd_stage, wg_bf, wu_bf, wd_bf, wsems, ssems,
                   *, dump_row):
    i = pl.program_id(0)
    slot = lax.rem(i, 2)
    tile_rows = ROW_TILE * ROW_CHUNKS

    def scatter_copy(m_ref, n_valid, s, r):
        m = m_ref[0, r]
        dst = jnp.where(r < n_valid, (m + (m >> 3)) * ROW_CHUNKS, dump_row + r * ROW_CHUNKS)
        src = s * tile_rows + r * ROW_CHUNKS
        return pltpu.make_async_copy(yo.at[pl.ds(pl.multiple_of(src, ROW_CHUNKS), ROW_CHUNKS)],
                                     ys_hbm.at[pl.ds(pl.multiple_of(dst, ROW_CHUNKS), ROW_CHUNKS)], ssems.at[s])

    def start_rows(make, m_ref, n_valid, s):
        def body(r, c):
            make(m_ref, n_valid, s, r).start(priority=0)
            make(m_ref, n_valid, s, r + ROW_TILE // 2).start(priority=1)
            return c
        lax.fori_loop(0, ROW_TILE // 2, body, 0, unroll=8)

    def start_rows_inline(make, m_ref, n_valid, s):
        for r in range(ROW_TILE):
            make(m_ref, n_valid, s, r).start(priority=r % 2)

    def wait_rows(src_ref, dst_ref, sem):
        for _ in range(ROW_TILE):
            pltpu.make_async_copy(src_ref.at[pl.ds(0, ROW_CHUNKS)], dst_ref.at[pl.ds(0, ROW_CHUNKS)], sem).wait()

    def weight_copies(e, s):
        return (pltpu.make_async_copy(wg_hbm.at[e], wg_stage.at[s], wsems.at[s, 0]),
                pltpu.make_async_copy(wu_hbm.at[e], wu_stage.at[s], wsems.at[s, 1]),
                pltpu.make_async_copy(wd_hbm.at[e], wd_stage.at[s], wsems.at[s, 2]))

    valid = tv_ref[i] == 1

    @pl.when((i == 0) & valid)
    def _prologue():
        for cp in weight_copies(te_ref[0], tslot_ref[0]):
            cp.start(priority=WEIGHT_DMA_PRIORITY)
        yo[...] = jnp.zeros_like(yo)
        start_rows(scatter_copy, meta_ref, 0, 0)

    n_valid_prev = jnp.where(i >= 1, tnv_ref[jnp.maximum(i - 1, 0)], 0)

    @pl.when(valid & (tfirst_ref[i] == 1))
    def _new_expert():
        ws = tslot_ref[i]
        for cp in weight_copies(te_ref[i], ws):
            cp.wait()

        @pl.when(tnext_ref[i] >= 0)
        def _():
            for cp in weight_copies(tnext_ref[i], 1 - ws):
                cp.start(priority=WEIGHT_DMA_PRIORITY)

        def cast_in(j, c):
            r = pl.multiple_of(j * WEIGHT_CAST_ROWS, WEIGHT_CAST_ROWS)
            wg_bf[pl.ds(r, WEIGHT_CAST_ROWS), :] = wg_stage[ws, pl.ds(r, WEIGHT_CAST_ROWS), :].astype(BF16)
            wu_bf[pl.ds(r, WEIGHT_CAST_ROWS), :] = wu_stage[ws, pl.ds(r, WEIGHT_CAST_ROWS), :].astype(BF16)
            return c

        def cast_out(j, c):
            r = pl.multiple_of(j * (WEIGHT_CAST_ROWS // 4), WEIGHT_CAST_ROWS // 4)
            wd_bf[pl.ds(r, WEIGHT_CAST_ROWS // 4), :] = wd_stage[ws, pl.ds(r, WEIGHT_CAST_ROWS // 4), :].astype(BF16)
            return c

        lax.fori_loop(0, D_MODEL // WEIGHT_CAST_ROWS, cast_in, 0)
        lax.fori_loop(0, EXPERT_FF // (WEIGHT_CAST_ROWS // 4), cast_out, 0)

    @pl.when(valid)
    def _tile():
        base = pl.multiple_of(slot * tile_rows, tile_rows)
        for c in range(ROW_CHUNKS):
            lo, hi = _unpack_bf16_pair(xs_ref[pl.ds(c, ROW_TILE, stride=ROW_CHUNKS), :])
            xb_ref[:, c * LANES:(c + 1) * LANES] = lo.astype(BF16)
            xb_ref[:, HALF_MODEL + c * LANES:HALF_MODEL + (c + 1) * LANES] = hi.astype(BF16)
        start_rows_inline(scatter_copy, meta_prev_ref, n_valid_prev, 1 - slot)
        xb = xb_ref[...]
        g = _dot(xb, wg_bf[...])
        u = _dot(xb, wu_bf[...])
        hb = (g * _sigmoid(g) * u).astype(BF16)
        y = _dot(hb, wd_bf[...])
        wait_rows(yo, ys_hbm, ssems.at[slot])
        for c in range(ROW_CHUNKS):
            yo[pl.ds(base + c, ROW_TILE, stride=ROW_CHUNKS), :] = _pack_bf16_pair(
                y[:, c * LANES:(c + 1) * LANES], y[:, HALF_MODEL + c * LANES:HALF_MODEL + (c + 1) * LANES])

    @pl.when(valid & (tlast_ref[i] == 1))
    def _drain():
        start_rows(scatter_copy, meta_ref, tnv_ref[i], slot)
        wait_rows(yo, ys_hbm, ssems.at[1 - slot])
        wait_rows(yo, ys_hbm, ssems.at[slot])


def _expert_call(tile_meta, tile_b, meta, xs, w_gate, w_up, w_down, tokens):
    n_tiles = tile_b.shape[0]
    meta3 = meta.reshape(n_tiles, 1, ROW_TILE)
    dump_row = tokens * TOKEN_PITCH
    meta_blk = pl.BlockSpec((None, 1, ROW_TILE), lambda i, te, tv, tf, tn, ts, tnv, tl, tb: (tb[i], 0, 0),
                            memory_space=pltpu.SMEM)
    row_blk = pl.BlockSpec((ROW_TILE * ROW_CHUNKS, LANES), lambda i, te, tv, tf, tn, ts, tnv, tl, tb: (tb[i], 0))
    meta_prev_blk = pl.BlockSpec(
        (None, 1, ROW_TILE), lambda i, te, tv, tf, tn, ts, tnv, tl, tb: (tb[jnp.maximum(i - 1, 0)], 0, 0),
        memory_space=pltpu.SMEM)
    any_spec = pl.BlockSpec(memory_space=pl.ANY)

    def body(te, tv, tf, tn, ts, tnv, tl, tb, *refs):
        _expert_kernel(te, tv, tf, tn, ts, tnv, tl, *refs, dump_row=dump_row)

    return pl.pallas_call(
        body,
        grid_spec=pltpu.PrefetchScalarGridSpec(
            num_scalar_prefetch=len(tile_meta) + 1,
            grid=(n_tiles,),
            in_specs=[meta_blk, meta_prev_blk, row_blk, any_spec, any_spec, any_spec],
            out_specs=any_spec,
            scratch_shapes=[
                pltpu.VMEM((2 * ROW_TILE * ROW_CHUNKS, LANES), U32),
                pltpu.VMEM((ROW_TILE, D_MODEL), BF16),
                pltpu.VMEM((2, D_MODEL, EXPERT_FF), F32),
                pltpu.VMEM((2, D_MODEL, EXPERT_FF), F32),
                pltpu.VMEM((2, EXPERT_FF, D_MODEL), F32),
                pltpu.VMEM((D_MODEL, EXPERT_FF), BF16),
                pltpu.VMEM((D_MODEL, EXPERT_FF), BF16),
                pltpu.VMEM((EXPERT_FF, D_MODEL), BF16),
                pltpu.SemaphoreType.DMA((2, 3)),
                pltpu.SemaphoreType.DMA((2,)),
            ],
        ),
        out_shape=jax.ShapeDtypeStruct((dump_row + ROW_TILE * ROW_CHUNKS, LANES), U32),
        compiler_params=pltpu.CompilerParams(
            dimension_semantics=("arbitrary",), vmem_limit_bytes=VMEM_LIMIT_BYTES),
        name="routed_experts",
    )(*tile_meta, tile_b, meta3, meta3, xs, w_gate, w_up, w_down)


def _final_kernel(h_ref, gffn_ref, p_ref, wt_ref, ys_ref,
                  wsg_ref, wsu_ref, wsd_ref, wpp_ref, gple_ref, wpg_ref, gfin_ref,
                  out_ref, h3_ref):
    TB = FINAL_TOKENS
    h = h_ref[...]
    h2b = _rms(h, gffn_ref[...]).astype(BF16)
    g = _dot(h2b, wsg_ref[...])
    u = _dot(h2b, wsu_ref[...])
    shared = _dot((g * _sigmoid(g) * u).astype(BF16), wsd_ref[...])
    ple = _rms(_dot(p_ref[...].astype(BF16), wpp_ref[...]), gple_ref[...])

    wt = wt_ref[...]
    for c in range(ROW_CHUNKS):
        routed_lo = routed_hi = None
        for k in range(TOP_K):
            lo, hi = _unpack_bf16_pair(ys_ref[pl.ds(k * ROW_CHUNKS + c, TB, stride=TOKEN_PITCH), :])
            wk = wt[:, k:k + 1]
            routed_lo = lo * wk if k == 0 else routed_lo + lo * wk
            routed_hi = hi * wk if k == 0 else routed_hi + hi * wk
        for routed, c0 in ((routed_lo, c * LANES), (routed_hi, HALF_MODEL + c * LANES)):
            cs = slice(c0, c0 + LANES)
            h3_ref[:, cs] = h[:, cs] + (routed + shared[:, cs])

    h3 = h3_ref[...]
    gate = _sigmoid(_dot(h3.astype(BF16), wpg_ref[...]))
    h4 = h3 + gate * ple
    out_ref[...] = _rms(h4, gfin_ref[...])


def _final_call(h, g_ffn, p, wt, ys, ws_gate, ws_up, ws_down, w_pp, g_ple, w_pg, g_final):
    TB = FINAL_TOKENS
    tokens = h.shape[0]
    const = lambda shape: pl.BlockSpec(shape, lambda i: (0,) * len(shape), pipeline_mode=pl.Buffered(1))
    tok = pl.BlockSpec((TB, D_MODEL), lambda i: (i, 0))
    return pl.pallas_call(
        _final_kernel,
        grid=(tokens // TB,),
        in_specs=[
            tok,
            const((1, D_MODEL)),
            pl.BlockSpec((TB, PLE_DIM), lambda i: (i, 0)),
            pl.BlockSpec((TB, TOP_K), lambda i: (i, 0)),
            pl.BlockSpec((TB * TOKEN_PITCH, LANES), lambda i: (i, 0)),
            const((D_MODEL, SHARED_FF)),
            const((D_MODEL, SHARED_FF)),
            const((SHARED_FF, D_MODEL)),
            const((PLE_DIM, D_MODEL)),
            const((1, D_MODEL)),
            const((D_MODEL, D_MODEL)),
            const((1, D_MODEL)),
        ],
        out_specs=tok,
        out_shape=jax.ShapeDtypeStruct((tokens, D_MODEL), F32),
        scratch_shapes=[pltpu.VMEM((TB, D_MODEL), F32)],
        compiler_params=pltpu.CompilerParams(
            dimension_semantics=("arbitrary",), vmem_limit_bytes=VMEM_LIMIT_BYTES),
        name="combine_final",
    )(h, g_ffn, p, wt, ys, ws_gate, ws_up, ws_down, w_pp, g_ple, w_pg, g_final)


def _per_step(a, tokens_per_step):
    tokens = a.shape[1]
    return a.reshape(TOP_K, tokens // tokens_per_step, tokens_per_step).transpose(1, 0, 2)


def _tile_metadata(counts, n_tiles_max):
    i32 = jnp.int32
    padded = (counts + ROW_TILE - 1) // ROW_TILE * ROW_TILE
    pend = jnp.cumsum(padded).astype(i32)
    poff = pend - padded
    n_tiles = pend[-1] // ROW_TILE
    tile = jnp.arange(n_tiles_max, dtype=i32)
    tile_v = (tile < n_tiles).astype(i32)
    tile_b = jnp.minimum(tile, jnp.maximum(n_tiles - 1, 0))
    tile_e = jnp.minimum(jnp.sum(pend[None, :] <= (tile_b * ROW_TILE)[:, None], axis=1), N_EXPERTS - 1).astype(i32)
    onehot = tile_e[:, None] == jnp.arange(N_EXPERTS, dtype=i32)[None, :]

    def lookup(table):
        return jnp.sum(jnp.where(onehot, table[None, :], 0), axis=1).astype(i32)

    tile_first = (tile_v * (tile_b * ROW_TILE == lookup(poff))).astype(i32)
    nonempty = counts > 0
    order = jnp.cumsum(nonempty.astype(i32)) - 1
    ids = jnp.where(nonempty, jnp.arange(N_EXPERTS, dtype=i32), N_EXPERTS)
    later = jnp.flip(lax.cummin(jnp.flip(ids)))
    next_e = jnp.concatenate([later[1:], jnp.full((1,), N_EXPERTS, i32)])
    next_e = jnp.where(next_e == N_EXPERTS, -1, next_e)
    tile_next = lookup(next_e)
    tile_slot = lookup(order % 2)
    tile_nvalid = jnp.clip(lookup(poff + counts) - tile_b * ROW_TILE, 0, ROW_TILE).astype(i32)
    tile_last = (tile_v * (tile == n_tiles - 1)).astype(i32)
    return poff, pend, tile_b, (tile_e, tile_v, tile_first, tile_next, tile_slot, tile_nvalid, tile_last)


def kernel(x, p, g_mix, w_in, w_pool, pool_scale, attn_sinks, rpe_table, w_out, g_ffn, w_router, router_bias,
           w_gate, w_up, w_down, ws_gate, ws_up, ws_down, w_ple_proj, g_ple, w_ple_gate, g_final):
    batch, seq, _ = x.shape
    tokens = batch * seq
    assert seq % MIX_TOKENS == 0 and tokens % DISPATCH_TOKENS == 0 and tokens % FINAL_TOKENS == 0
    n_rows = tokens * TOP_K + N_EXPERTS * ROW_TILE
    bucket = jnp.asarray(_rpe_bucket_map())
    h, h2, idx, topw, rank, cnt = _mix_call(
        x, g_mix[0][None, :], w_in[0].astype(BF16), w_pool[0].astype(BF16), pool_scale[0][None, :],
        attn_sinks[0], rpe_table, bucket, w_out[0].astype(BF16), g_ffn[0][None, :],
        w_router[0].T, router_bias[0][:, None])

    poff, pend, tile_b, tile_meta = _tile_metadata(cnt[:, 0].astype(jnp.int32), n_rows // ROW_TILE)
    experts = jnp.arange(N_EXPERTS, dtype=jnp.int32)[:, None, None]
    dest = jnp.sum(jnp.where(idx[None] == experts, poff[:, None, None], 0), axis=0) + rank
    xs, meta = _dispatch_call(_per_step(dest, DISPATCH_TOKENS), poff, pend, h2, n_rows)
    ys = _expert_call(tile_meta, tile_b, meta, xs, w_gate[0], w_up[0], w_down[0], tokens)
    out = _final_call(
        h.reshape(tokens, D_MODEL), g_ffn[0][None, :], p[0].reshape(tokens, PLE_DIM), topw.T, ys,
        ws_gate[0].astype(BF16), ws_up[0].astype(BF16), ws_down[0].astype(BF16),
        w_ple_proj[0].astype(BF16), g_ple[0][None, :], w_ple_gate[0].astype(BF16), g_final[None, :])
    return out.reshape(batch, seq, D_MODEL)
```

```python
import math

import numpy as np
import jax
import jax.numpy as jnp
from jax import lax
from jax.experimental import pallas as pl
from jax.experimental.pallas import tpu as pltpu

F32 = jnp.float32
BF16 = jnp.bfloat16
NEG_INF = float("-inf")

D_MODEL = 2048
PLE_DIM = 256
POOL_WIDTH = 1024
POOL_GROUPS = 4
POOL_GROUP_DIM = POOL_WIDTH // POOL_GROUPS
POOL_WINDOWS = (2, 4, 8, 16)
N_HEADS = 16
N_KV_HEADS = 2
HEAD_DIM = 64
HEADS_PER_KV = N_HEADS // N_KV_HEADS
ATTN_WIDTH = N_HEADS * HEAD_DIM
KV_WIDTH = N_KV_HEADS * HEAD_DIM
MIX_WIDTH = POOL_WIDTH + ATTN_WIDTH
IN_WIDTH = POOL_WIDTH + ATTN_WIDTH + 2 * KV_WIDTH
Q_OFF = POOL_WIDTH
K_OFF = POOL_WIDTH + ATTN_WIDTH
V_OFF = K_OFF + KV_WIDTH
ATTN_BLOCK = 128
WINDOW = 128
RPE_BUCKETS = 32
RPE_MAX_EXACT = RPE_BUCKETS // 2
RPE_MAX_DISTANCE = 128
N_EXPERTS = 64
TOP_K = 8
N_EXPERT_GROUPS = 8
EXPERTS_PER_GROUP = N_EXPERTS // N_EXPERT_GROUPS
TOPK_GROUPS = 4
EXPERT_FF = 512
SHARED_FF = 512
ROUTED_SCALE = 2.5
EPS = 1e-6

LANES = 128
SUBLANES = 8
VMEM_LIMIT_BYTES = 58 * 1024 * 1024

MIX_TOKENS = 512
POOL_HISTORY = 16
ROW_TILE = 256
DISPATCH_TOKENS = 512
FINAL_TOKENS = 256
WEIGHT_CAST_ROWS = 256
WEIGHT_DMA_PRIORITY = 1
HALF_MODEL = D_MODEL // 2
ROW_CHUNKS = HALF_MODEL // LANES
TOKEN_PITCH = (TOP_K + 1) * ROW_CHUNKS
U32 = jnp.uint32


def _rms(x, g):
    return x * lax.rsqrt(jnp.mean(x * x, axis=-1, keepdims=True) + EPS) * g


def _sigmoid(x):
    return 1.0 / (1.0 + jnp.exp(-x))


def _pack_bf16_pair(lo, hi):
    ulo = lax.bitcast_convert_type(lo.astype(BF16).astype(F32), U32)
    uhi = lax.bitcast_convert_type(hi.astype(BF16).astype(F32), U32)
    return (ulo >> 16) | uhi


def _unpack_bf16_pair(w):
    lo = lax.bitcast_convert_type(w << 16, F32)
    hi = lax.bitcast_convert_type(w & jnp.uint32(0xFFFF0000), F32)
    return lo, hi


def _dot(a, b):
    return jnp.dot(a, b, preferred_element_type=F32)


def _dot_nt(a, b, precision=None):
    return lax.dot_general(a, b, (((1,), (1,)), ((), ())), preferred_element_type=F32, precision=precision)


def _rpe_bucket_map():
    i = np.arange(ATTN_BLOCK)[:, None]
    j = np.arange(2 * ATTN_BLOCK)[None, :]
    dist = i + ATTN_BLOCK - j
    n = np.maximum(dist, 0)
    nf = np.maximum(n, 1).astype(np.float32)
    large = RPE_MAX_EXACT + (np.log(nf / np.float32(RPE_MAX_EXACT)) / np.float32(math.log(RPE_MAX_DISTANCE / RPE_MAX_EXACT))
                             * np.float32(RPE_BUCKETS - RPE_MAX_EXACT)).astype(np.int32)
    large = np.minimum(large, RPE_BUCKETS - 1)
    bucket = np.where(n < RPE_MAX_EXACT, n, large)
    valid = (dist >= 0) & (dist < WINDOW)
    return np.where(valid, bucket, -1).astype(np.int32)


def _mix_kernel(x_ref, gmix_ref, win_ref, wpool_ref, pscale_ref, sinks_ref, rpe_ref, bucket_ref,
                wout_ref, gffn_ref, wrt_ref, rbias_ref,
                h_ref, h2_ref, idx_ref, topw_ref, rank_ref, cnt_ref,
                ubuf, kbuf, vbuf, bias_buf, ybuf, cnt_acc):
    TS = MIX_TOKENS
    b = pl.program_id(0)
    s = pl.program_id(1)

    @pl.when((b == 0) & (s == 0))
    def _init():
        bucket = bucket_ref[...]
        for h in range(N_HEADS):
            acc = jnp.full((ATTN_BLOCK, 2 * ATTN_BLOCK), NEG_INF, F32)
            for bk in range(RPE_BUCKETS):
                acc = jnp.where(bucket == bk, rpe_ref[bk, h], acc)
            bias_buf[h * ATTN_BLOCK:(h + 1) * ATTN_BLOCK, :] = acc
        cnt_acc[...] = jnp.zeros_like(cnt_acc)

    @pl.when(s == 0)
    def _reset_history():
        ubuf[0:POOL_HISTORY, :] = jnp.zeros((POOL_HISTORY, POOL_WIDTH), F32)
        kbuf[0:ATTN_BLOCK, :] = jnp.zeros((ATTN_BLOCK, KV_WIDTH), BF16)
        vbuf[0:ATTN_BLOCK, :] = jnp.zeros((ATTN_BLOCK, KV_WIDTH), BF16)

    x = x_ref[...]
    a = _rms(x, gmix_ref[...]).astype(BF16)
    z = _dot(a, win_ref[...])

    ubuf[POOL_HISTORY:POOL_HISTORY + TS, :] = z[:, 0:POOL_WIDTH]
    pos = s * TS + lax.broadcasted_iota(jnp.int32, (TS, 1), 0)
    for gi, w in enumerate(POOL_WINDOWS):
        c0, c1 = gi * POOL_GROUP_DIM, (gi + 1) * POOL_GROUP_DIM
        e = ubuf[:, c0:c1]
        shift = 1
        while shift < w:
            e = e + pltpu.roll(e, shift, axis=0)
            shift *= 2
        wsum = e[POOL_HISTORY:, :]
        count = jnp.minimum(pos + 1, w).astype(F32)
        pooled = wsum / count - z[:, c0:c1]
        yp = _dot(pooled.astype(BF16), wpool_ref[gi]) * pscale_ref[:, c0:c1]
        ybuf[:, c0:c1] = yp.astype(BF16)
    ubuf[0:POOL_HISTORY, :] = ubuf[TS:TS + POOL_HISTORY, :]

    kbuf[ATTN_BLOCK:ATTN_BLOCK + TS, :] = z[:, K_OFF:K_OFF + KV_WIDTH].astype(BF16)
    vbuf[ATTN_BLOCK:ATTN_BLOCK + TS, :] = z[:, V_OFF:V_OFF + KV_WIDTH].astype(BF16)
    lane = lax.broadcasted_iota(jnp.int32, (ATTN_BLOCK, LANES), 1)
    low_half = lane < HEAD_DIM
    high_half = jnp.logical_not(low_half)
    col = lax.broadcasted_iota(jnp.int32, (1, 2 * ATTN_BLOCK), 1)
    first_mask = jnp.where((col < ATTN_BLOCK) & (s == 0), NEG_INF, 0.0).astype(F32)
    for sb in range(TS // ATTN_BLOCK):
        r0 = sb * ATTN_BLOCK
        kband = kbuf[r0:r0 + 2 * ATTN_BLOCK, :]
        vband = vbuf[r0:r0 + 2 * ATTN_BLOCK, :]
        for p in range(N_HEADS // 2):
            qp = z[r0:r0 + ATTN_BLOCK, Q_OFF + p * LANES:Q_OFF + (p + 1) * LANES] * (HEAD_DIM ** -0.5)
            qr = pltpu.roll(qp, HEAD_DIM, axis=1)
            kvh = (2 * p) // HEADS_PER_KV
            kv_lanes = low_half if kvh == 0 else high_half
            outs = []
            for par in range(2):
                h = 2 * p + par
                qh = jnp.where(kv_lanes, qp if par == kvh else qr, 0.0).astype(BF16)
                lg = _dot_nt(qh, kband) + bias_buf[h * ATTN_BLOCK:(h + 1) * ATTN_BLOCK, :]
                if sb == 0:
                    lg = lg + first_mask
                sink = sinks_ref[h]
                m = jnp.maximum(jnp.max(lg, axis=-1, keepdims=True), sink)
                pe = jnp.exp(lg - m)
                den = jnp.sum(pe, axis=-1, keepdims=True) + jnp.exp(sink - m)
                outs.append(_dot(pe.astype(BF16), vband) / den)
            if kvh == 0:
                pair = jnp.where(low_half, outs[0], pltpu.roll(outs[1], HEAD_DIM, axis=1))
            else:
                pair = jnp.where(low_half, pltpu.roll(outs[0], HEAD_DIM, axis=1), outs[1])
            ybuf[r0:r0 + ATTN_BLOCK, POOL_WIDTH + p * LANES:POOL_WIDTH + (p + 1) * LANES] = pair.astype(BF16)
    kbuf[0:ATTN_BLOCK, :] = kbuf[TS:TS + ATTN_BLOCK, :]
    vbuf[0:ATTN_BLOCK, :] = vbuf[TS:TS + ATTN_BLOCK, :]

    h = x + _dot(ybuf[...], wout_ref[...])
    h_ref[...] = h
    h2 = _rms(h, gffn_ref[...])
    for c in range(ROW_CHUNKS):
        h2_ref[pl.ds(c, TS, stride=ROW_CHUNKS), :] = _pack_bf16_pair(
            h2[:, c * LANES:(c + 1) * LANES], h2[:, HALF_MODEL + c * LANES:HALF_MODEL + (c + 1) * LANES])

    logits = _dot_nt(wrt_ref[...], h2, precision=lax.Precision.HIGHEST)
    scores = _sigmoid(logits)
    biased = scores + rbias_ref[...]
    erow = lax.broadcasted_iota(jnp.int32, (N_EXPERTS, TS), 0)
    grow = lax.broadcasted_iota(jnp.int32, (EXPERTS_PER_GROUP, TS), 0)
    group_scores = []
    for g in range(N_EXPERT_GROUPS):
        blk = biased[g * EXPERTS_PER_GROUP:(g + 1) * EXPERTS_PER_GROUP, :]
        m1 = jnp.max(blk, axis=0, keepdims=True)
        i1 = jnp.min(jnp.where(blk == m1, grow, EXPERTS_PER_GROUP), axis=0, keepdims=True)
        m2 = jnp.max(jnp.where(grow == i1, NEG_INF, blk), axis=0, keepdims=True)
        group_scores.append(m1 + m2)
    cur = jnp.concatenate(group_scores, axis=0)
    gsel = jnp.zeros((N_EXPERT_GROUPS, TS), jnp.bool_)
    for _ in range(TOPK_GROUPS):
        m = jnp.max(cur, axis=0, keepdims=True)
        i = jnp.min(jnp.where(cur == m, grow, N_EXPERT_GROUPS), axis=0, keepdims=True)
        hit = grow == i
        gsel = jnp.logical_or(gsel, hit)
        cur = jnp.where(hit, NEG_INF, cur)
    gmask = jnp.concatenate(
        [jnp.broadcast_to(gsel[g:g + 1, :], (EXPERTS_PER_GROUP, TS)) for g in range(N_EXPERT_GROUPS)], axis=0)
    masked = jnp.where(gmask, biased, NEG_INF)
    sel = jnp.zeros((N_EXPERTS, TS), jnp.bool_)
    idxs, ws = [], []
    for _ in range(TOP_K):
        m = jnp.max(masked, axis=0, keepdims=True)
        i = jnp.min(jnp.where(masked == m, erow, N_EXPERTS), axis=0, keepdims=True)
        hit = erow == i
        idxs.append(i)
        ws.append(jnp.sum(jnp.where(hit, scores, 0.0), axis=0, keepdims=True))
        sel = jnp.logical_or(sel, hit)
        masked = jnp.where(hit, NEG_INF, masked)
    wtot = ws[0]
    for wk in ws[1:]:
        wtot = wtot + wk
    idx_ref[...] = jnp.concatenate(idxs, axis=0)
    topw_ref[...] = jnp.concatenate([wk / wtot * ROUTED_SCALE for wk in ws], axis=0)

    self32 = sel.astype(F32)
    ri = lax.broadcasted_iota(jnp.int32, (TS, TS), 0)
    ci = lax.broadcasted_iota(jnp.int32, (TS, TS), 1)
    before = (ri < ci).astype(BF16)
    running = _dot(self32.astype(BF16), before) + cnt_acc[:, 0:1]
    rank_ref[...] = jnp.concatenate(
        [jnp.sum(jnp.where(erow == i, running, 0.0), axis=0, keepdims=True) for i in idxs], axis=0).astype(jnp.int32)
    cnt_acc[...] = cnt_acc[...] + jnp.sum(self32, axis=1, keepdims=True)
    cnt_ref[...] = cnt_acc[...]


def _mix_call(x, g_mix, w_in, w_pool, pool_scale, sinks, rpe_table, bucket, w_out, g_ffn, w_rt, r_bias):
    batch, seq, _ = x.shape
    tokens = batch * seq
    TS = MIX_TOKENS
    ns = seq // TS
    const = lambda shape: pl.BlockSpec(shape, lambda b, s: (0,) * len(shape), pipeline_mode=pl.Buffered(1))
    smem = pl.BlockSpec(memory_space=pltpu.SMEM)
    tok3 = pl.BlockSpec((None, TS, D_MODEL), lambda b, s: (b, s, 0))
    lane_blk = pl.BlockSpec((TOP_K, TS), lambda b, s: (0, b * ns + s))
    return pl.pallas_call(
        _mix_kernel,
        grid=(batch, ns),
        in_specs=[
            tok3,
            const((1, D_MODEL)),
            const((D_MODEL, IN_WIDTH)),
            const((POOL_GROUPS, POOL_GROUP_DIM, POOL_GROUP_DIM)),
            const((1, POOL_WIDTH)),
            smem,
            smem,
            const((ATTN_BLOCK, 2 * ATTN_BLOCK)),
            const((MIX_WIDTH, D_MODEL)),
            const((1, D_MODEL)),
            const((N_EXPERTS, D_MODEL)),
            const((N_EXPERTS, 1)),
        ],
        out_specs=[
            pl.BlockSpec((None, TS, D_MODEL), lambda b, s: (b, s, 0), pipeline_mode=pl.Buffered(1)),
            pl.BlockSpec((TS * ROW_CHUNKS, LANES), lambda b, s: (b * ns + s, 0), pipeline_mode=pl.Buffered(1)),
            lane_blk,
            lane_blk,
            lane_blk,
            pl.BlockSpec((N_EXPERTS, LANES), lambda b, s: (0, 0)),
        ],
        out_shape=[
            jax.ShapeDtypeStruct((batch, seq, D_MODEL), F32),
            jax.ShapeDtypeStruct((tokens * ROW_CHUNKS, LANES), U32),
            jax.ShapeDtypeStruct((TOP_K, tokens), jnp.int32),
            jax.ShapeDtypeStruct((TOP_K, tokens), F32),
            jax.ShapeDtypeStruct((TOP_K, tokens), jnp.int32),
            jax.ShapeDtypeStruct((N_EXPERTS, LANES), F32),
        ],
        scratch_shapes=[
            pltpu.VMEM((POOL_HISTORY + TS, POOL_WIDTH), F32),
            pltpu.VMEM((ATTN_BLOCK + TS, KV_WIDTH), BF16),
            pltpu.VMEM((ATTN_BLOCK + TS, KV_WIDTH), BF16),
            pltpu.VMEM((N_HEADS * ATTN_BLOCK, 2 * ATTN_BLOCK), F32),
            pltpu.VMEM((TS, MIX_WIDTH), BF16),
            pltpu.VMEM((N_EXPERTS, LANES), F32),
        ],
        compiler_params=pltpu.CompilerParams(
            dimension_semantics=("arbitrary", "arbitrary"), vmem_limit_bytes=VMEM_LIMIT_BYTES),
        name="mix_router",
    )(x, g_mix, w_in, w_pool, pool_scale, sinks, rpe_table, bucket, w_out, g_ffn, w_rt, r_bias)


def _dispatch_kernel(dest_ref, poff_ref, pend_ref, h2_ref, xs_hbm, meta_ref, zbuf, sem, zsem):
    step = pl.program_id(0)
    base = step * DISPATCH_TOKENS

    def zero_copy(e):
        start = pl.multiple_of((pend_ref[e] - ROW_TILE) * ROW_CHUNKS, ROW_CHUNKS)
        return pltpu.make_async_copy(zbuf, xs_hbm.at[pl.ds(start, ROW_TILE * ROW_CHUNKS)], zsem)

    @pl.when(step == 0)
    def _zero_tails():
        zbuf[...] = jnp.zeros_like(zbuf)

        def start(e, c):
            @pl.when(pend_ref[e] > poff_ref[e])
            def _():
                zero_copy(e).start()
            return c

        def wait(e, c):
            @pl.when(pend_ref[e] > poff_ref[e])
            def _():
                zero_copy(e).wait()
            return c

        lax.fori_loop(0, N_EXPERTS, start, 0)
        lax.fori_loop(0, N_EXPERTS, wait, 0)

    def row_copy(t, row):
        src = pl.multiple_of(t * ROW_CHUNKS, ROW_CHUNKS)
        dest = pl.multiple_of(row * ROW_CHUNKS, ROW_CHUNKS)
        return pltpu.make_async_copy(h2_ref.at[pl.ds(src, ROW_CHUNKS)], xs_hbm.at[pl.ds(dest, ROW_CHUNKS)], sem)

    def start_tok(t, c):
        rows = [dest_ref[k, t] for k in range(TOP_K)]
        slot0 = (base + t) * TOP_K
        for k in range(TOP_K):
            row_copy(t, rows[k]).start(priority=k % 2)
        for k in range(TOP_K):
            meta_ref[rows[k]] = slot0 + k
        return c

    def wait_tok(t, c):
        for k in range(TOP_K):
            row_copy(t, 0).wait()
        return c

    lax.fori_loop(0, DISPATCH_TOKENS, start_tok, 0, unroll=4)
    lax.fori_loop(0, DISPATCH_TOKENS, wait_tok, 0)


def _dispatch_call(dest3, poff, pend, h2, n_rows):
    smem_blk = pl.BlockSpec((None, TOP_K, DISPATCH_TOKENS), lambda i: (i, 0, 0), memory_space=pltpu.SMEM)
    smem = pl.BlockSpec(memory_space=pltpu.SMEM)
    return pl.pallas_call(
        _dispatch_kernel,
        grid=(dest3.shape[0],),
        in_specs=[smem_blk, smem, smem,
                  pl.BlockSpec((DISPATCH_TOKENS * ROW_CHUNKS, LANES), lambda i: (i, 0))],
        out_specs=[pl.BlockSpec(memory_space=pl.ANY), smem],
        out_shape=[jax.ShapeDtypeStruct((n_rows * ROW_CHUNKS, LANES), U32),
                   jax.ShapeDtypeStruct((n_rows,), jnp.int32)],
        scratch_shapes=[
            pltpu.VMEM((ROW_TILE * ROW_CHUNKS, LANES), U32),
            pltpu.SemaphoreType.DMA,
            pltpu.SemaphoreType.DMA,
        ],
        compiler_params=pltpu.CompilerParams(dimension_semantics=("arbitrary",)),
        name="dispatch_rows",
    )(dest3, poff, pend, h2)


def _expert_kernel(te_ref, tv_ref, tfirst_ref, tnext_ref, tslot_ref, tnv_ref, tlast_ref,
                   meta_ref, meta_prev_ref, xs_ref, wg_hbm, wu_hbm, wd_hbm, ys_hbm,
                   yo, xb_ref, wg_stage, wu_stage, wd_stage, wg_bf, wu_bf, wd_bf, wsems, ssems,
                   *, dump_row):
    i = pl.program_id(0)
    slot = lax.rem(i, 2)
    tile_rows = ROW_TILE * ROW_CHUNKS

    def scatter_copy(m_ref, n_valid, s, r):
        m = m_ref[0, r]
        dst = jnp.where(r < n_valid, (m + (m >> 3)) * ROW_CHUNKS, dump_row + r * ROW_CHUNKS)
        src = s * tile_rows + r * ROW_CHUNKS
        return pltpu.make_async_copy(yo.at[pl.ds(pl.multiple_of(src, ROW_CHUNKS), ROW_CHUNKS)],
                                     ys_hbm.at[pl.ds(pl.multiple_of(dst, ROW_CHUNKS), ROW_CHUNKS)], ssems.at[s])

    def start_rows(make, m_ref, n_valid, s):
        def body(r, c):
            make(m_ref, n_valid, s, r).start(priority=0)
            make(m_ref, n_valid, s, r + ROW_TILE // 2).start(priority=1)
            return c
        lax.fori_loop(0, ROW_TILE // 2, body, 0, unroll=8)

    def start_rows_inline(make, m_ref, n_valid, s):
        for r in range(ROW_TILE):
            make(m_ref, n_valid, s, r).start(priority=r % 2)

    def wait_rows(src_ref, dst_ref, sem):
        for _ in range(ROW_TILE):
            pltpu.make_async_copy(src_ref.at[pl.ds(0, ROW_CHUNKS)], dst_ref.at[pl.ds(0, ROW_CHUNKS)], sem).wait()

    def weight_copies(e, s):
        return (pltpu.make_async_copy(wg_hbm.at[e], wg_stage.at[s], wsems.at[s, 0]),
                pltpu.make_async_copy(wu_hbm.at[e], wu_stage.at[s], wsems.at[s, 1]),
                pltpu.make_async_copy(wd_hbm.at[e], wd_stage.at[s], wsems.at[s, 2]))

    valid = tv_ref[i] == 1

    @pl.when((i == 0) & valid)
    def _prologue():
        for cp in weight_copies(te_ref[0], tslot_ref[0]):
            cp.start(priority=WEIGHT_DMA_PRIORITY)
        yo[...] = jnp.zeros_like(yo)
        start_rows(scatter_copy, meta_ref, 0, 0)

    n_valid_prev = jnp.where(i >= 1, tnv_ref[jnp.maximum(i - 1, 0)], 0)

    @pl.when(valid & (tfirst_ref[i] == 1))
    def _new_expert():
        ws = tslot_ref[i]
        for cp in weight_copies(te_ref[i], ws):
            cp.wait()

        @pl.when(tnext_ref[i] >= 0)
        def _():
            for cp in weight_copies(tnext_ref[i], 1 - ws):
                cp.start(priority=WEIGHT_DMA_PRIORITY)

        def cast_in(j, c):
            r = pl.multiple_of(j * WEIGHT_CAST_ROWS, WEIGHT_CAST_ROWS)
            wg_bf[pl.ds(r, WEIGHT_CAST_ROWS), :] = wg_stage[ws, pl.ds(r, WEIGHT_CAST_ROWS), :].astype(BF16)
            wu_bf[pl.ds(r, WEIGHT_CAST_ROWS), :] = wu_stage[ws, pl.ds(r, WEIGHT_CAST_ROWS), :].astype(BF16)
            return c

        def cast_out(j, c):
            r = pl.multiple_of(j * (WEIGHT_CAST_ROWS // 4), WEIGHT_CAST_ROWS // 4)
            wd_bf[pl.ds(r, WEIGHT_CAST_ROWS // 4), :] = wd_stage[ws, pl.ds(r, WEIGHT_CAST_ROWS // 4), :].astype(BF16)
            return c

        lax.fori_loop(0, D_MODEL // WEIGHT_CAST_ROWS, cast_in, 0)
        lax.fori_loop(0, EXPERT_FF // (WEIGHT_CAST_ROWS // 4), cast_out, 0)

    @pl.when(valid)
    def _tile():
        base = pl.multiple_of(slot * tile_rows, tile_rows)
        for c in range(ROW_CHUNKS):
            lo, hi = _unpack_bf16_pair(xs_ref[pl.ds(c, ROW_TILE, stride=ROW_CHUNKS), :])
            xb_ref[:, c * LANES:(c + 1) * LANES] = lo.astype(BF16)
            xb_ref[:, HALF_MODEL + c * LANES:HALF_MODEL + (c + 1) * LANES] = hi.astype(BF16)
        start_rows_inline(scatter_copy, meta_prev_ref, n_valid_prev, 1 - slot)
        xb = xb_ref[...]
        g = _dot(xb, wg_bf[...])
        u = _dot(xb, wu_bf[...])
        hb = (g * _sigmoid(g) * u).astype(BF16)
        y = _dot(hb, wd_bf[...])
        wait_rows(yo, ys_hbm, ssems.at[slot])
        for c in range(ROW_CHUNKS):
            yo[pl.ds(base + c, ROW_TILE, stride=ROW_CHUNKS), :] = _pack_bf16_pair(
                y[:, c * LANES:(c + 1) * LANES], y[:, HALF_MODEL + c * LANES:HALF_MODEL + (c + 1) * LANES])

    @pl.when(valid & (tlast_ref[i] == 1))
    def _drain():
        start_rows(scatter_copy, meta_ref, tnv_ref[i], slot)
        wait_rows(yo, ys_hbm, ssems.at[1 - slot])
        wait_rows(yo, ys_hbm, ssems.at[slot])


def _expert_call(tile_meta, tile_b, meta, xs, w_gate, w_up, w_down, tokens):
    n_tiles = tile_b.shape[0]
    meta3 = meta.reshape(n_tiles, 1, ROW_TILE)
    dump_row = tokens * TOKEN_PITCH
    meta_blk = pl.BlockSpec((None, 1, ROW_TILE), lambda i, te, tv, tf, tn, ts, tnv, tl, tb: (tb[i], 0, 0),
                            memory_space=pltpu.SMEM)
    row_blk = pl.BlockSpec((ROW_TILE * ROW_CHUNKS, LANES), lambda i, te, tv, tf, tn, ts, tnv, tl, tb: (tb[i], 0))
    meta_prev_blk = pl.BlockSpec(
        (None, 1, ROW_TILE), lambda i, te, tv, tf, tn, ts, tnv, tl, tb: (tb[jnp.maximum(i - 1, 0)], 0, 0),
        memory_space=pltpu.SMEM)
    any_spec = pl.BlockSpec(memory_space=pl.ANY)

    def body(te, tv, tf, tn, ts, tnv, tl, tb, *refs):
        _expert_kernel(te, tv, tf, tn, ts, tnv, tl, *refs, dump_row=dump_row)

    return pl.pallas_call(
        body,
        grid_spec=pltpu.PrefetchScalarGridSpec(
            num_scalar_prefetch=len(tile_meta) + 1,
            grid=(n_tiles,),
            in_specs=[meta_blk, meta_prev_blk, row_blk, any_spec, any_spec, any_spec],
            out_specs=any_spec,
            scratch_shapes=[
                pltpu.VMEM((2 * ROW_TILE * ROW_CHUNKS, LANES), U32),
                pltpu.VMEM((ROW_TILE, D_MODEL), BF16),
                pltpu.VMEM((2, D_MODEL, EXPERT_FF), F32),
                pltpu.VMEM((2, D_MODEL, EXPERT_FF), F32),
                pltpu.VMEM((2, EXPERT_FF, D_MODEL), F32),
                pltpu.VMEM((D_MODEL, EXPERT_FF), BF16),
                pltpu.VMEM((D_MODEL, EXPERT_FF), BF16),
                pltpu.VMEM((EXPERT_FF, D_MODEL), BF16),
                pltpu.SemaphoreType.DMA((2, 3)),
                pltpu.SemaphoreType.DMA((2,)),
            ],
        ),
        out_shape=jax.ShapeDtypeStruct((dump_row + ROW_TILE * ROW_CHUNKS, LANES), U32),
        compiler_params=pltpu.CompilerParams(
            dimension_semantics=("arbitrary",), vmem_limit_bytes=VMEM_LIMIT_BYTES),
        name="routed_experts",
    )(*tile_meta, tile_b, meta3, meta3, xs, w_gate, w_up, w_down)


def _final_kernel(h_ref, gffn_ref, p_ref, wt_ref, ys_ref,
                  wsg_ref, wsu_ref, wsd_ref, wpp_ref, gple_ref, wpg_ref, gfin_ref,
                  out_ref, h3_ref):
    TB = FINAL_TOKENS
    h = h_ref[...]
    h2b = _rms(h, gffn_ref[...]).astype(BF16)
    g = _dot(h2b, wsg_ref[...])
    u = _dot(h2b, wsu_ref[...])
    shared = _dot((g * _sigmoid(g) * u).astype(BF16), wsd_ref[...])
    ple = _rms(_dot(p_ref[...].astype(BF16), wpp_ref[...]), gple_ref[...])

    wt = wt_ref[...]
    for c in range(ROW_CHUNKS):
        routed_lo = routed_hi = None
        for k in range(TOP_K):
            lo, hi = _unpack_bf16_pair(ys_ref[pl.ds(k * ROW_CHUNKS + c, TB, stride=TOKEN_PITCH), :])
            wk = wt[:, k:k + 1]
            routed_lo = lo * wk if k == 0 else routed_lo + lo * wk
            routed_hi = hi * wk if k == 0 else routed_hi + hi * wk
        for routed, c0 in ((routed_lo, c * LANES), (routed_hi, HALF_MODEL + c * LANES)):
            cs = slice(c0, c0 + LANES)
            h3_ref[:, cs] = h[:, cs] + (routed + shared[:, cs])

    h3 = h3_ref[...]
    gate = _sigmoid(_dot(h3.astype(BF16), wpg_ref[...]))
    h4 = h3 + gate * ple
    out_ref[...] = _rms(h4, gfin_ref[...])


def _final_call(h, g_ffn, p, wt, ys, ws_gate, ws_up, ws_down, w_pp, g_ple, w_pg, g_final):
    TB = FINAL_TOKENS
    tokens = h.shape[0]
    const = lambda shape: pl.BlockSpec(shape, lambda i: (0,) * len(shape), pipeline_mode=pl.Buffered(1))
    tok = pl.BlockSpec((TB, D_MODEL), lambda i: (i, 0))
    return pl.pallas_call(
        _final_kernel,
        grid=(tokens // TB,),
        in_specs=[
            tok,
            const((1, D_MODEL)),
            pl.BlockSpec((TB, PLE_DIM), lambda i: (i, 0)),
            pl.BlockSpec((TB, TOP_K), lambda i: (i, 0)),
            pl.BlockSpec((TB * TOKEN_PITCH, LANES), lambda i: (i, 0)),
            const((D_MODEL, SHARED_FF)),
            const((D_MODEL, SHARED_FF)),
            const((SHARED_FF, D_MODEL)),
            const((PLE_DIM, D_MODEL)),
            const((1, D_MODEL)),
            const((D_MODEL, D_MODEL)),
            const((1, D_MODEL)),
        ],
        out_specs=tok,
        out_shape=jax.ShapeDtypeStruct((tokens, D_MODEL), F32),
        scratch_shapes=[pltpu.VMEM((TB, D_MODEL), F32)],
        compiler_params=pltpu.CompilerParams(
            dimension_semantics=("arbitrary",), vmem_limit_bytes=VMEM_LIMIT_BYTES),
        name="combine_final",
    )(h, g_ffn, p, wt, ys, ws_gate, ws_up, ws_down, w_pp, g_ple, w_pg, g_final)


def _per_step(a, tokens_per_step):
    tokens = a.shape[1]
    return a.reshape(TOP_K, tokens // tokens_per_step, tokens_per_step).transpose(1, 0, 2)


def _tile_metadata(counts, n_tiles_max):
    i32 = jnp.int32
    padded = (counts + ROW_TILE - 1) // ROW_TILE * ROW_TILE
    pend = jnp.cumsum(padded).astype(i32)
    poff = pend - padded
    n_tiles = pend[-1] // ROW_TILE
    tile = jnp.arange(n_tiles_max, dtype=i32)
    tile_v = (tile < n_tiles).astype(i32)
    tile_b = jnp.minimum(tile, jnp.maximum(n_tiles - 1, 0))
    tile_e = jnp.minimum(jnp.sum(pend[None, :] <= (tile_b * ROW_TILE)[:, None], axis=1), N_EXPERTS - 1).astype(i32)
    onehot = tile_e[:, None] == jnp.arange(N_EXPERTS, dtype=i32)[None, :]

    def lookup(table):
        return jnp.sum(jnp.where(onehot, table[None, :], 0), axis=1).astype(i32)

    tile_first = (tile_v * (tile_b * ROW_TILE == lookup(poff))).astype(i32)
    nonempty = counts > 0
    order = jnp.cumsum(nonempty.astype(i32)) - 1
    ids = jnp.where(nonempty, jnp.arange(N_EXPERTS, dtype=i32), N_EXPERTS)
    later = jnp.flip(lax.cummin(jnp.flip(ids)))
    next_e = jnp.concatenate([later[1:], jnp.full((1,), N_EXPERTS, i32)])
    next_e = jnp.where(next_e == N_EXPERTS, -1, next_e)
    tile_next = lookup(next_e)
    tile_slot = lookup(order % 2)
    tile_nvalid = jnp.clip(lookup(poff + counts) - tile_b * ROW_TILE, 0, ROW_TILE).astype(i32)
    tile_last = (tile_v * (tile == n_tiles - 1)).astype(i32)
    return poff, pend, tile_b, (tile_e, tile_v, tile_first, tile_next, tile_slot, tile_nvalid, tile_last)


def kernel(x, p, g_mix, w_in, w_pool, pool_scale, attn_sinks, rpe_table, w_out, g_ffn, w_router, router_bias,
           w_gate, w_up, w_down, ws_gate, ws_up, ws_down, w_ple_proj, g_ple, w_ple_gate, g_final):
    batch, seq, _ = x.shape
    tokens = batch * seq
    assert seq % MIX_TOKENS == 0 and tokens % DISPATCH_TOKENS == 0 and tokens % FINAL_TOKENS == 0
    n_rows = tokens * TOP_K + N_EXPERTS * ROW_TILE
    bucket = jnp.asarray(_rpe_bucket_map())
    h, h2, idx, topw, rank, cnt = _mix_call(
        x, g_mix[0][None, :], w_in[0].astype(BF16), w_pool[0].astype(BF16), pool_scale[0][None, :],
        attn_sinks[0], rpe_table, bucket, w_out[0].astype(BF16), g_ffn[0][None, :],
        w_router[0].T, router_bias[0][:, None])

    poff, pend, tile_b, tile_meta = _tile_metadata(cnt[:, 0].astype(jnp.int32), n_rows // ROW_TILE)
    experts = jnp.arange(N_EXPERTS, dtype=jnp.int32)[:, None, None]
    dest = jnp.sum(jnp.where(idx[None] == experts, poff[:, None, None], 0), axis=0) + rank
    xs, meta = _dispatch_call(_per_step(dest, DISPATCH_TOKENS), poff, pend, h2, n_rows)
    ys = _expert_call(tile_meta, tile_b, meta, xs, w_gate[0], w_up[0], w_down[0], tokens)
    out = _final_call(
        h.reshape(tokens, D_MODEL), g_ffn[0][None, :], p[0].reshape(tokens, PLE_DIM), topw.T, ys,
        ws_gate[0].astype(BF16), ws_up[0].astype(BF16), ws_down[0].astype(BF16),
        w_ple_proj[0].astype(BF16), g_ple[0][None, :], w_ple_gate[0].astype(BF16), g_final[None, :])
    return out.reshape(batch, seq, D_MODEL)
```

```python
import math

import numpy as np
import jax
import jax.numpy as jnp
from jax import lax
from jax.experimental import pallas as pl
from jax.experimental.pallas import tpu as pltpu

F32 = jnp.float32
BF16 = jnp.bfloat16
NEG_INF = float("-inf")

D_MODEL = 2048
PLE_DIM = 256
POOL_WIDTH = 1024
POOL_GROUPS = 4
POOL_GROUP_DIM = POOL_WIDTH // POOL_GROUPS
POOL_WINDOWS = (2, 4, 8, 16)
N_HEADS = 16
N_KV_HEADS = 2
HEAD_DIM = 64
HEADS_PER_KV = N_HEADS // N_KV_HEADS
ATTN_WIDTH = N_HEADS * HEAD_DIM
KV_WIDTH = N_KV_HEADS * HEAD_DIM
MIX_WIDTH = POOL_WIDTH + ATTN_WIDTH
IN_WIDTH = POOL_WIDTH + ATTN_WIDTH + 2 * KV_WIDTH
Q_OFF = POOL_WIDTH
K_OFF = POOL_WIDTH + ATTN_WIDTH
V_OFF = K_OFF + KV_WIDTH
ATTN_BLOCK = 128
WINDOW = 128
RPE_BUCKETS = 32
RPE_MAX_EXACT = RPE_BUCKETS // 2
RPE_MAX_DISTANCE = 128
N_EXPERTS = 64
TOP_K = 8
N_EXPERT_GROUPS = 8
EXPERTS_PER_GROUP = N_EXPERTS // N_EXPERT_GROUPS
TOPK_GROUPS = 4
EXPERT_FF = 512
SHARED_FF = 512
ROUTED_SCALE = 2.5
EPS = 1e-6

LANES = 128
SUBLANES = 8
VMEM_LIMIT_BYTES = 58 * 1024 * 1024

MIX_TOKENS = 512
POOL_HISTORY = 16
ROW_TILE = 256
DISPATCH_TOKENS = 512
WAIT_CHUNK = 64
FINAL_TOKENS = 256
WEIGHT_CAST_ROWS = 256
WEIGHT_DMA_PRIORITY = 1
HALF_MODEL = D_MODEL // 2
ROW_CHUNKS = HALF_MODEL // LANES
TOKEN_PITCH = (TOP_K + 1) * ROW_CHUNKS
U32 = jnp.uint32


def _rms(x, g):
    return x * lax.rsqrt(jnp.mean(x * x, axis=-1, keepdims=True) + EPS) * g


def _sigmoid(x):
    return 1.0 / (1.0 + jnp.exp(-x))


def _pack_bf16_pair(lo, hi):
    ulo = lax.bitcast_convert_type(lo.astype(BF16).astype(F32), U32)
    uhi = lax.bitcast_convert_type(hi.astype(BF16).astype(F32), U32)
    return (ulo >> 16) | uhi


def _unpack_bf16_pair(w):
    lo = lax.bitcast_convert_type(w << 16, F32)
    hi = lax.bitcast_convert_type(w & jnp.uint32(0xFFFF0000), F32)
    return lo, hi


def _dot(a, b):
    return jnp.dot(a, b, preferred_element_type=F32)


def _dot_nt(a, b, precision=None):
    return lax.dot_general(a, b, (((1,), (1,)), ((), ())), preferred_element_type=F32, precision=precision)


def _rpe_bucket_map():
    i = np.arange(ATTN_BLOCK)[:, None]
    j = np.arange(2 * ATTN_BLOCK)[None, :]
    dist = i + ATTN_BLOCK - j
    n = np.maximum(dist, 0)
    nf = np.maximum(n, 1).astype(np.float32)
    large = RPE_MAX_EXACT + (np.log(nf / np.float32(RPE_MAX_EXACT)) / np.float32(math.log(RPE_MAX_DISTANCE / RPE_MAX_EXACT))
                             * np.float32(RPE_BUCKETS - RPE_MAX_EXACT)).astype(np.int32)
    large = np.minimum(large, RPE_BUCKETS - 1)
    bucket = np.where(n < RPE_MAX_EXACT, n, large)
    valid = (dist >= 0) & (dist < WINDOW)
    return np.where(valid, bucket, -1).astype(np.int32)


def _mix_kernel(x_ref, gmix_ref, win_ref, wpool_ref, pscale_ref, sinks_ref, rpe_ref, bucket_ref,
                wout_ref, gffn_ref, wrt_ref, rbias_ref,
                h_ref, h2_ref, idx_ref, topw_ref, rank_ref, cnt_ref,
                ubuf, kbuf, vbuf, bias_buf, ybuf, cnt_acc):
    TS = MIX_TOKENS
    b = pl.program_id(0)
    s = pl.program_id(1)

    @pl.when((b == 0) & (s == 0))
    def _init():
        bucket = bucket_ref[...]
        for h in range(N_HEADS):
            acc = jnp.full((ATTN_BLOCK, 2 * ATTN_BLOCK), NEG_INF, F32)
            for bk in range(RPE_BUCKETS):
                acc = jnp.where(bucket == bk, rpe_ref[bk, h], acc)
            bias_buf[h * ATTN_BLOCK:(h + 1) * ATTN_BLOCK, :] = acc
        cnt_acc[...] = jnp.zeros_like(cnt_acc)

    @pl.when(s == 0)
    def _reset_history():
        ubuf[0:POOL_HISTORY, :] = jnp.zeros((POOL_HISTORY, POOL_WIDTH), F32)
        kbuf[0:ATTN_BLOCK, :] = jnp.zeros((ATTN_BLOCK, KV_WIDTH), BF16)
        vbuf[0:ATTN_BLOCK, :] = jnp.zeros((ATTN_BLOCK, KV_WIDTH), BF16)

    x = x_ref[...]
    a = _rms(x, gmix_ref[...]).astype(BF16)
    z = _dot(a, win_ref[...])

    ubuf[POOL_HISTORY:POOL_HISTORY + TS, :] = z[:, 0:POOL_WIDTH]
    pos = s * TS + lax.broadcasted_iota(jnp.int32, (TS, 1), 0)
    for gi, w in enumerate(POOL_WINDOWS):
        c0, c1 = gi * POOL_GROUP_DIM, (gi + 1) * POOL_GROUP_DIM
        e = ubuf[:, c0:c1]
        shift = 1
        while shift < w:
            e = e + pltpu.roll(e, shift, axis=0)
            shift *= 2
        wsum = e[POOL_HISTORY:, :]
        count = jnp.minimum(pos + 1, w).astype(F32)
        pooled = wsum / count - z[:, c0:c1]
        yp = _dot(pooled.astype(BF16), wpool_ref[gi]) * pscale_ref[:, c0:c1]
        ybuf[:, c0:c1] = yp.astype(BF16)
    ubuf[0:POOL_HISTORY, :] = ubuf[TS:TS + POOL_HISTORY, :]

    kbuf[ATTN_BLOCK:ATTN_BLOCK + TS, :] = z[:, K_OFF:K_OFF + KV_WIDTH].astype(BF16)
    vbuf[ATTN_BLOCK:ATTN_BLOCK + TS, :] = z[:, V_OFF:V_OFF + KV_WIDTH].astype(BF16)
    lane = lax.broadcasted_iota(jnp.int32, (ATTN_BLOCK, LANES), 1)
    low_half = lane < HEAD_DIM
    high_half = jnp.logical_not(low_half)
    col = lax.broadcasted_iota(jnp.int32, (1, 2 * ATTN_BLOCK), 1)
    first_mask = jnp.where((col < ATTN_BLOCK) & (s == 0), NEG_INF, 0.0).astype(F32)
    for sb in range(TS // ATTN_BLOCK):
        r0 = sb * ATTN_BLOCK
        kband = kbuf[r0:r0 + 2 * ATTN_BLOCK, :]
        vband = vbuf[r0:r0 + 2 * ATTN_BLOCK, :]
        for p in range(N_HEADS // 2):
            qp = z[r0:r0 + ATTN_BLOCK, Q_OFF + p * LANES:Q_OFF + (p + 1) * LANES] * (HEAD_DIM ** -0.5)
            qr = pltpu.roll(qp, HEAD_DIM, axis=1)
            kvh = (2 * p) // HEADS_PER_KV
            kv_lanes = low_half if kvh == 0 else high_half
            outs = []
            for par in range(2):
                h = 2 * p + par
                qh = jnp.where(kv_lanes, qp if par == kvh else qr, 0.0).astype(BF16)
                lg = _dot_nt(qh, kband) + bias_buf[h * ATTN_BLOCK:(h + 1) * ATTN_BLOCK, :]
                if sb == 0:
                    lg = lg + first_mask
                sink = sinks_ref[h]
                m = jnp.maximum(jnp.max(lg, axis=-1, keepdims=True), sink)
                pe = jnp.exp(lg - m)
                den = jnp.sum(pe, axis=-1, keepdims=True) + jnp.exp(sink - m)
                outs.append(_dot(pe.astype(BF16), vband) / den)
            if kvh == 0:
                pair = jnp.where(low_half, outs[0], pltpu.roll(outs[1], HEAD_DIM, axis=1))
            else:
                pair = jnp.where(low_half, pltpu.roll(outs[0], HEAD_DIM, axis=1), outs[1])
            ybuf[r0:r0 + ATTN_BLOCK, POOL_WIDTH + p * LANES:POOL_WIDTH + (p + 1) * LANES] = pair.astype(BF16)
    kbuf[0:ATTN_BLOCK, :] = kbuf[TS:TS + ATTN_BLOCK, :]
    vbuf[0:ATTN_BLOCK, :] = vbuf[TS:TS + ATTN_BLOCK, :]

    h = x + _dot(ybuf[...], wout_ref[...])
    h_ref[...] = h
    h2 = _rms(h, gffn_ref[...])
    for c in range(ROW_CHUNKS):
        h2_ref[pl.ds(c, TS, stride=ROW_CHUNKS), :] = _pack_bf16_pair(
            h2[:, c * LANES:(c + 1) * LANES], h2[:, HALF_MODEL + c * LANES:HALF_MODEL + (c + 1) * LANES])

    logits = _dot_nt(wrt_ref[...], h2, precision=lax.Precision.HIGHEST)
    scores = _sigmoid(logits)
    biased = scores + rbias_ref[...]
    erow = lax.broadcasted_iota(jnp.int32, (N_EXPERTS, TS), 0)
    grow = lax.broadcasted_iota(jnp.int32, (EXPERTS_PER_GROUP, TS), 0)
    group_scores = []
    for g in range(N_EXPERT_GROUPS):
        blk = biased[g * EXPERTS_PER_GROUP:(g + 1) * EXPERTS_PER_GROUP, :]
        m1 = jnp.max(blk, axis=0, keepdims=True)
        i1 = jnp.min(jnp.where(blk == m1, grow, EXPERTS_PER_GROUP), axis=0, keepdims=True)
        m2 = jnp.max(jnp.where(grow == i1, NEG_INF, blk), axis=0, keepdims=True)
        group_scores.append(m1 + m2)
    cur = jnp.concatenate(group_scores, axis=0)
    gsel = jnp.zeros((N_EXPERT_GROUPS, TS), jnp.bool_)
    for _ in range(TOPK_GROUPS):
        m = jnp.max(cur, axis=0, keepdims=True)
        i = jnp.min(jnp.where(cur == m, grow, N_EXPERT_GROUPS), axis=0, keepdims=True)
        hit = grow == i
        gsel = jnp.logical_or(gsel, hit)
        cur = jnp.where(hit, NEG_INF, cur)
    gmask = jnp.concatenate(
        [jnp.broadcast_to(gsel[g:g + 1, :], (EXPERTS_PER_GROUP, TS)) for g in range(N_EXPERT_GROUPS)], axis=0)
    masked = jnp.where(gmask, biased, NEG_INF)
    sel = jnp.zeros((N_EXPERTS, TS), jnp.bool_)
    idxs, ws = [], []
    for _ in range(TOP_K):
        m = jnp.max(masked, axis=0, keepdims=True)
        i = jnp.min(jnp.where(masked == m, erow, N_EXPERTS), axis=0, keepdims=True)
        hit = erow == i
        idxs.append(i)
        ws.append(jnp.sum(jnp.where(hit, scores, 0.0), axis=0, keepdims=True))
        sel = jnp.logical_or(sel, hit)
        masked = jnp.where(hit, NEG_INF, masked)
    wtot = ws[0]
    for wk in ws[1:]:
        wtot = wtot + wk
    idx_ref[...] = jnp.concatenate(idxs, axis=0)
    topw_ref[...] = jnp.concatenate([wk / wtot * ROUTED_SCALE for wk in ws], axis=0)

    self32 = sel.astype(F32)
    ri = lax.broadcasted_iota(jnp.int32, (TS, TS), 0)
    ci = lax.broadcasted_iota(jnp.int32, (TS, TS), 1)
    before = (ri < ci).astype(BF16)
    running = _dot(self32.astype(BF16), before) + cnt_acc[:, 0:1]
    rank_ref[...] = jnp.concatenate(
        [jnp.sum(jnp.where(erow == i, running, 0.0), axis=0, keepdims=True) for i in idxs], axis=0).astype(jnp.int32)
    cnt_acc[...] = cnt_acc[...] + jnp.sum(self32, axis=1, keepdims=True)
    cnt_ref[...] = cnt_acc[...]


def _mix_call(x, g_mix, w_in, w_pool, pool_scale, sinks, rpe_table, bucket, w_out, g_ffn, w_rt, r_bias):
    batch, seq, _ = x.shape
    tokens = batch * seq
    TS = MIX_TOKENS
    ns = seq // TS
    const = lambda shape: pl.BlockSpec(shape, lambda b, s: (0,) * len(shape), pipeline_mode=pl.Buffered(1))
    smem = pl.BlockSpec(memory_space=pltpu.SMEM)
    tok3 = pl.BlockSpec((None, TS, D_MODEL), lambda b, s: (b, s, 0))
    lane_blk = pl.BlockSpec((TOP_K, TS), lambda b, s: (0, b * ns + s))
    return pl.pallas_call(
        _mix_kernel,
        grid=(batch, ns),
        in_specs=[
            tok3,
            const((1, D_MODEL)),
            const((D_MODEL, IN_WIDTH)),
            const((POOL_GROUPS, POOL_GROUP_DIM, POOL_GROUP_DIM)),
            const((1, POOL_WIDTH)),
            smem,
            smem,
            const((ATTN_BLOCK, 2 * ATTN_BLOCK)),
            const((MIX_WIDTH, D_MODEL)),
            const((1, D_MODEL)),
            const((N_EXPERTS, D_MODEL)),
            const((N_EXPERTS, 1)),
        ],
        out_specs=[
            pl.BlockSpec((None, TS, D_MODEL), lambda b, s: (b, s, 0), pipeline_mode=pl.Buffered(1)),
            pl.BlockSpec((TS * ROW_CHUNKS, LANES), lambda b, s: (b * ns + s, 0), pipeline_mode=pl.Buffered(1)),
            lane_blk,
            lane_blk,
            lane_blk,
            pl.BlockSpec((N_EXPERTS, LANES), lambda b, s: (0, 0)),
        ],
        out_shape=[
            jax.ShapeDtypeStruct((batch, seq, D_MODEL), F32),
            jax.ShapeDtypeStruct((tokens * ROW_CHUNKS, LANES), U32),
            jax.ShapeDtypeStruct((TOP_K, tokens), jnp.int32),
            jax.ShapeDtypeStruct((TOP_K, tokens), F32),
            jax.ShapeDtypeStruct((TOP_K, tokens), jnp.int32),
            jax.ShapeDtypeStruct((N_EXPERTS, LANES), F32),
        ],
        scratch_shapes=[
            pltpu.VMEM((POOL_HISTORY + TS, POOL_WIDTH), F32),
            pltpu.VMEM((ATTN_BLOCK + TS, KV_WIDTH), BF16),
            pltpu.VMEM((ATTN_BLOCK + TS, KV_WIDTH), BF16),
            pltpu.VMEM((N_HEADS * ATTN_BLOCK, 2 * ATTN_BLOCK), F32),
            pltpu.VMEM((TS, MIX_WIDTH), BF16),
            pltpu.VMEM((N_EXPERTS, LANES), F32),
        ],
        compiler_params=pltpu.CompilerParams(
            dimension_semantics=("arbitrary", "arbitrary"), vmem_limit_bytes=VMEM_LIMIT_BYTES),
        name="mix_router",
    )(x, g_mix, w_in, w_pool, pool_scale, sinks, rpe_table, bucket, w_out, g_ffn, w_rt, r_bias)


def _dispatch_kernel(dest_ref, poff_ref, pend_ref, h2_ref, xs_hbm, meta_ref, zbuf, sem, zsem):
    step = pl.program_id(0)
    base = step * DISPATCH_TOKENS

    def zero_copy(e):
        start = pl.multiple_of((pend_ref[e] - ROW_TILE) * ROW_CHUNKS, ROW_CHUNKS)
        return pltpu.make_async_copy(zbuf, xs_hbm.at[pl.ds(start, ROW_TILE * ROW_CHUNKS)], zsem)

    @pl.when(step == 0)
    def _zero_tails():
        zbuf[...] = jnp.zeros_like(zbuf)

        def start(e, c):
            @pl.when(pend_ref[e] > poff_ref[e])
            def _():
                zero_copy(e).start()
            return c

        def wait(e, c):
            @pl.when(pend_ref[e] > poff_ref[e])
            def _():
                zero_copy(e).wait()
            return c

        lax.fori_loop(0, N_EXPERTS, start, 0)
        lax.fori_loop(0, N_EXPERTS, wait, 0)

    def row_copy(t, row):
        src = pl.multiple_of(t * ROW_CHUNKS, ROW_CHUNKS)
        dest = pl.multiple_of(row * ROW_CHUNKS, ROW_CHUNKS)
        return pltpu.make_async_copy(h2_ref.at[pl.ds(src, ROW_CHUNKS)], xs_hbm.at[pl.ds(dest, ROW_CHUNKS)], sem)

    def start_tok(t, c):
        rows = [dest_ref[k, t] for k in range(TOP_K)]
        slot0 = (base + t) * TOP_K
        for k in range(TOP_K):
            row_copy(t, rows[k]).start(priority=k % 2)
        for k in range(TOP_K):
            meta_ref[rows[k]] = slot0 + k
        return c

    def wait_chunk(j, c):
        for _ in range(WAIT_CHUNK):
            row_copy(0, 0).wait()
        return c

    lax.fori_loop(0, DISPATCH_TOKENS, start_tok, 0, unroll=4)
    lax.fori_loop(0, DISPATCH_TOKENS * TOP_K // WAIT_CHUNK, wait_chunk, 0)


def _dispatch_call(dest3, poff, pend, h2, n_rows):
    smem_blk = pl.BlockSpec((None, TOP_K, DISPATCH_TOKENS), lambda i: (i, 0, 0), memory_space=pltpu.SMEM)
    smem = pl.BlockSpec(memory_space=pltpu.SMEM)
    return pl.pallas_call(
        _dispatch_kernel,
        grid=(dest3.shape[0],),
        in_specs=[smem_blk, smem, smem,
                  pl.BlockSpec((DISPATCH_TOKENS * ROW_CHUNKS, LANES), lambda i: (i, 0))],
        out_specs=[pl.BlockSpec(memory_space=pl.ANY), smem],
        out_shape=[jax.ShapeDtypeStruct((n_rows * ROW_CHUNKS, LANES), U32),
                   jax.ShapeDtypeStruct((n_rows,), jnp.int32)],
        scratch_shapes=[
            pltpu.VMEM((ROW_TILE * ROW_CHUNKS, LANES), U32),
            pltpu.SemaphoreType.DMA,
            pltpu.SemaphoreType.DMA,
        ],
        compiler_params=pltpu.CompilerParams(dimension_semantics=("arbitrary",)),
        name="dispatch_rows",
    )(dest3, poff, pend, h2)


def _expert_kernel(te_ref, tv_ref, tfirst_ref, tnext_ref, tslot_ref, tnv_ref, tlast_ref,
                   meta_ref, meta_prev_ref, xs_ref, wg_hbm, wu_hbm, wd_hbm, ys_hbm,
                   yo, xb_ref, wg_stage, wu_stage, wd_stage, wg_bf, wu_bf, wd_bf, wsems, ssems,
                   *, dump_row):
    i = pl.program_id(0)
    slot = lax.rem(i, 2)
    tile_rows = ROW_TILE * ROW_CHUNKS

    def scatter_copy(m_ref, n_valid, s, r):
        m = m_ref[0, r]
        dst = jnp.where(r < n_valid, (m + (m >> 3)) * ROW_CHUNKS, dump_row + r * ROW_CHUNKS)
        src = s * tile_rows + r * ROW_CHUNKS
        return pltpu.make_async_copy(yo.at[pl.ds(pl.multiple_of(src, ROW_CHUNKS), ROW_CHUNKS)],
                                     ys_hbm.at[pl.ds(pl.multiple_of(dst, ROW_CHUNKS), ROW_CHUNKS)], ssems.at[s])

    def start_rows(make, m_ref, n_valid, s):
        def body(r, c):
            make(m_ref, n_valid, s, r).start(priority=0)
            make(m_ref, n_valid, s, r + ROW_TILE // 2).start(priority=1)
            return c
        lax.fori_loop(0, ROW_TILE // 2, body, 0, unroll=8)

    def start_rows_inline(make, m_ref, n_valid, s):
        for r in range(ROW_TILE):
            make(m_ref, n_valid, s, r).start(priority=r % 2)

    def wait_rows(src_ref, dst_ref, sem):
        for _ in range(ROW_TILE):
            pltpu.make_async_copy(src_ref.at[pl.ds(0, ROW_CHUNKS)], dst_ref.at[pl.ds(0, ROW_CHUNKS)], sem).wait()

    def weight_copies(e, s):
        return (pltpu.make_async_copy(wg_hbm.at[e], wg_stage.at[s], wsems.at[s, 0]),
                pltpu.make_async_copy(wu_hbm.at[e], wu_stage.at[s], wsems.at[s, 1]),
                pltpu.make_async_copy(wd_hbm.at[e], wd_stage.at[s], wsems.at[s, 2]))

    valid = tv_ref[i] == 1

    @pl.when((i == 0) & valid)
    def _prologue():
        for cp in weight_copies(te_ref[0], tslot_ref[0]):
            cp.start(priority=WEIGHT_DMA_PRIORITY)
        yo[...] = jnp.zeros_like(yo)
        start_rows(scatter_copy, meta_ref, 0, 0)

    n_valid_prev = jnp.where(i >= 1, tnv_ref[jnp.maximum(i - 1, 0)], 0)

    @pl.when(valid & (tfirst_ref[i] == 1))
    def _new_expert():
        ws = tslot_ref[i]
        for cp in weight_copies(te_ref[i], ws):
            cp.wait()

        @pl.when(tnext_ref[i] >= 0)
        def _():
            for cp in weight_copies(tnext_ref[i], 1 - ws):
                cp.start(priority=WEIGHT_DMA_PRIORITY)

        def cast_in(j, c):
            r = pl.multiple_of(j * WEIGHT_CAST_ROWS, WEIGHT_CAST_ROWS)
            wg_bf[pl.ds(r, WEIGHT_CAST_ROWS), :] = wg_stage[ws, pl.ds(r, WEIGHT_CAST_ROWS), :].astype(BF16)
            wu_bf[pl.ds(r, WEIGHT_CAST_ROWS), :] = wu_stage[ws, pl.ds(r, WEIGHT_CAST_ROWS), :].astype(BF16)
            return c

        def cast_out(j, c):
            r = pl.multiple_of(j * (WEIGHT_CAST_ROWS // 4), WEIGHT_CAST_ROWS // 4)
            wd_bf[pl.ds(r, WEIGHT_CAST_ROWS // 4), :] = wd_stage[ws, pl.ds(r, WEIGHT_CAST_ROWS // 4), :].astype(BF16)
            return c

        lax.fori_loop(0, D_MODEL // WEIGHT_CAST_ROWS, cast_in, 0)
        lax.fori_loop(0, EXPERT_FF // (WEIGHT_CAST_ROWS // 4), cast_out, 0)

    @pl.when(valid)
    def _tile():
        base = pl.multiple_of(slot * tile_rows, tile_rows)
        for c in range(ROW_CHUNKS):
            lo, hi = _unpack_bf16_pair(xs_ref[pl.ds(c, ROW_TILE, stride=ROW_CHUNKS), :])
            xb_ref[:, c * LANES:(c + 1) * LANES] = lo.astype(BF16)
            xb_ref[:, HALF_MODEL + c * LANES:HALF_MODEL + (c + 1) * LANES] = hi.astype(BF16)
        start_rows_inline(scatter_copy, meta_prev_ref, n_valid_prev, 1 - slot)
        xb = xb_ref[...]
        g = _dot(xb, wg_bf[...])
        u = _dot(xb, wu_bf[...])
        hb = (g * _sigmoid(g) * u).astype(BF16)
        y = _dot(hb, wd_bf[...])
        wait_rows(yo, ys_hbm, ssems.at[slot])
        for c in range(ROW_CHUNKS):
            yo[pl.ds(base + c, ROW_TILE, stride=ROW_CHUNKS), :] = _pack_bf16_pair(
                y[:, c * LANES:(c + 1) * LANES], y[:, HALF_MODEL + c * LANES:HALF_MODEL + (c + 1) * LANES])

    @pl.when(valid & (tlast_ref[i] == 1))
    def _drain():
        start_rows(scatter_copy, meta_ref, tnv_ref[i], slot)
        wait_rows(yo, ys_hbm, ssems.at[1 - slot])
        wait_rows(yo, ys_hbm, ssems.at[slot])


def _expert_call(tile_meta, tile_b, meta, xs, w_gate, w_up, w_down, tokens):
    n_tiles = tile_b.shape[0]
    meta3 = meta.reshape(n_tiles, 1, ROW_TILE)
    dump_row = tokens * TOKEN_PITCH
    meta_blk = pl.BlockSpec((None, 1, ROW_TILE), lambda i, te, tv, tf, tn, ts, tnv, tl, tb: (tb[i], 0, 0),
                            memory_space=pltpu.SMEM)
    row_blk = pl.BlockSpec((ROW_TILE * ROW_CHUNKS, LANES), lambda i, te, tv, tf, tn, ts, tnv, tl, tb: (tb[i], 0))
    meta_prev_blk = pl.BlockSpec(
        (None, 1, ROW_TILE), lambda i, te, tv, tf, tn, ts, tnv, tl, tb: (tb[jnp.maximum(i - 1, 0)], 0, 0),
        memory_space=pltpu.SMEM)
    any_spec = pl.BlockSpec(memory_space=pl.ANY)

    def body(te, tv, tf, tn, ts, tnv, tl, tb, *refs):
        _expert_kernel(te, tv, tf, tn, ts, tnv, tl, *refs, dump_row=dump_row)

    return pl.pallas_call(
        body,
        grid_spec=pltpu.PrefetchScalarGridSpec(
            num_scalar_prefetch=len(tile_meta) + 1,
            grid=(n_tiles,),
            in_specs=[meta_blk, meta_prev_blk, row_blk, any_spec, any_spec, any_spec],
            out_specs=any_spec,
            scratch_shapes=[
                pltpu.VMEM((2 * ROW_TILE * ROW_CHUNKS, LANES), U32),
                pltpu.VMEM((ROW_TILE, D_MODEL), BF16),
                pltpu.VMEM((2, D_MODEL, EXPERT_FF), F32),
                pltpu.VMEM((2, D_MODEL, EXPERT_FF), F32),
                pltpu.VMEM((2, EXPERT_FF, D_MODEL), F32),
                pltpu.VMEM((D_MODEL, EXPERT_FF), BF16),
                pltpu.VMEM((D_MODEL, EXPERT_FF), BF16),
                pltpu.VMEM((EXPERT_FF, D_MODEL), BF16),
                pltpu.SemaphoreType.DMA((2, 3)),
                pltpu.SemaphoreType.DMA((2,)),
            ],
        ),
        out_shape=jax.ShapeDtypeStruct((dump_row + ROW_TILE * ROW_CHUNKS, LANES), U32),
        compiler_params=pltpu.CompilerParams(
            dimension_semantics=("arbitrary",), vmem_limit_bytes=VMEM_LIMIT_BYTES),
        name="routed_experts",
    )(*tile_meta, tile_b, meta3, meta3, xs, w_gate, w_up, w_down)


def _final_kernel(h_ref, gffn_ref, p_ref, wt_ref, ys_ref,
                  wsg_ref, wsu_ref, wsd_ref, wpp_ref, gple_ref, wpg_ref, gfin_ref,
                  out_ref, h3_ref):
    TB = FINAL_TOKENS
    h = h_ref[...]
    h2b = _rms(h, gffn_ref[...]).astype(BF16)
    g = _dot(h2b, wsg_ref[...])
    u = _dot(h2b, wsu_ref[...])
    shared = _dot((g * _sigmoid(g) * u).astype(BF16), wsd_ref[...])
    ple = _rms(_dot(p_ref[...].astype(BF16), wpp_ref[...]), gple_ref[...])

    wt = wt_ref[...]
    for c in range(ROW_CHUNKS):
        routed_lo = routed_hi = None
        for k in range(TOP_K):
            lo, hi = _unpack_bf16_pair(ys_ref[pl.ds(k * ROW_CHUNKS + c, TB, stride=TOKEN_PITCH), :])
            wk = wt[:, k:k + 1]
            routed_lo = lo * wk if k == 0 else routed_lo + lo * wk
            routed_hi = hi * wk if k == 0 else routed_hi + hi * wk
        for routed, c0 in ((routed_lo, c * LANES), (routed_hi, HALF_MODEL + c * LANES)):
            cs = slice(c0, c0 + LANES)
            h3_ref[:, cs] = h[:, cs] + (routed + shared[:, cs])

    h3 = h3_ref[...]
    gate = _sigmoid(_dot(h3.astype(BF16), wpg_ref[...]))
    h4 = h3 + gate * ple
    out_ref[...] = _rms(h4, gfin_ref[...])


def _final_call(h, g_ffn, p, wt, ys, ws_gate, ws_up, ws_down, w_pp, g_ple, w_pg, g_final):
    TB = FINAL_TOKENS
    tokens = h.shape[0]
    const = lambda shape: pl.BlockSpec(shape, lambda i: (0,) * len(shape), pipeline_mode=pl.Buffered(1))
    tok = pl.BlockSpec((TB, D_MODEL), lambda i: (i, 0))
    return pl.pallas_call(
        _final_kernel,
        grid=(tokens // TB,),
        in_specs=[
            tok,
            const((1, D_MODEL)),
            pl.BlockSpec((TB, PLE_DIM), lambda i: (i, 0)),
            pl.BlockSpec((TB, TOP_K), lambda i: (i, 0)),
            pl.BlockSpec((TB * TOKEN_PITCH, LANES), lambda i: (i, 0)),
            const((D_MODEL, SHARED_FF)),
            const((D_MODEL, SHARED_FF)),
            const((SHARED_FF, D_MODEL)),
            const((PLE_DIM, D_MODEL)),
            const((1, D_MODEL)),
            const((D_MODEL, D_MODEL)),
            const((1, D_MODEL)),
        ],
        out_specs=tok,
        out_shape=jax.ShapeDtypeStruct((tokens, D_MODEL), F32),
        scratch_shapes=[pltpu.VMEM((TB, D_MODEL), F32)],
        compiler_params=pltpu.CompilerParams(
            dimension_semantics=("arbitrary",), vmem_limit_bytes=VMEM_LIMIT_BYTES),
        name="combine_final",
    )(h, g_ffn, p, wt, ys, ws_gate, ws_up, ws_down, w_pp, g_ple, w_pg, g_final)


def _per_step(a, tokens_per_step):
    tokens = a.shape[1]
    return a.reshape(TOP_K, tokens // tokens_per_step, tokens_per_step).transpose(1, 0, 2)


def _tile_metadata(counts, n_tiles_max):
    i32 = jnp.int32
    padded = (counts + ROW_TILE - 1) // ROW_TILE * ROW_TILE
    pend = jnp.cumsum(padded).astype(i32)
    poff = pend - padded
    n_tiles = pend[-1] // ROW_TILE
    tile = jnp.arange(n_tiles_max, dtype=i32)
    tile_v = (tile < n_tiles).astype(i32)
    tile_b = jnp.minimum(tile, jnp.maximum(n_tiles - 1, 0))
    tile_e = jnp.minimum(jnp.sum(pend[None, :] <= (tile_b * ROW_TILE)[:, None], axis=1), N_EXPERTS - 1).astype(i32)
    onehot = tile_e[:, None] == jnp.arange(N_EXPERTS, dtype=i32)[None, :]

    def lookup(table):
        return jnp.sum(jnp.where(onehot, table[None, :], 0), axis=1).astype(i32)

    tile_first = (tile_v * (tile_b * ROW_TILE == lookup(poff))).astype(i32)
    nonempty = counts > 0
    order = jnp.cumsum(nonempty.astype(i32)) - 1
    ids = jnp.where(nonempty, jnp.arange(N_EXPERTS, dtype=i32), N_EXPERTS)
    later = jnp.flip(lax.cummin(jnp.flip(ids)))
    next_e = jnp.concatenate([later[1:], jnp.full((1,), N_EXPERTS, i32)])
    next_e = jnp.where(next_e == N_EXPERTS, -1, next_e)
    tile_next = lookup(next_e)
    tile_slot = lookup(order % 2)
    tile_nvalid = jnp.clip(lookup(poff + counts) - tile_b * ROW_TILE, 0, ROW_TILE).astype(i32)
    tile_last = (tile_v * (tile == n_tiles - 1)).astype(i32)
    return poff, pend, tile_b, (tile_e, tile_v, tile_first, tile_next, tile_slot, tile_nvalid, tile_last)


def kernel(x, p, g_mix, w_in, w_pool, pool_scale, attn_sinks, rpe_table, w_out, g_ffn, w_router, router_bias,
           w_gate, w_up, w_down, ws_gate, ws_up, ws_down, w_ple_proj, g_ple, w_ple_gate, g_final):
    batch, seq, _ = x.shape
    tokens = batch * seq
    assert seq % MIX_TOKENS == 0 and tokens % DISPATCH_TOKENS == 0 and tokens % FINAL_TOKENS == 0
    n_rows = tokens * TOP_K + N_EXPERTS * ROW_TILE
    bucket = jnp.asarray(_rpe_bucket_map())
    h, h2, idx, topw, rank, cnt = _mix_call(
        x, g_mix[0][None, :], w_in[0].astype(BF16), w_pool[0].astype(BF16), pool_scale[0][None, :],
        attn_sinks[0], rpe_table, bucket, w_out[0].astype(BF16), g_ffn[0][None, :],
        w_router[0].T, router_bias[0][:, None])

    poff, pend, tile_b, tile_meta = _tile_metadata(cnt[:, 0].astype(jnp.int32), n_rows // ROW_TILE)
    experts = jnp.arange(N_EXPERTS, dtype=jnp.int32)[:, None, None]
    dest = jnp.sum(jnp.where(idx[None] == experts, poff[:, None, None], 0), axis=0) + rank
    xs, meta = _dispatch_call(_per_step(dest, DISPATCH_TOKENS), poff, pend, h2, n_rows)
    ys = _expert_call(tile_meta, tile_b, meta, xs, w_gate[0], w_up[0], w_down[0], tokens)
    out = _final_call(
        h.reshape(tokens, D_MODEL), g_ffn[0][None, :], p[0].reshape(tokens, PLE_DIM), topw.T, ys,
        ws_gate[0].astype(BF16), ws_up[0].astype(BF16), ws_down[0].astype(BF16),
        w_ple_proj[0].astype(BF16), g_ple[0][None, :], w_ple_gate[0].astype(BF16), g_final[None, :])
    return out.reshape(batch, seq, D_MODEL)
```
